```python
import math
import jax, jax.numpy as jnp
from jax import lax
import numpy as np

D_MODEL = 1024
BATCH = 32
SEQ = 2048
DEPTH = 1
DEC_BATCH = 32
DEC_SEQ = 32
PAST_LEN = 4096

CHUNK = 64
A_HEADS = 16
A_KV_HEADS = 2
A_HEAD_DIM = 64
A_GROUP = A_HEADS // A_KV_HEADS
WINDOW = 128
WIN_CHUNKS = WINDOW // CHUNK
A_Q_W = A_HEADS * A_HEAD_DIM
A_KV_W = A_KV_HEADS * A_HEAD_DIM
NUM_BUCKETS = 32
MAX_DISTANCE = 128
B_HEADS = 8
B_KEY_DIM = 128
B_VAL_DIM = D_MODEL // B_HEADS
B_KEY_W = B_HEADS * B_KEY_DIM
B_VAL_W = B_HEADS * B_VAL_DIM
D_FF = 2816
CONV_W = 3
PLE_DIM = 256
EPS = 1e-6
NEG_INF = -1e30

_SPLIT_SIZES = (A_Q_W, A_KV_W, A_KV_W, B_KEY_W, B_KEY_W, B_VAL_W, B_VAL_W, D_MODEL, D_MODEL)
IN_COLS = sum(_SPLIT_SIZES)
SPLIT_POINTS = tuple(int(s) for s in np.cumsum(_SPLIT_SIZES)[:-1])

kernel_name = "hybrid_swa_hgrn2_convffn_stream_step"


def _rms_norm(x, g):
    xf = x.astype(jnp.float32)
    y = xf * lax.rsqrt(jnp.mean(xf * xf, axis=-1, keepdims=True) + EPS)
    return (y * g.astype(jnp.float32)).astype(x.dtype)


def _t5_bucket(rel):
    nb = NUM_BUCKETS // 2
    ret = jnp.where(rel > 0, nb, 0)
    n = jnp.abs(rel)
    max_exact = nb // 2
    large = max_exact + (jnp.log(jnp.maximum(n, max_exact).astype(jnp.float32) / max_exact)
                         / math.log(MAX_DISTANCE / max_exact) * (nb - max_exact)).astype(jnp.int32)
    large = jnp.minimum(large, nb - 1)
    return ret + jnp.where(n < max_exact, n, large)


def _rel_bias(table, q_pos, k_pos):
    buckets = _t5_bucket(k_pos[None, :] - q_pos[:, None])
    b = jnp.transpose(table[buckets], (2, 0, 1))
    return b.reshape(A_KV_HEADS, A_GROUP, q_pos.shape[0], k_pos.shape[0])


def _sink_attention(q, k, v, bias, mask, sinks):
    s = jnp.einsum('bnqkgd,bnskd->bnkgqs', q, k).astype(jnp.float32) * (A_HEAD_DIM ** -0.5)
    s = s + bias.astype(jnp.float32)[None, None]
    s = jnp.where(mask[None, :, None, None], s, NEG_INF)
    sink = sinks.astype(jnp.float32)[None, None, :, :, None, None]
    m = jnp.maximum(jnp.max(s, axis=-1, keepdims=True), sink)
    p = jnp.exp(s - m)
    denom = jnp.sum(p, axis=-1, keepdims=True) + jnp.exp(sink - m)
    return jnp.einsum('bnkgqs,bnskd->bnqkgd', (p / denom).astype(v.dtype), v)


def _hgrn2(q, f_logit, i_in, s0, lb, block):
    f32 = jnp.float32
    f = lb + (1.0 - lb) * jax.nn.sigmoid(f_logit.astype(f32))
    logf = jnp.log(f)
    k = 1.0 - f
    q = jax.nn.silu(q.astype(f32))
    v = i_in.astype(f32)
    bsz, t, h, dk = q.shape
    dv = v.shape[-1]
    nb = t // block

    def to_blocks(a):
        return a.reshape(bsz, nb, block, h, a.shape[-1]).transpose(1, 0, 3, 2, 4)

    qc, kc, vc = to_blocks(q), to_blocks(k), to_blocks(v)
    cum = jnp.cumsum(to_blocks(logf), axis=3)
    causal = jnp.tril(jnp.ones((block, block), dtype=bool))
    mid = block // 2

    def step(state, blk):
        q_, k_, v_, b_ = blk
        b_last = b_[:, :, -1:, :]
        b_mid = b_[:, :, mid:mid + 1, :]
        o = jnp.einsum('bhld,bhdv->bhlv', q_ * jnp.exp(b_), state)
        a = jnp.einsum('bhtd,bhsd->bhts', q_ * jnp.exp(b_ - b_mid), k_ * jnp.exp(b_mid - b_))
        o = o + jnp.einsum('bhts,bhsv->bhtv', jnp.where(causal, a, 0.0), v_)
        state = (jnp.exp(b_last)[:, :, 0, :, None] * state
                 + jnp.einsum('bhsd,bhsv->bhdv', k_ * jnp.exp(b_last - b_), v_))
        return state, o

    s_fin, o = lax.scan(step, s0.astype(f32), (qc, kc, vc, cum))
    o = o.transpose(1, 0, 3, 2, 4).reshape(bsz, t, h, dv)
    return o, s_fin


def _layer(x, pe, k_prev, v_prev, s_prev, conv_prev, rel_table, lb,
           g_pre_mix, w_in, sinks, g_hgrn_out, w_br_a, w_br_b, w_out, g_post_mix,
           g_pre_ffn, w_up, w_conv, b_conv, w_down, g_post_ffn, w_ple, w_ple_gate):
    prompt = k_prev is None
    bsz, t, _ = x.shape
    h = _rms_norm(x, g_pre_mix)
    z = h @ w_in
    qa, ka, va, qb, fb, ib, ob, ga, gb = jnp.split(z, SPLIT_POINTS, axis=-1)
    qa = qa.reshape(bsz, t, A_KV_HEADS, A_GROUP, A_HEAD_DIM)
    ka = ka.reshape(bsz, t, A_KV_HEADS, A_HEAD_DIM)
    va = va.reshape(bsz, t, A_KV_HEADS, A_HEAD_DIM)

    if prompt:
        nc = t // CHUNK
        lk = (WIN_CHUNKS + 1) * CHUNK
        qblk = qa.reshape(bsz, nc, CHUNK, A_KV_HEADS, A_GROUP, A_HEAD_DIM)
        pad = ((0, 0), (WIN_CHUNKS * CHUNK, 0), (0, 0), (0, 0))
        kp = jnp.pad(ka, pad).reshape(bsz, nc + WIN_CHUNKS, CHUNK, A_KV_HEADS, A_HEAD_DIM)
        vp = jnp.pad(va, pad).reshape(bsz, nc + WIN_CHUNKS, CHUNK, A_KV_HEADS, A_HEAD_DIM)
        kblk = jnp.concatenate([kp[:, j:j + nc] for j in range(WIN_CHUNKS + 1)], axis=2)
        vblk = jnp.concatenate([vp[:, j:j + nc] for j in range(WIN_CHUNKS + 1)], axis=2)
        q_pos = jnp.arange(CHUNK) + WIN_CHUNKS * CHUNK
        k_pos = jnp.arange(lk)
        valid = (jnp.arange(nc)[:, None] - WIN_CHUNKS + (k_pos // CHUNK)[None, :]) >= 0
        mask = jnp.broadcast_to(valid[:, None, :], (nc, CHUNK, lk))
        keep = min(WINDOW, t)
        new_k, new_v = ka[:, t - keep:], va[:, t - keep:]
    else:
        lc = k_prev.shape[1]
        qblk = qa[:, None]
        kblk = jnp.concatenate([k_prev.astype(ka.dtype), ka], axis=1)[:, None]
        vblk = jnp.concatenate([v_prev.astype(va.dtype), va], axis=1)[:, None]
        q_pos = jnp.arange(t) + lc
        k_pos = jnp.arange(lc + t)
        mask = jnp.ones((1, t, lc + t), dtype=bool)
        new_k, new_v = ka, va
    bias = _rel_bias(rel_table, q_pos, k_pos)
    ya = _sink_attention(qblk, kblk, vblk, bias, mask, sinks.reshape(A_KV_HEADS, A_GROUP))
    ya = ya.reshape(bsz, t, A_Q_W)

    s0 = jnp.zeros((bsz, B_HEADS, B_KEY_DIM, B_VAL_DIM), jnp.float32) if prompt else s_prev
    block = CHUNK if prompt else t
    yb, s_fin = _hgrn2(qb.reshape(bsz, t, B_HEADS, B_KEY_DIM), fb.reshape(bsz, t, B_HEADS, B_KEY_DIM),
                       ib.reshape(bsz, t, B_HEADS, B_VAL_DIM), s0, lb, block)
    yb = _rms_norm(yb.astype(x.dtype), g_hgrn_out) * jax.nn.silu(ob.reshape(bsz, t, B_HEADS, B_VAL_DIM))
    yb = yb.reshape(bsz, t, B_VAL_W)

    mix = jax.nn.sigmoid(ga) * (ya @ w_br_a) + jax.nn.sigmoid(gb) * (yb @ w_br_b)
    x = x + _rms_norm(mix @ w_out, g_post_mix)

    hf = _rms_norm(x, g_pre_ffn)
    a, u = jnp.split(hf @ w_up, 2, axis=-1)
    prev = jnp.zeros((bsz, CONV_W - 1, D_FF), a.dtype) if prompt else conv_prev.astype(a.dtype)
    ap = jnp.concatenate([prev, a], axis=1)
    ac = b_conv
    for j in range(CONV_W):
        ac = ac + ap[:, j:j + t] * w_conv[j]
    ffn = (jax.nn.gelu(ac, approximate=True) * u) @ w_down
    x = x + _rms_norm(ffn, g_post_ffn)
    conv_tail = ap[:, ap.shape[1] - (CONV_W - 1):]

    x = x + (pe @ w_ple) * jax.nn.sigmoid(x @ w_ple_gate)
    return x, new_k, new_v, s_fin.astype(x.dtype), conv_tail


def setup_inputs(seed: int = 0) -> dict:
    key = jax.random.key(seed)
    ks = jax.random.split(key, 32)
    f32 = jnp.float32

    def nrm(k, shape, scale):
        return jax.random.normal(k, shape, f32) * scale

    def gain(k, shape):
        return 1.0 + 0.05 * jax.random.normal(k, shape, f32)

    w_cache = min(WINDOW, PAST_LEN)
    return {
        "x_prompt": nrm(ks[0], (BATCH, SEQ, D_MODEL), 1.0),
        "x_sample": nrm(ks[1], (DEC_BATCH, DEC_SEQ, D_MODEL), 1.0),
        "cache_win_k": nrm(ks[2], (DEPTH, DEC_BATCH, w_cache, A_KV_HEADS, A_HEAD_DIM), 1.0),
        "cache_win_v": nrm(ks[3], (DEPTH, DEC_BATCH, w_cache, A_KV_HEADS, A_HEAD_DIM), 1.0),
        "state_hgrn": nrm(ks[4], (DEPTH, DEC_BATCH, B_HEADS, B_KEY_DIM, B_VAL_DIM), 0.5),
        "cache_ffn_conv": nrm(ks[5], (DEPTH, DEC_BATCH, CONV_W - 1, D_FF), 1.0),
        "p_prompt": nrm(ks[6], (DEPTH, BATCH, SEQ, PLE_DIM), 1.0),
        "p_sample": nrm(ks[7], (DEPTH, DEC_BATCH, DEC_SEQ, PLE_DIM), 1.0),
        "rel_bias_table": nrm(ks[8], (NUM_BUCKETS, A_HEADS), 0.5),
        "lb_logits": nrm(ks[9], (DEPTH + 1, B_KEY_W), 0.5),
        "g_pre_mix": gain(ks[10], (DEPTH, D_MODEL)),
        "w_in": nrm(ks[11], (DEPTH, D_MODEL, IN_COLS), D_MODEL ** -0.5),
        "attn_sinks": nrm(ks[12], (DEPTH, A_HEADS), 0.5),
        "g_hgrn_out": gain(ks[13], (DEPTH, B_VAL_DIM)),
        "w_br_a": nrm(ks[14], (DEPTH, A_Q_W, D_MODEL), A_Q_W ** -0.5),
        "w_br_b": nrm(ks[15], (DEPTH, B_VAL_W, D_MODEL), B_VAL_W ** -0.5),
        "w_out": nrm(ks[16], (DEPTH, D_MODEL, D_MODEL), D_MODEL ** -0.5),
        "g_post_mix": gain(ks[17], (DEPTH, D_MODEL)),
        "g_pre_ffn": gain(ks[18], (DEPTH, D_MODEL)),
        "w_up": nrm(ks[19], (DEPTH, D_MODEL, 2 * D_FF), D_MODEL ** -0.5),
        "w_conv": nrm(ks[20], (DEPTH, CONV_W, D_FF), CONV_W ** -0.5),
        "b_conv": nrm(ks[21], (DEPTH, D_FF), 0.02),
        "w_down": nrm(ks[22], (DEPTH, D_FF, D_MODEL), D_FF ** -0.5),
        "g_post_ffn": gain(ks[23], (DEPTH, D_MODEL)),
        "w_ple": nrm(ks[24], (DEPTH, PLE_DIM, D_MODEL), PLE_DIM ** -0.5),
        "w_ple_gate": nrm(ks[25], (DEPTH, D_MODEL, D_MODEL), D_MODEL ** -0.5),
    }


def reference(x_prompt, x_sample, cache_win_k, cache_win_v, state_hgrn, cache_ffn_conv,
              p_prompt, p_sample, rel_bias_table, lb_logits, g_pre_mix, w_in, attn_sinks,
              g_hgrn_out, w_br_a, w_br_b, w_out, g_post_mix, g_pre_ffn, w_up, w_conv, b_conv,
              w_down, g_post_ffn, w_ple, w_ple_gate):
    lbs = jnp.cumsum(jax.nn.softmax(lb_logits.astype(jnp.float32), axis=0), axis=0)
    yp, ys = x_prompt, x_sample
    outs_p, outs_s = [], []
    for i in range(DEPTH):
        lw = (g_pre_mix[i], w_in[i], attn_sinks[i], g_hgrn_out[i], w_br_a[i], w_br_b[i], w_out[i],
              g_post_mix[i], g_pre_ffn[i], w_up[i], w_conv[i], b_conv[i], w_down[i], g_post_ffn[i],
              w_ple[i], w_ple_gate[i])
        lb = lbs[i].reshape(B_HEADS, B_KEY_DIM)
        yp, pk, pv, ps, pc = _layer(yp, p_prompt[i], None, None, None, None, rel_bias_table, lb, *lw)
        ys, sk, sv, ss, sc = _layer(ys, p_sample[i], cache_win_k[i], cache_win_v[i], state_hgrn[i],
                                    cache_ffn_conv[i], rel_bias_table, lb, *lw)
        outs_p.append((pk, pv, ps, pc))
        outs_s.append((sk, sv, ss, sc))
    prompt_win_k = jnp.stack([o[0] for o in outs_p])
    prompt_win_v = jnp.stack([o[1] for o in outs_p])
    prompt_hgrn_state = jnp.stack([o[2] for o in outs_p])
    prompt_ffn_conv = jnp.stack([o[3] for o in outs_p])
    sample_win_k = jnp.stack([o[0] for o in outs_s])
    sample_win_v = jnp.stack([o[1] for o in outs_s])
    sample_hgrn_state = jnp.stack([o[2] for o in outs_s])
    sample_ffn_conv = jnp.stack([o[3] for o in outs_s])
    return (yp, ys, prompt_win_k, prompt_win_v, prompt_hgrn_state, prompt_ffn_conv,
            sample_win_k, sample_win_v, sample_hgrn_state, sample_ffn_conv)
```

```python
import functools
import math

import jax
import jax.numpy as jnp
from jax import lax
from jax.experimental import pallas as pl
from jax.experimental.pallas import tpu as pltpu

D_MODEL = 1024
CHUNK = 64
A_HEADS = 16
A_KV_HEADS = 2
A_HEAD_DIM = 64
A_GROUP = A_HEADS // A_KV_HEADS
WINDOW = 128
A_Q_W = A_HEADS * A_HEAD_DIM
A_KV_W = A_KV_HEADS * A_HEAD_DIM
NUM_BUCKETS = 32
MAX_DISTANCE = 128
B_HEADS = 8
B_KEY_DIM = 128
B_VAL_DIM = D_MODEL // B_HEADS
B_KEY_W = B_HEADS * B_KEY_DIM
B_VAL_W = B_HEADS * B_VAL_DIM
D_FF = 2816
CONV_W = 3
PLE_DIM = 256
EPS = 1e-6
NEG_INF = -1e30

_QA0 = 0
_KV0 = A_Q_W
_HB0 = _KV0 + 2 * A_KV_W
_GG0 = _HB0 + 2 * B_KEY_W + 2 * B_VAL_W
IN_COLS = _GG0 + 2 * D_MODEL

_SCORE_W = 256

_VMEM_LIMIT = 56 * 1024 * 1024

BF16 = jnp.bfloat16
F32 = jnp.float32


def _const_spec(shape):
    nd = len(shape)
    return pl.BlockSpec(shape, lambda *_: (0,) * nd, pipeline_mode=pl.Buffered(1))


def _rms(x, g):
    ms = jnp.mean(x * x, axis=-1, keepdims=True)
    return x * lax.rsqrt(ms + EPS) * g


def _sigmoid(x):
    return 1.0 / (1.0 + jnp.exp(-x))


def _dot(a, b):
    return jnp.dot(a, b, preferred_element_type=F32)


def _dot_nt(a, b):
    return lax.dot_general(a, b, (((1,), (1,)), ((), ())), preferred_element_type=F32)


def _dot_tn(a, b):
    return lax.dot_general(a, b, (((0,), (0,)), ((), ())), preferred_element_type=F32)


def _inproj_body(x_ref, g_ref, w_ref, qa_ref, kv_ref, hb_ref, gg_ref):
    h = _rms(x_ref[...], g_ref[...]).astype(BF16)
    step = 512

    def mm(lo, width):
        return _dot(h, w_ref[:, lo:lo + width])

    for j in range(0, A_Q_W, step):
        qa_ref[:, j:j + step] = mm(_QA0 + j, step).astype(BF16)
    kv_ref[...] = mm(_KV0, 2 * A_KV_W)
    for j in range(0, _GG0 - _HB0, step):
        hb_ref[:, j:j + step] = mm(_HB0 + j, step)
    for j in range(0, 2 * D_MODEL, step):
        gg_ref[:, j:j + step] = mm(_GG0 + j, step)


def _inproj(x2d, g, w_bf, tm):
    n = x2d.shape[0]
    row = lambda w: pl.BlockSpec((tm, w), lambda i: (i, 0))
    return pl.pallas_call(
        _inproj_body,
        grid=(n // tm,),
        in_specs=[row(D_MODEL), _const_spec((1, D_MODEL)), _const_spec((D_MODEL, IN_COLS))],
        out_specs=[row(A_Q_W), row(2 * A_KV_W), row(_GG0 - _HB0), row(2 * D_MODEL)],
        out_shape=[
            jax.ShapeDtypeStruct((n, A_Q_W), BF16),
            jax.ShapeDtypeStruct((n, 2 * A_KV_W), F32),
            jax.ShapeDtypeStruct((n, _GG0 - _HB0), F32),
            jax.ShapeDtypeStruct((n, 2 * D_MODEL), F32),
        ],
        compiler_params=pltpu.CompilerParams(
            dimension_semantics=("parallel",), vmem_limit_bytes=_VMEM_LIMIT),
        name="inproj",
    )(x2d, g, w_bf)


def _attn_body(q_ref, kvc_ref, kvp_ref, bias_ref, o_ref, kv_s, *, lq, lk, nchunk, tq, masked):
    i = pl.program_id(1)
    kv_s[0:WINDOW, :] = kvp_ref[...].astype(BF16)
    kv_s[WINDOW:WINDOW + tq, :] = kvc_ref[...].astype(BF16)
    kv_s[WINDOW + tq:, :] = jnp.zeros((kv_s.shape[0] - WINDOW - tq, 2 * A_KV_W), BF16)
    rows = A_GROUP * lq
    col = lax.broadcasted_iota(jnp.int32, (rows, _SCORE_W), 1)
    scale = A_HEAD_DIM ** -0.5

    def chunk(c, carry):
        r0 = pl.multiple_of(c * lq, lq)
        qc = q_ref[pl.ds(r0, lq), :]
        kw = kv_s[pl.ds(r0, _SCORE_W), :]
        if masked:
            thr = jnp.maximum(WINDOW // CHUNK - (i * nchunk + c), 0) * CHUNK
        outs = []
        for k in range(A_KV_HEADS):
            qs = jnp.concatenate(
                [qc[:, (k * A_GROUP + g) * A_HEAD_DIM:(k * A_GROUP + g + 1) * A_HEAD_DIM]
                 for g in range(A_GROUP)], axis=0)
            kk = kw[:, k * A_HEAD_DIM:(k + 1) * A_HEAD_DIM]
            s = _dot_nt(qs, kk)
            t = jnp.where(col < lk, s * scale, 0.0) + bias_ref[k]
            if masked:
                t = jnp.where(col >= thr, t, NEG_INF)
            m = jnp.max(t, axis=-1, keepdims=True)
            p = jnp.exp(t - m)
            den = jnp.sum(p, axis=-1, keepdims=True)
            pz = jnp.where(col < lk, p, 0.0).astype(BF16)
            vv = kw[:, A_KV_W + k * A_HEAD_DIM:A_KV_W + (k + 1) * A_HEAD_DIM]
            o = _dot(pz, vv) * (1.0 / den)
            outs.append(jnp.concatenate([o[g * lq:(g + 1) * lq, :] for g in range(A_GROUP)], axis=1))
        o_ref[pl.ds(r0, lq), :] = jnp.concatenate(outs, axis=1).astype(BF16)
        return carry

    lax.fori_loop(0, nchunk, chunk, 0)


def _attention(qa, kv, kv_prev, bias_ext, *, lq, tq, masked):
    bsz, t, _ = qa.shape
    lk = WINDOW + lq
    nchunk = tq // lq
    if kv_prev is None:
        per = tq // WINDOW
        prev_arr = kv
        prev_spec = pl.BlockSpec((None, WINDOW, 2 * A_KV_W),
                                 lambda b, i: (b, jnp.maximum(i * per - 1, 0), 0))
    else:
        prev_arr = kv_prev
        prev_spec = pl.BlockSpec((None, WINDOW, 2 * A_KV_W), lambda b, i: (b, 0, 0))
    body = functools.partial(_attn_body, lq=lq, lk=lk, nchunk=nchunk, tq=tq, masked=masked)
    return pl.pallas_call(
        body,
        grid=(bsz, t // tq),
        in_specs=[
            pl.BlockSpec((None, tq, A_Q_W), lambda b, i: (b, i, 0)),
            pl.BlockSpec((None, tq, 2 * A_KV_W), lambda b, i: (b, i, 0)),
            prev_spec,
            _const_spec((A_KV_HEADS, A_GROUP * lq, _SCORE_W)),
        ],
        out_specs=pl.BlockSpec((None, tq, A_Q_W), lambda b, i: (b, i, 0)),
        out_shape=jax.ShapeDtypeStruct((bsz, t, A_Q_W), BF16),
        scratch_shapes=[pltpu.VMEM((WINDOW + tq + _SCORE_W - lk + lq, 2 * A_KV_W), BF16)],
        compiler_params=pltpu.CompilerParams(
            dimension_semantics=("parallel", "arbitrary"), vmem_limit_bytes=_VMEM_LIMIT),
        name="attention",
    )(qa, kv, prev_arr, bias_ext)


def _cumsum_rows(x):
    n = x.shape[0]
    row = lax.broadcasted_iota(jnp.int32, x.shape, 0)
    s = 1
    while s < n:
        x = x + jnp.where(row >= s, pltpu.roll(x, s, 0), 0.0)
        s *= 2
    return x


def _hgrn_body(*refs, blk, nchunk, has_state):
    if has_state:
        hb_ref, lbl_ref, g_ref, s0_ref, yb_ref, sfin_ref, st_s = refs
    else:
        hb_ref, lbl_ref, g_ref, yb_ref, sfin_ref, st_s = refs
    i = pl.program_id(1)
    nt = pl.num_programs(1)

    @pl.when(i == 0)
    def _():
        for h in range(B_HEADS):
            if has_state:
                st_s[h] = s0_ref[h].T
            else:
                st_s[h] = jnp.zeros((B_VAL_DIM, B_KEY_DIM), F32)

    lbl = lbl_ref[...]
    e = jnp.exp(lbl - jnp.max(lbl, axis=0, keepdims=True))
    lb = e[0:1, :] / jnp.sum(e, axis=0, keepdims=True)
    g = g_ref[...]
    ri = lax.broadcasted_iota(jnp.int32, (blk, blk), 0)
    ci = lax.broadcasted_iota(jnp.int32, (blk, blk), 1)
    causal = ri >= ci
    mid = blk // 2
    w = B_KEY_W

    def chunk(c, carry):
        r0 = pl.multiple_of(c * blk, blk)
        q = hb_ref[pl.ds(r0, blk), 0:w]
        fl = hb_ref[pl.ds(r0, blk), w:2 * w]
        v = hb_ref[pl.ds(r0, blk), 2 * w:3 * w]
        og = hb_ref[pl.ds(r0, blk), 3 * w:4 * w]
        f = lb + (1.0 - lb) * _sigmoid(fl)
        cum = _cumsum_rows(jnp.log(f))
        kk = 1.0 - f
        qs = q * _sigmoid(q)
        b_last = cum[blk - 1:blk, :]
        b_mid = cum[mid:mid + 1, :]
        q1 = (qs * jnp.exp(cum)).astype(BF16)
        q2 = (qs * jnp.exp(cum - b_mid)).astype(BF16)
        k2 = (kk * jnp.exp(b_mid - cum)).astype(BF16)
        k3 = (kk * jnp.exp(b_last - cum)).astype(BF16)
        vb = v.astype(BF16)
        dec = jnp.exp(b_last)
        ys = []
        for h in range(B_HEADS):
            sl = slice(h * B_KEY_DIM, (h + 1) * B_KEY_DIM)
            st = st_s[h]
            o = _dot_nt(q1[:, sl], st.astype(BF16))
            a = jnp.where(causal, _dot_nt(q2[:, sl], k2[:, sl]), 0.0)
            o = o + _dot(a.astype(BF16), vb[:, sl])
            st_s[h] = dec[:, sl] * st + _dot_tn(vb[:, sl], k3[:, sl])
            ys.append(_rms(o, g))
        y = jnp.concatenate(ys, axis=1) * (og * _sigmoid(og))
        yb_ref[pl.ds(r0, blk), :] = y.astype(BF16)
        return carry

    lax.fori_loop(0, nchunk, chunk, 0)

    @pl.when(i == nt - 1)
    def _():
        for h in range(B_HEADS):
            sfin_ref[h] = st_s[h].T


def _hgrn(hb, lb_logits, g_out, s0, *, blk, tc):
    bsz, t, _ = hb.shape
    has_state = s0 is not None
    body = functools.partial(_hgrn_body, blk=blk, nchunk=tc // blk, has_state=has_state)
    st_spec = pl.BlockSpec((None, B_HEADS, B_KEY_DIM, B_VAL_DIM), lambda b, i: (b, 0, 0, 0))
    in_specs = [
        pl.BlockSpec((None, tc, 4 * B_KEY_W), lambda b, i: (b, i, 0)),
        _const_spec(lb_logits.shape),
        _const_spec((1, B_VAL_DIM)),
    ]
    args = [hb, lb_logits, g_out]
    if has_state:
        in_specs.append(st_spec)
        args.append(s0)
    return pl.pallas_call(
        body,
        grid=(bsz, t // tc),
        in_specs=in_specs,
        out_specs=[pl.BlockSpec((None, tc, B_VAL_W), lambda b, i: (b, i, 0)), st_spec],
        out_shape=[jax.ShapeDtypeStruct((bsz, t, B_VAL_W), BF16),
                   jax.ShapeDtypeStruct((bsz, B_HEADS, B_KEY_DIM, B_VAL_DIM), F32)],
        scratch_shapes=[pltpu.VMEM((B_HEADS, B_VAL_DIM, B_KEY_DIM), F32)],
        compiler_params=pltpu.CompilerParams(
            dimension_semantics=("parallel", "arbitrary"), vmem_limit_bytes=_VMEM_LIMIT),
        name="hgrn2",
    )(*args)


def _merge_body(ya_ref, yb_ref, gg_ref, x_ref, wa_ref, wb_ref, wo_ref, g_ref, o_ref):
    ga = gg_ref[:, 0:D_MODEL]
    gb = gg_ref[:, D_MODEL:2 * D_MODEL]
    mix = _sigmoid(ga) * _dot(ya_ref[...], wa_ref[...]) + _sigmoid(gb) * _dot(yb_ref[...], wb_ref[...])
    r = _dot(mix.astype(BF16), wo_ref[...])
    o_ref[...] = x_ref[...] + _rms(r, g_ref[...])


def _merge(ya, yb, gg, x2d, wa, wb, wo, g, tm):
    n = x2d.shape[0]
    row = lambda w: pl.BlockSpec((tm, w), lambda i: (i, 0))
    wspec = _const_spec((D_MODEL, D_MODEL))
    return pl.pallas_call(
        _merge_body,
        grid=(n // tm,),
        in_specs=[row(A_Q_W), row(B_VAL_W), row(2 * D_MODEL), row(D_MODEL),
                  wspec, wspec, wspec, _const_spec((1, D_MODEL))],
        out_specs=row(D_MODEL),
        out_shape=jax.ShapeDtypeStruct((n, D_MODEL), F32),
        compiler_params=pltpu.CompilerParams(
            dimension_semantics=("parallel",), vmem_limit_bytes=_VMEM_LIMIT),
        name="merge",
    )(ya, yb, gg, x2d, wa, wb, wo, g)


def _gelu_tanh(x):
    c = math.sqrt(2.0 / math.pi)
    return 0.5 * x * (1.0 + jnp.tanh(c * (x + 0.044715 * (x * x * x))))


def _ffn_body(*refs, tm, has_prev):
    if has_prev:
        (x_ref, pe_ref, prev_ref, gpre_ref, wup_ref, wconv_ref, bconv_ref, wdown_ref, gpost_ref,
         wple_ref, wgate_ref, o_ref, tail_ref, carry_s) = refs
    else:
        (x_ref, pe_ref, gpre_ref, wup_ref, wconv_ref, bconv_ref, wdown_ref, gpost_ref,
         wple_ref, wgate_ref, o_ref, tail_ref, carry_s) = refs
    i = pl.program_id(1)

    @pl.when(i == 0)
    def _():
        if has_prev:
            carry_s[...] = prev_ref[...]
        else:
            carry_s[...] = jnp.zeros((CONV_W - 1, D_FF), F32)

    x = x_ref[...]
    hf = _rms(x, gpre_ref[...]).astype(BF16)
    a = _dot(hf, wup_ref[:, 0:D_FF])
    u = _dot(hf, wup_ref[:, D_FF:2 * D_FF])
    row = lax.broadcasted_iota(jnp.int32, (tm, D_FF), 0)
    c0 = carry_s[0:1, :]
    c1 = carry_s[1:2, :]
    a1 = jnp.where(row == 0, c1, pltpu.roll(a, 1, 0))
    a2 = jnp.where(row == 0, c0, jnp.where(row == 1, c1, pltpu.roll(a, 2, 0)))
    ac = bconv_ref[...] + a2 * wconv_ref[0:1, :] + a1 * wconv_ref[1:2, :] + a * wconv_ref[2:3, :]
    tail = a[tm - (CONV_W - 1):tm, :]
    carry_s[...] = tail
    tail_ref[...] = tail
    gl = (_gelu_tanh(ac) * u).astype(BF16)
    x2 = x + _rms(_dot(gl, wdown_ref[...]), gpost_ref[...])
    pe = _dot(pe_ref[...].astype(BF16), wple_ref[...])
    gate = _sigmoid(_dot(x2.astype(BF16), wgate_ref[...]))
    o_ref[...] = x2 + pe * gate


def _ffn(x, pe, prev, gpre, wup, wconv, bconv, wdown, gpost, wple, wgate, tm):
    bsz, t, _ = x.shape
    has_prev = prev is not None
    body = functools.partial(_ffn_body, tm=tm, has_prev=has_prev)
    tail_spec = pl.BlockSpec((None, CONV_W - 1, D_FF), lambda b, i: (b, 0, 0))
    in_specs = [pl.BlockSpec((None, tm, D_MODEL), lambda b, i: (b, i, 0)),
                pl.BlockSpec((None, tm, PLE_DIM), lambda b, i: (b, i, 0))]
    args = [x, pe]
    if has_prev:
        in_specs.append(tail_spec)
        args.append(prev)
    in_specs += [_const_spec((1, D_MODEL)), _const_spec((D_MODEL, 2 * D_FF)),
                 _const_spec((CONV_W, D_FF)), _const_spec((1, D_FF)),
                 _const_spec((D_FF, D_MODEL)), _const_spec((1, D_MODEL)),
                 _const_spec((PLE_DIM, D_MODEL)), _const_spec((D_MODEL, D_MODEL))]
    args += [gpre, wup, wconv, bconv, wdown, gpost, wple, wgate]
    return pl.pallas_call(
        body,
        grid=(bsz, t // tm),
        in_specs=in_specs,
        out_specs=[pl.BlockSpec((None, tm, D_MODEL), lambda b, i: (b, i, 0)), tail_spec],
        out_shape=[jax.ShapeDtypeStruct((bsz, t, D_MODEL), F32),
                   jax.ShapeDtypeStruct((bsz, CONV_W - 1, D_FF), F32)],
        scratch_shapes=[pltpu.VMEM((CONV_W - 1, D_FF), F32)],
        compiler_params=pltpu.CompilerParams(
            dimension_semantics=("parallel", "arbitrary"), vmem_limit_bytes=_VMEM_LIMIT),
        name="convffn",
    )(*args)


def _t5_bucket(rel):
    nb = NUM_BUCKETS // 2
    ret = jnp.where(rel > 0, nb, 0)
    n = jnp.abs(rel)
    max_exact = nb // 2
    large = max_exact + (jnp.log(jnp.maximum(n, max_exact).astype(jnp.float32) / max_exact)
                         / math.log(MAX_DISTANCE / max_exact) * (nb - max_exact)).astype(jnp.int32)
    large = jnp.minimum(large, nb - 1)
    return ret + jnp.where(n < max_exact, n, large)


def _bias_ext(table, sinks, lq, lk):
    q_pos = jnp.arange(lq) + WINDOW
    k_pos = jnp.arange(lk)
    buckets = _t5_bucket(k_pos[None, :] - q_pos[:, None])
    b = jnp.transpose(table[buckets], (2, 0, 1)).astype(F32)
    sink = jnp.broadcast_to(sinks.astype(F32)[:, None, None], (A_HEADS, lq, 1))
    pad = jnp.full((A_HEADS, lq, _SCORE_W - lk - 1), NEG_INF, F32)
    ext = jnp.concatenate([b, sink, pad], axis=-1)
    return ext.reshape(A_KV_HEADS, A_GROUP * lq, _SCORE_W)


def _layer(x, pe, kv_prev, s_prev, conv_prev, w, *, blk, tm_tok, tq, tc, tm_ffn):
    bsz, t, _ = x.shape
    n = bsz * t
    x2d = x.reshape(n, D_MODEL)
    qa, kv, hb, gg = _inproj(x2d, w["g_pre_mix"], w["w_in"], tm_tok)
    lq = blk
    bias_ext = _bias_ext(w["rel_table"], w["sinks"], lq, WINDOW + lq)
    ya = _attention(qa.reshape(bsz, t, A_Q_W), kv.reshape(bsz, t, 2 * A_KV_W), kv_prev, bias_ext,
                    lq=lq, tq=tq, masked=kv_prev is None)
    yb, s_fin = _hgrn(hb.reshape(bsz, t, 4 * B_KEY_W), w["lb_logits"], w["g_hgrn_out"], s_prev,
                      blk=blk, tc=tc)
    x1 = _merge(ya.reshape(n, A_Q_W), yb.reshape(n, B_VAL_W), gg, x2d,
                w["w_br_a"], w["w_br_b"], w["w_out"], w["g_post_mix"], tm_tok)
    y, conv_tail = _ffn(x1.reshape(bsz, t, D_MODEL), pe, conv_prev, w["g_pre_ffn"], w["w_up"],
                        w["w_conv"], w["b_conv"], w["w_down"], w["g_post_ffn"], w["w_ple"],
                        w["w_ple_gate"], tm_ffn)
    return y, kv.reshape(bsz, t, 2 * A_KV_W), s_fin, conv_tail


def kernel(x_prompt, x_sample, cache_win_k, cache_win_v, state_hgrn, cache_ffn_conv, p_prompt, p_sample,
           rel_bias_table, lb_logits, g_pre_mix, w_in, attn_sinks, g_hgrn_out, w_br_a, w_br_b, w_out,
           g_post_mix, g_pre_ffn, w_up, w_conv, b_conv, w_down, g_post_ffn, w_ple, w_ple_gate):
    bsz, seq, _ = x_prompt.shape
    dbsz, dseq, _ = x_sample.shape
    w = {
        "rel_table": rel_bias_table, "sinks": attn_sinks[0], "lb_logits": lb_logits.astype(F32),
        "g_pre_mix": g_pre_mix[0][None, :], "w_in": w_in[0].astype(BF16),
        "g_hgrn_out": g_hgrn_out[0][None, :],
        "w_br_a": w_br_a[0].astype(BF16), "w_br_b": w_br_b[0].astype(BF16), "w_out": w_out[0].astype(BF16),
        "g_post_mix": g_post_mix[0][None, :], "g_pre_ffn": g_pre_ffn[0][None, :],
        "w_up": w_up[0].astype(BF16), "w_conv": w_conv[0], "b_conv": b_conv[0][None, :],
        "w_down": w_down[0].astype(BF16), "g_post_ffn": g_post_ffn[0][None, :],
        "w_ple": w_ple[0].astype(BF16), "w_ple_gate": w_ple_gate[0].astype(BF16),
    }
    yp, kvp, sp, cp = _layer(x_prompt, p_prompt[0], None, None, None, w,
                             blk=CHUNK, tm_tok=256, tq=512, tc=512, tm_ffn=256)
    wc = cache_win_k.shape[2]
    kv_cache = jnp.concatenate([cache_win_k[0].reshape(dbsz, wc, A_KV_W),
                                cache_win_v[0].reshape(dbsz, wc, A_KV_W)], axis=-1)
    ys, kvs, ss, cs = _layer(x_sample, p_sample[0], kv_cache, state_hgrn[0], cache_ffn_conv[0], w,
                             blk=dseq, tm_tok=256, tq=dseq, tc=dseq, tm_ffn=dseq)
    keep = min(WINDOW, seq)

    def heads(a):
        return a.reshape(a.shape[0], a.shape[1], A_KV_HEADS, A_HEAD_DIM)[None]

    return (yp, ys,
            heads(kvp[:, seq - keep:, 0:A_KV_W]), heads(kvp[:, seq - keep:, A_KV_W:]),
            sp[None], cp[None],
            heads(kvs[:, :, 0:A_KV_W]), heads(kvs[:, :, A_KV_W:]),
            ss[None], cs[None])
```

```python
import functools
import math

import jax
import jax.numpy as jnp
from jax import lax
from jax.experimental import pallas as pl
from jax.experimental.pallas import tpu as pltpu

D_MODEL = 1024
CHUNK = 64
A_HEADS = 16
A_KV_HEADS = 2
A_HEAD_DIM = 64
A_GROUP = A_HEADS // A_KV_HEADS
WINDOW = 128
A_Q_W = A_HEADS * A_HEAD_DIM
A_KV_W = A_KV_HEADS * A_HEAD_DIM
NUM_BUCKETS = 32
MAX_DISTANCE = 128
B_HEADS = 8
B_KEY_DIM = 128
B_VAL_DIM = D_MODEL // B_HEADS
B_KEY_W = B_HEADS * B_KEY_DIM
B_VAL_W = B_HEADS * B_VAL_DIM
D_FF = 2816
CONV_W = 3
PLE_DIM = 256
EPS = 1e-6
NEG_INF = -1e30

_QA0 = 0
_KV0 = A_Q_W
_HB0 = _KV0 + 2 * A_KV_W
_GG0 = _HB0 + 2 * B_KEY_W + 2 * B_VAL_W
IN_COLS = _GG0 + 2 * D_MODEL

_SCORE_W = 256

_VMEM_LIMIT = 56 * 1024 * 1024

BF16 = jnp.bfloat16
F32 = jnp.float32


def _const_spec(shape):
    nd = len(shape)
    return pl.BlockSpec(shape, lambda *_: (0,) * nd, pipeline_mode=pl.Buffered(1))


def _rms(x, g):
    ms = jnp.mean(x * x, axis=-1, keepdims=True)
    return x * lax.rsqrt(ms + EPS) * g


def _sigmoid_of_twice(hx):
    return 0.5 * jnp.tanh(hx) + 0.5


def _dot(a, b):
    return jnp.dot(a, b, preferred_element_type=F32)


def _dot_nt(a, b):
    return lax.dot_general(a, b, (((1,), (1,)), ((), ())), preferred_element_type=F32)


def _dot_tn(a, b):
    return lax.dot_general(a, b, (((0,), (0,)), ((), ())), preferred_element_type=F32)


def _inproj_body(x_ref, g_ref, w_ref, qa_ref, kv_ref, hb_ref, gg_ref):
    h = _rms(x_ref[...], g_ref[...]).astype(BF16)
    step = 512

    def mm(lo, width):
        return _dot(h, w_ref[:, lo:lo + width])

    for j in range(0, A_Q_W, step):
        qa_ref[:, j:j + step] = mm(_QA0 + j, step).astype(BF16)
    kv_ref[...] = mm(_KV0, 2 * A_KV_W)
    for j in range(0, _GG0 - _HB0, step):
        hb_ref[:, j:j + step] = mm(_HB0 + j, step)
    for j in range(0, 2 * D_MODEL, step):
        gg_ref[:, j:j + step] = mm(_GG0 + j, step)


def _inproj(x2d, g, w_bf, tm):
    n = x2d.shape[0]
    row = lambda w: pl.BlockSpec((tm, w), lambda i: (i, 0))
    return pl.pallas_call(
        _inproj_body,
        grid=(n // tm,),
        in_specs=[row(D_MODEL), _const_spec((1, D_MODEL)), _const_spec((D_MODEL, IN_COLS))],
        out_specs=[row(A_Q_W), row(2 * A_KV_W), row(_GG0 - _HB0), row(2 * D_MODEL)],
        out_shape=[
            jax.ShapeDtypeStruct((n, A_Q_W), BF16),
            jax.ShapeDtypeStruct((n, 2 * A_KV_W), F32),
            jax.ShapeDtypeStruct((n, _GG0 - _HB0), F32),
            jax.ShapeDtypeStruct((n, 2 * D_MODEL), F32),
        ],
        compiler_params=pltpu.CompilerParams(
            dimension_semantics=("parallel",), vmem_limit_bytes=_VMEM_LIMIT),
        name="inproj",
    )(x2d, g, w_bf)


def _attn_body(q_ref, kvc_ref, kvp_ref, bias_ref, o_ref, kv_s, *, lq, lk, nchunk, tq, masked):
    i = pl.program_id(1)
    kv_s[0:WINDOW, :] = kvp_ref[...].astype(BF16)
    kv_s[WINDOW:WINDOW + tq, :] = kvc_ref[...].astype(BF16)
    zpad = jnp.zeros((_SCORE_W - lk, 2 * A_KV_W), BF16)

    def stage_scores(c):
        qc = q_ref[c * lq:(c + 1) * lq, :]
        kw = jnp.concatenate([kv_s[c * lq:c * lq + lk, :], zpad], axis=0)
        st = []
        for k in range(A_KV_HEADS):
            qs = jnp.concatenate(
                [qc[:, (k * A_GROUP + g) * A_HEAD_DIM:(k * A_GROUP + g + 1) * A_HEAD_DIM]
                 for g in range(A_GROUP)], axis=0)
            st.append(_dot_nt(kw[:, k * A_HEAD_DIM:(k + 1) * A_HEAD_DIM], qs))
        return dict(c=c, kw=kw, st=st)

    def stage_softmax(s):
        c, kw = s["c"], s["kw"]
        if masked:
            var = jnp.clip(WINDOW // CHUNK - (i * nchunk + c), 0, WINDOW // CHUNK)
        else:
            var = 0
        ot, rden = [], []
        for k in range(A_KV_HEADS):
            ps, rs = [], []
            for j in range(0, A_GROUP * lq, 128):
                t = s["st"][k][:, j:j + 128] + bias_ref[var, k, :, j:j + 128]
                m = jnp.max(t, axis=0, keepdims=True)
                p = jnp.exp(t - m)
                rs.append(1.0 / jnp.sum(p, axis=0, keepdims=True))
                ps.append(p.astype(BF16))
            rden.append(jnp.concatenate(rs, axis=1))
            vv = kw[:, A_KV_W + k * A_HEAD_DIM:A_KV_W + (k + 1) * A_HEAD_DIM]
            ot.append(_dot_tn(vv, jnp.concatenate(ps, axis=1)))
        return dict(c=c, ot=ot, rden=rden)

    def stage_out(s):
        c = s["c"]
        outs = []
        for k in range(A_KV_HEADS):
            o = (s["ot"][k] * s["rden"][k]).T
            outs.append(jnp.concatenate([o[g * lq:(g + 1) * lq, :] for g in range(A_GROUP)], axis=1))
        o_ref[c * lq:(c + 1) * lq, :] = jnp.concatenate(outs, axis=1).astype(BF16)

    ahead = 2
    scores = {c: stage_scores(c) for c in range(min(ahead, nchunk))}
    pending = None
    for c in range(nchunk):
        if c + ahead < nchunk:
            scores[c + ahead] = stage_scores(c + ahead)
        sm = stage_softmax(scores.pop(c))
        if pending is not None:
            stage_out(pending)
        pending = sm
    stage_out(pending)


def _attention(qa, kv, kv_prev, bias_ext, *, lq, tq, masked):
    bsz, t, _ = qa.shape
    lk = WINDOW + lq
    nchunk = tq // lq
    if kv_prev is None:
        per = tq // WINDOW
        prev_arr = kv
        prev_spec = pl.BlockSpec((None, WINDOW, 2 * A_KV_W),
                                 lambda b, i: (b, jnp.maximum(i * per - 1, 0), 0))
    else:
        prev_arr = kv_prev
        prev_spec = pl.BlockSpec((None, WINDOW, 2 * A_KV_W), lambda b, i: (b, 0, 0))
    body = functools.partial(_attn_body, lq=lq, lk=lk, nchunk=nchunk, tq=tq, masked=masked)
    return pl.pallas_call(
        body,
        grid=(bsz, t // tq),
        in_specs=[
            pl.BlockSpec((None, tq, A_Q_W), lambda b, i: (b, i, 0)),
            pl.BlockSpec((None, tq, 2 * A_KV_W), lambda b, i: (b, i, 0)),
            prev_spec,
            _const_spec(bias_ext.shape),
        ],
        out_specs=pl.BlockSpec((None, tq, A_Q_W), lambda b, i: (b, i, 0)),
        out_shape=jax.ShapeDtypeStruct((bsz, t, A_Q_W), BF16),
        scratch_shapes=[pltpu.VMEM((WINDOW + tq, 2 * A_KV_W), BF16)],
        compiler_params=pltpu.CompilerParams(
            dimension_semantics=("parallel", "arbitrary"), vmem_limit_bytes=_VMEM_LIMIT),
        name="attention",
    )(qa, kv, prev_arr, bias_ext)


def _cumsum_rows(x, tril3):
    hi = x.astype(BF16)
    r = x - hi.astype(F32)
    mid = r.astype(BF16)
    lo = (r - mid.astype(F32)).astype(BF16)
    return _dot(tril3, jnp.concatenate([hi, mid, lo], axis=0))


def _hgrn_body(*refs, blk, nchunk, has_state):
    if has_state:
        hb_ref, lbl_ref, g_ref, s0_ref, yb_ref, sfin_ref, st_s = refs
    else:
        hb_ref, lbl_ref, g_ref, yb_ref, sfin_ref, st_s = refs
    i = pl.program_id(1)
    nt = pl.num_programs(1)

    @pl.when(i == 0)
    def _():
        for h in range(B_HEADS):
            if has_state:
                st_s[h] = s0_ref[h].T
            else:
                st_s[h] = jnp.zeros((B_VAL_DIM, B_KEY_DIM), F32)

    lbl = lbl_ref[...]
    e = jnp.exp(lbl - jnp.max(lbl, axis=0, keepdims=True))
    lb = e[0:1, :] / jnp.sum(e, axis=0, keepdims=True)
    fa = 0.5 * (1.0 + lb)
    fb = 0.5 * (1.0 - lb)
    g = g_ref[...]
    ri = lax.broadcasted_iota(jnp.int32, (blk, blk), 0)
    ci = lax.broadcasted_iota(jnp.int32, (blk, blk), 1)
    tril = (ri >= ci).astype(BF16)
    tril3 = jnp.concatenate([tril, tril, tril], axis=1)
    ri2 = lax.broadcasted_iota(jnp.int32, (blk, 2 * blk), 0)
    ci2 = lax.broadcasted_iota(jnp.int32, (blk, 2 * blk), 1)
    causal2 = ri2 >= (ci2 & (blk - 1))
    mid = blk // 2
    w = B_KEY_W
    pw = 2 * B_KEY_DIM
    npair = B_HEADS // 2

    def blockdiag(x0, x1):
        z = jnp.zeros_like(x0)
        return jnp.concatenate([jnp.concatenate([x0, z], axis=1), jnp.concatenate([z, x1], axis=1)], axis=0)

    hs = [slice(h * B_KEY_DIM, (h + 1) * B_KEY_DIM) for h in range(B_HEADS)]
    ps = [slice(j * pw, (j + 1) * pw) for j in range(npair)]

    def stage_decay(c):
        rows = pl.ds(c * blk, blk)
        bt = fb * jnp.tanh(hb_ref[rows, w:2 * w])
        f = fa + bt
        cum = _cumsum_rows(jnp.log2(f), tril3)
        return dict(rows=rows, kk=fb - bt, cum=cum)

    def stage_state(s):
        rows, cum = s["rows"], s["cum"]
        hq = hb_ref[rows, 0:w]
        qs = hq + hq * jnp.tanh(hq)
        b_last = cum[blk - 1:blk, :]
        b_mid = cum[mid:mid + 1, :]
        q2f = qs * jnp.exp2(cum - b_mid)
        k2f = s["kk"] * jnp.exp2(b_mid - cum)
        q1 = (q2f * jnp.exp2(b_mid)).astype(BF16)
        k3 = (k2f * jnp.exp2(b_last - b_mid)).astype(BF16)
        q2 = q2f.astype(BF16)
        k2 = k2f.astype(BF16)
        vb = hb_ref[rows, 2 * w:3 * w].astype(BF16)
        dec = jnp.exp2(b_last)
        a = [_dot_nt(q2[:, ps[j]], blockdiag(k2[:, hs[2 * j]], k2[:, hs[2 * j + 1]])) for j in range(npair)]
        st = [st_s[h] for h in range(B_HEADS)]
        o1 = [_dot_nt(q1[:, ps[j]], blockdiag(st[2 * j].astype(BF16), st[2 * j + 1].astype(BF16)))
              for j in range(npair)]
        upd = [_dot_tn(vb[:, sl], k3[:, sl]) for sl in hs]
        for h, sl in enumerate(hs):
            st_s[h] = dec[:, sl] * st[h] + upd[h]
        return dict(rows=rows, a=a, o1=o1, vb=vb)

    def stage_out(s):
        rows, vb = s["rows"], s["vb"]
        am = [jnp.where(causal2, s["a"][j], 0.0).astype(BF16) for j in range(npair)]
        o2 = [_dot(am[j], blockdiag(vb[:, hs[2 * j]], vb[:, hs[2 * j + 1]])) for j in range(npair)]
        o = [s["o1"][j] + o2[j] for j in range(npair)]
        ys = [_rms(o[h // 2][:, (h % 2) * B_VAL_DIM:(h % 2 + 1) * B_VAL_DIM], g) for h in range(B_HEADS)]
        hog = hb_ref[rows, 3 * w:4 * w]
        y = jnp.concatenate(ys, axis=1) * (hog + hog * jnp.tanh(hog))
        yb_ref[rows, :] = y.astype(BF16)

    decay = {0: stage_decay(0)}
    pending = None
    for c in range(nchunk):
        if c + 1 < nchunk:
            decay[c + 1] = stage_decay(c + 1)
        state_c = stage_state(decay.pop(c))
        if pending is not None:
            stage_out(pending)
        pending = state_c
    stage_out(pending)

    @pl.when(i == nt - 1)
    def _():
        for h in range(B_HEADS):
            sfin_ref[h] = st_s[h].T


def _hgrn(hb, lb_logits, g_out, s0, *, blk, tc):
    bsz, t, _ = hb.shape
    has_state = s0 is not None
    body = functools.partial(_hgrn_body, blk=blk, nchunk=tc // blk, has_state=has_state)
    st_spec = pl.BlockSpec((None, B_HEADS, B_KEY_DIM, B_VAL_DIM), lambda b, i: (b, 0, 0, 0))
    in_specs = [
        pl.BlockSpec((None, tc, 4 * B_KEY_W), lambda b, i: (b, i, 0)),
        _const_spec(lb_logits.shape),
        _const_spec((1, B_VAL_DIM)),
    ]
    args = [hb, lb_logits, g_out]
    if has_state:
        in_specs.append(st_spec)
        args.append(s0)
    return pl.pallas_call(
        body,
        grid=(bsz, t // tc),
        in_specs=in_specs,
        out_specs=[pl.BlockSpec((None, tc, B_VAL_W), lambda b, i: (b, i, 0)), st_spec],
        out_shape=[jax.ShapeDtypeStruct((bsz, t, B_VAL_W), BF16),
                   jax.ShapeDtypeStruct((bsz, B_HEADS, B_KEY_DIM, B_VAL_DIM), F32)],
        scratch_shapes=[pltpu.VMEM((B_HEADS, B_VAL_DIM, B_KEY_DIM), F32)],
        compiler_params=pltpu.CompilerParams(
            dimension_semantics=("parallel", "arbitrary"), vmem_limit_bytes=_VMEM_LIMIT),
        name="hgrn2",
    )(*args)


def _merge_body(ya_ref, yb_ref, gg_ref, x_ref, wa_ref, wb_ref, wo_ref, g_ref, o_ref):
    ga = gg_ref[:, 0:D_MODEL]
    gb = gg_ref[:, D_MODEL:2 * D_MODEL]
    mix = (_sigmoid_of_twice(ga) * _dot(ya_ref[...], wa_ref[...])
           + _sigmoid_of_twice(gb) * _dot(yb_ref[...], wb_ref[...]))
    r = _dot(mix.astype(BF16), wo_ref[...])
    o_ref[...] = x_ref[...] + _rms(r, g_ref[...])


def _merge(ya, yb, gg, x2d, wa, wb, wo, g, tm):
    n = x2d.shape[0]
    row = lambda w: pl.BlockSpec((tm, w), lambda i: (i, 0))
    wspec = _const_spec((D_MODEL, D_MODEL))
    return pl.pallas_call(
        _merge_body,
        grid=(n // tm,),
        in_specs=[row(A_Q_W), row(B_VAL_W), row(2 * D_MODEL), row(D_MODEL),
                  wspec, wspec, wspec, _const_spec((1, D_MODEL))],
        out_specs=row(D_MODEL),
        out_shape=jax.ShapeDtypeStruct((n, D_MODEL), F32),
        compiler_params=pltpu.CompilerParams(
            dimension_semantics=("parallel",), vmem_limit_bytes=_VMEM_LIMIT),
        name="merge",
    )(ya, yb, gg, x2d, wa, wb, wo, g)


def _gelu_tanh(x):
    c = math.sqrt(2.0 / math.pi)
    return 0.5 * x * (1.0 + jnp.tanh(c * (x + 0.044715 * (x * x * x))))


def _ffn_body(*refs, tm, has_prev):
    if has_prev:
        (x_ref, pe_ref, prev_ref, gpre_ref, wup_ref, wconv_ref, bconv_ref, wdown_ref, gpost_ref,
         wple_ref, wgate_ref, o_ref, tail_ref, carry_s) = refs
    else:
        (x_ref, pe_ref, gpre_ref, wup_ref, wconv_ref, bconv_ref, wdown_ref, gpost_ref,
         wple_ref, wgate_ref, o_ref, tail_ref, carry_s) = refs
    i = pl.program_id(1)

    @pl.when(i == 0)
    def _():
        if has_prev:
            carry_s[...] = prev_ref[...]
        else:
            carry_s[...] = jnp.zeros((CONV_W - 1, D_FF), F32)

    x = x_ref[...]
    hf = _rms(x, gpre_ref[...]).astype(BF16)
    a = _dot(hf, wup_ref[:, 0:D_FF])
    u = _dot(hf, wup_ref[:, D_FF:2 * D_FF])
    row = lax.broadcasted_iota(jnp.int32, (tm, D_FF), 0)
    c0 = carry_s[0:1, :]
    c1 = carry_s[1:2, :]
    a1 = jnp.where(row == 0, c1, pltpu.roll(a, 1, 0))
    a2 = jnp.where(row == 0, c0, jnp.where(row == 1, c1, pltpu.roll(a, 2, 0)))
    ac = bconv_ref[...] + a2 * wconv_ref[0:1, :] + a1 * wconv_ref[1:2, :] + a * wconv_ref[2:3, :]
    tail = a[tm - (CONV_W - 1):tm, :]
    carry_s[...] = tail
    tail_ref[...] = tail
    gl = (_gelu_tanh(ac) * u).astype(BF16)
    x2 = x + _rms(_dot(gl, wdown_ref[...]), gpost_ref[...])
    pe = _dot(pe_ref[...].astype(BF16), wple_ref[...])
    gate = _sigmoid_of_twice(_dot(x2.astype(BF16), wgate_ref[...]))
    o_ref[...] = x2 + pe * gate


def _ffn(x, pe, prev, gpre, wup, wconv, bconv, wdown, gpost, wple, wgate, tm):
    bsz, t, _ = x.shape
    has_prev = prev is not None
    body = functools.partial(_ffn_body, tm=tm, has_prev=has_prev)
    tail_spec = pl.BlockSpec((None, CONV_W - 1, D_FF), lambda b, i: (b, 0, 0))
    in_specs = [pl.BlockSpec((None, tm, D_MODEL), lambda b, i: (b, i, 0)),
                pl.BlockSpec((None, tm, PLE_DIM), lambda b, i: (b, i, 0))]
    args = [x, pe]
    if has_prev:
        in_specs.append(tail_spec)
        args.append(prev)
    in_specs += [_const_spec((1, D_MODEL)), _const_spec((D_MODEL, 2 * D_FF)),
                 _const_spec((CONV_W, D_FF)), _const_spec((1, D_FF)),
                 _const_spec((D_FF, D_MODEL)), _const_spec((1, D_MODEL)),
                 _const_spec((PLE_DIM, D_MODEL)), _const_spec((D_MODEL, D_MODEL))]
    args += [gpre, wup, wconv, bconv, wdown, gpost, wple, wgate]
    return pl.pallas_call(
        body,
        grid=(bsz, t // tm),
        in_specs=in_specs,
        out_specs=[pl.BlockSpec((None, tm, D_MODEL), lambda b, i: (b, i, 0)), tail_spec],
        out_shape=[jax.ShapeDtypeStruct((bsz, t, D_MODEL), F32),
                   jax.ShapeDtypeStruct((bsz, CONV_W - 1, D_FF), F32)],
        scratch_shapes=[pltpu.VMEM((CONV_W - 1, D_FF), F32)],
        compiler_params=pltpu.CompilerParams(
            dimension_semantics=("parallel", "arbitrary"), vmem_limit_bytes=_VMEM_LIMIT),
        name="convffn",
    )(*args)


def _t5_bucket(rel):
    nb = NUM_BUCKETS // 2
    ret = jnp.where(rel > 0, nb, 0)
    n = jnp.abs(rel)
    max_exact = nb // 2
    large = max_exact + (jnp.log(jnp.maximum(n, max_exact).astype(jnp.float32) / max_exact)
                         / math.log(MAX_DISTANCE / max_exact) * (nb - max_exact)).astype(jnp.int32)
    large = jnp.minimum(large, nb - 1)
    return ret + jnp.where(n < max_exact, n, large)


def _bias_body(table_ref, sinks_ref, bk_ref, o_ref, *, nvar):
    bk = bk_ref[...]
    row = lax.broadcasted_iota(jnp.int32, bk.shape, 0)
    for k in range(A_KV_HEADS):
        acc = jnp.where(bk == -1, sinks_ref[k], NEG_INF)
        for b in range(NUM_BUCKETS):
            acc = jnp.where(bk == b, table_ref[k, b:b + 1, :], acc)
        for v in range(nvar):
            o_ref[v, k] = jnp.where(row < v * CHUNK, NEG_INF, acc)


def _bias_ext(table, sinks, lq, lk, nvar):
    q_pos = jnp.arange(lq) + WINDOW
    k_pos = jnp.arange(lk)
    buckets = _t5_bucket(k_pos[:, None] - q_pos[None, :]).astype(jnp.int32)
    bk = jnp.concatenate([buckets, jnp.full((1, lq), -1, jnp.int32),
                          jnp.full((_SCORE_W - lk - 1, lq), -2, jnp.int32)], axis=0)
    bk = jnp.tile(bk, (1, A_GROUP))
    tab = jnp.repeat(table.astype(F32).reshape(NUM_BUCKETS, A_KV_HEADS, A_GROUP), lq, axis=2)
    tab = jnp.transpose(tab, (1, 0, 2))
    snk = jnp.repeat(sinks.astype(F32).reshape(A_KV_HEADS, 1, A_GROUP), lq, axis=2)
    vmem = pl.BlockSpec(memory_space=pltpu.VMEM)
    return pl.pallas_call(
        functools.partial(_bias_body, nvar=nvar),
        in_specs=[vmem, vmem, vmem],
        out_specs=vmem,
        out_shape=jax.ShapeDtypeStruct((nvar, A_KV_HEADS, _SCORE_W, A_GROUP * lq), F32),
        name="relbias",
    )(tab, snk, bk)


def _layer(x, pe, kv_prev, s_prev, conv_prev, w, *, blk, tm_tok, tq, tc, tm_ffn):
    bsz, t, _ = x.shape
    n = bsz * t
    x2d = x.reshape(n, D_MODEL)
    qa, kv, hb, gg = _inproj(x2d, w["g_pre_mix"], w["w_in"], tm_tok)
    lq = blk
    masked = kv_prev is None
    bias_ext = _bias_ext(w["rel_table"], w["sinks"], lq, WINDOW + lq, WINDOW // CHUNK + 1 if masked else 1)
    ya = _attention(qa.reshape(bsz, t, A_Q_W), kv.reshape(bsz, t, 2 * A_KV_W), kv_prev, bias_ext,
                    lq=lq, tq=tq, masked=masked)
    yb, s_fin = _hgrn(hb.reshape(bsz, t, 4 * B_KEY_W), w["lb_logits"], w["g_hgrn_out"], s_prev,
                      blk=blk, tc=tc)
    x1 = _merge(ya.reshape(n, A_Q_W), yb.reshape(n, B_VAL_W), gg, x2d,
                w["w_br_a"], w["w_br_b"], w["w_out"], w["g_post_mix"], tm_tok)
    y, conv_tail = _ffn(x1.reshape(bsz, t, D_MODEL), pe, conv_prev, w["g_pre_ffn"], w["w_up"],
                        w["w_conv"], w["b_conv"], w["w_down"], w["g_post_ffn"], w["w_ple"],
                        w["w_ple_gate"], tm_ffn)
    return y, kv.reshape(bsz, t, 2 * A_KV_W), s_fin, conv_tail


def _scale_in_cols(w_in):
    h = B_KEY_W
    scale = jnp.concatenate([
        jnp.full((A_Q_W,), A_HEAD_DIM ** -0.5, F32), jnp.ones((2 * A_KV_W,), F32),
        jnp.full((2 * h,), 0.5, F32), jnp.ones((B_VAL_W,), F32), jnp.full((B_VAL_W,), 0.5, F32),
        jnp.full((2 * D_MODEL,), 0.5, F32)])
    return w_in * scale[None, :]


def kernel(x_prompt, x_sample, cache_win_k, cache_win_v, state_hgrn, cache_ffn_conv, p_prompt, p_sample,
           rel_bias_table, lb_logits, g_pre_mix, w_in, attn_sinks, g_hgrn_out, w_br_a, w_br_b, w_out,
           g_post_mix, g_pre_ffn, w_up, w_conv, b_conv, w_down, g_post_ffn, w_ple, w_ple_gate):
    bsz, seq, _ = x_prompt.shape
    dbsz, dseq, _ = x_sample.shape
    w = {
        "rel_table": rel_bias_table, "sinks": attn_sinks[0], "lb_logits": lb_logits.astype(F32),
        "g_pre_mix": g_pre_mix[0][None, :], "w_in": _scale_in_cols(w_in[0]).astype(BF16),
        "g_hgrn_out": g_hgrn_out[0][None, :],
        "w_br_a": w_br_a[0].astype(BF16), "w_br_b": w_br_b[0].astype(BF16), "w_out": w_out[0].astype(BF16),
        "g_post_mix": g_post_mix[0][None, :], "g_pre_ffn": g_pre_ffn[0][None, :],
        "w_up": w_up[0].astype(BF16), "w_conv": w_conv[0], "b_conv": b_conv[0][None, :],
        "w_down": w_down[0].astype(BF16), "g_post_ffn": g_post_ffn[0][None, :],
        "w_ple": w_ple[0].astype(BF16), "w_ple_gate": (0.5 * w_ple_gate[0]).astype(BF16),
    }
    yp, kvp, sp, cp = _layer(x_prompt, p_prompt[0], None, None, None, w,
                             blk=CHUNK, tm_tok=256, tq=512, tc=512, tm_ffn=256)
    wc = cache_win_k.shape[2]
    kv_cache = jnp.concatenate([cache_win_k[0].reshape(dbsz, wc, A_KV_W),
                                cache_win_v[0].reshape(dbsz, wc, A_KV_W)], axis=-1)
    ys, kvs, ss, cs = _layer(x_sample, p_sample[0], kv_cache, state_hgrn[0], cache_ffn_conv[0], w,
                             blk=dseq, tm_tok=256, tq=dseq, tc=dseq, tm_ffn=dseq)
    keep = min(WINDOW, seq)

    def heads(a):
        return a.reshape(a.shape[0], a.shape[1], A_KV_HEADS, A_HEAD_DIM)[None]

    return (yp, ys,
            heads(kvp[:, seq - keep:, 0:A_KV_W]), heads(kvp[:, seq - keep:, A_KV_W:]),
            sp[None], cp[None],
            heads(kvs[:, :, 0:A_KV_W]), heads(kvs[:, :, A_KV_W:]),
            ss[None], cs[None])
```

```python
import functools
import math

import jax
import jax.numpy as jnp
from jax import lax
from jax.experimental import pallas as pl
from jax.experimental.pallas import tpu as pltpu

D_MODEL = 1024
CHUNK = 64
A_HEADS = 16
A_KV_HEADS = 2
A_HEAD_DIM = 64
A_GROUP = A_HEADS // A_KV_HEADS
WINDOW = 128
A_Q_W = A_HEADS * A_HEAD_DIM
A_KV_W = A_KV_HEADS * A_HEAD_DIM
NUM_BUCKETS = 32
MAX_DISTANCE = 128
B_HEADS = 8
B_KEY_DIM = 128
B_VAL_DIM = D_MODEL // B_HEADS
B_KEY_W = B_HEADS * B_KEY_DIM
B_VAL_W = B_HEADS * B_VAL_DIM
D_FF = 2816
CONV_W = 3
PLE_DIM = 256
EPS = 1e-6
NEG_INF = -1e30

_QA0 = 0
_KV0 = A_Q_W
_HB0 = _KV0 + 2 * A_KV_W
_GG0 = _HB0 + 2 * B_KEY_W + 2 * B_VAL_W
IN_COLS = _GG0 + 2 * D_MODEL
_HB_W = _GG0 - _HB0

_BF16_ROWS = 16
_VMEM_LIMIT = 56 * 1024 * 1024

BF16 = jnp.bfloat16
F32 = jnp.float32


def _score_rows(lk):
    return lk + _BF16_ROWS


def _const_spec(shape):
    nd = len(shape)
    return pl.BlockSpec(shape, lambda *_: (0,) * nd, pipeline_mode=pl.Buffered(1))


def _rms(x, g):
    ms = jnp.mean(x * x, axis=-1, keepdims=True)
    return x * lax.rsqrt(ms + EPS) * g


def _sigmoid_of_twice(hx):
    return 0.5 * jnp.tanh(hx) + 0.5


def _dot(a, b):
    return jnp.dot(a, b, preferred_element_type=F32)


def _dot_nt(a, b):
    return lax.dot_general(a, b, (((1,), (1,)), ((), ())), preferred_element_type=F32)


def _dot_tn(a, b):
    return lax.dot_general(a, b, (((0,), (0,)), ((), ())), preferred_element_type=F32)


def _interleave(a, b):
    out, nb = [], 0
    for i, t in enumerate(a):
        out.append(t)
        want = ((i + 1) * len(b)) // len(a)
        out.extend(b[nb:want])
        nb = want
    return out + b[nb:]


def _attn_thunks(nchunk, lq, lk, load_q, load_kw, bias_strip, store_o):
    def scores(c):
        qc = load_q(c)
        kw = load_kw(c)
        st = []
        for k in range(A_KV_HEADS):
            qs = jnp.concatenate(
                [qc[:, (k * A_GROUP + g) * A_HEAD_DIM:(k * A_GROUP + g + 1) * A_HEAD_DIM]
                 for g in range(A_GROUP)], axis=0)
            st.append(_dot_nt(kw[:, k * A_HEAD_DIM:(k + 1) * A_HEAD_DIM], qs))
        return dict(kw=kw, st=st)

    def softmax(c, s):
        ot, rden = [], []
        for k in range(A_KV_HEADS):
            ps, rs = [], []
            for j in range(0, A_GROUP * lq, 128):
                t = s["st"][k][:, j:j + 128] + bias_strip(c, k, j)
                m = jnp.max(t, axis=0, keepdims=True)
                p = jnp.exp(t - m)
                rs.append(1.0 / jnp.sum(p, axis=0, keepdims=True))
                ps.append(p.astype(BF16))
            rden.append(jnp.concatenate(rs, axis=1))
            vv = s["kw"][:, A_KV_W + k * A_HEAD_DIM:A_KV_W + (k + 1) * A_HEAD_DIM]
            ot.append(_dot_tn(vv, jnp.concatenate(ps, axis=1)))
        return dict(ot=ot, rden=rden)

    def out(c, s):
        outs = []
        for k in range(A_KV_HEADS):
            o = (s["ot"][k] * s["rden"][k]).T
            outs.append(jnp.concatenate([o[g * lq:(g + 1) * lq, :] for g in range(A_GROUP)], axis=1))
        store_o(c, jnp.concatenate(outs, axis=1).astype(BF16))

    ahead = 2
    sc, sm, th = {}, {}, []

    def do_scores(c):
        sc[c] = scores(c)

    def do_softmax(c):
        sm[c] = softmax(c, sc.pop(c))

    def do_out(c):
        out(c, sm.pop(c))

    for c in range(min(ahead, nchunk)):
        th.append(functools.partial(do_scores, c))
    for c in range(nchunk):
        if c + ahead < nchunk:
            th.append(functools.partial(do_scores, c + ahead))
        th.append(functools.partial(do_softmax, c))
        if c >= 1:
            th.append(functools.partial(do_out, c - 1))
    th.append(functools.partial(do_out, nchunk - 1))
    return th


def _attn_body(q_ref, kvc_ref, kvp_ref, bias_ref, o_ref, kv_s, *, lq, lk, nchunk, tq):
    kv_s[0:WINDOW, :] = kvp_ref[...].astype(BF16)
    kv_s[WINDOW:WINDOW + tq, :] = kvc_ref[...].astype(BF16)
    zpad = jnp.zeros((_score_rows(lk) - lk, 2 * A_KV_W), BF16)

    def store_o(c, o):
        o_ref[c * lq:(c + 1) * lq, :] = o

    for t in _attn_thunks(
            nchunk, lq, lk,
            load_q=lambda c: q_ref[c * lq:(c + 1) * lq, :],
            load_kw=lambda c: jnp.concatenate([kv_s[c * lq:c * lq + lk, :], zpad], axis=0),
            bias_strip=lambda c, k, j: bias_ref[0, k, :, j:j + 128],
            store_o=store_o):
        t()


def _attention_cached(qa, kv, kv_prev, bias_ext, *, lq):
    bsz, t, _ = qa.shape
    lk = WINDOW + lq
    body = functools.partial(_attn_body, lq=lq, lk=lk, nchunk=t // lq, tq=t)
    return pl.pallas_call(
        body,
        grid=(bsz,),
        in_specs=[
            pl.BlockSpec((None, t, A_Q_W), lambda b: (b, 0, 0)),
            pl.BlockSpec((None, t, 2 * A_KV_W), lambda b: (b, 0, 0)),
            pl.BlockSpec((None, WINDOW, 2 * A_KV_W), lambda b: (b, 0, 0)),
            _const_spec(bias_ext.shape),
        ],
        out_specs=pl.BlockSpec((None, t, A_Q_W), lambda b: (b, 0, 0)),
        out_shape=jax.ShapeDtypeStruct((bsz, t, A_Q_W), BF16),
        scratch_shapes=[pltpu.VMEM((WINDOW + t, 2 * A_KV_W), BF16)],
        compiler_params=pltpu.CompilerParams(
            dimension_semantics=("parallel",), vmem_limit_bytes=_VMEM_LIMIT),
        name="attention",
    )(qa, kv, kv_prev, bias_ext)


def _cumsum_rows(x, tril3):
    hi = x.astype(BF16)
    r = x - hi.astype(F32)
    mid = r.astype(BF16)
    lo = (r - mid.astype(F32)).astype(BF16)
    return _dot(tril3, jnp.concatenate([hi, mid, lo], axis=0))


def _hgrn_thunks(nchunk, blk, load, store_y, st_s, lbl, g):
    e = jnp.exp(lbl - jnp.max(lbl, axis=0, keepdims=True))
    lb = e[0:1, :] / jnp.sum(e, axis=0, keepdims=True)
    fa = 0.5 * (1.0 + lb)
    fb = 0.5 * (1.0 - lb)
    ri = lax.broadcasted_iota(jnp.int32, (blk, blk), 0)
    ci = lax.broadcasted_iota(jnp.int32, (blk, blk), 1)
    tril = (ri >= ci).astype(BF16)
    tril3 = jnp.concatenate([tril, tril, tril], axis=1)
    ri2 = lax.broadcasted_iota(jnp.int32, (blk, 2 * blk), 0)
    ci2 = lax.broadcasted_iota(jnp.int32, (blk, 2 * blk), 1)
    causal2 = ri2 >= (ci2 & (blk - 1))
    mid = blk // 2
    w = B_KEY_W
    pw = 2 * B_KEY_DIM
    npair = B_HEADS // 2
    hs = [slice(h * B_KEY_DIM, (h + 1) * B_KEY_DIM) for h in range(B_HEADS)]
    ps = [slice(j * pw, (j + 1) * pw) for j in range(npair)]

    def blockdiag(x0, x1):
        z = jnp.zeros_like(x0)
        return jnp.concatenate([jnp.concatenate([x0, z], axis=1), jnp.concatenate([z, x1], axis=1)], axis=0)

    def stage_decay(c):
        bt = fb * jnp.tanh(load(c, w, 2 * w))
        f = fa + bt
        cum = _cumsum_rows(jnp.log2(f), tril3)
        return dict(kk=fb - bt, cum=cum)

    def stage_state(c, s):
        cum = s["cum"]
        hq = load(c, 0, w)
        qs = hq + hq * jnp.tanh(hq)
        b_last = cum[blk - 1:blk, :]
        b_mid = cum[mid:mid + 1, :]
        q2f = qs * jnp.exp2(cum - b_mid)
        k2f = s["kk"] * jnp.exp2(b_mid - cum)
        q1 = (q2f * jnp.exp2(b_mid)).astype(BF16)
        k3 = (k2f * jnp.exp2(b_last - b_mid)).astype(BF16)
        q2 = q2f.astype(BF16)
        k2 = k2f.astype(BF16)
        vb = load(c, 2 * w, 3 * w).astype(BF16)
        dec = jnp.exp2(b_last)
        a = [_dot_nt(q2[:, ps[j]], blockdiag(k2[:, hs[2 * j]], k2[:, hs[2 * j + 1]])) for j in range(npair)]
        st = [st_s[h] for h in range(B_HEADS)]
        o1 = [_dot_nt(q1[:, ps[j]], blockdiag(st[2 * j].astype(BF16), st[2 * j + 1].astype(BF16)))
              for j in range(npair)]
        upd = [_dot_tn(vb[:, sl], k3[:, sl]) for sl in hs]
        for h, sl in enumerate(hs):
            st_s[h] = dec[:, sl] * st[h] + upd[h]
        return dict(a=a, o1=o1, vb=vb)

    def stage_out(c, s):
        vb = s["vb"]
        am = [jnp.where(causal2, s["a"][j], 0.0).astype(BF16) for j in range(npair)]
        o2 = [_dot(am[j], blockdiag(vb[:, hs[2 * j]], vb[:, hs[2 * j + 1]])) for j in range(npair)]
        o = [s["o1"][j] + o2[j] for j in range(npair)]
        ys = [_rms(o[h // 2][:, (h % 2) * B_VAL_DIM:(h % 2 + 1) * B_VAL_DIM], g) for h in range(B_HEADS)]
        hog = load(c, 3 * w, 4 * w)
        y = jnp.concatenate(ys, axis=1) * (hog + hog * jnp.tanh(hog))
        store_y(c, y.astype(BF16))

    dec, sta, th = {}, {}, []

    def do_decay(c):
        dec[c] = stage_decay(c)

    def do_state(c):
        sta[c] = stage_state(c, dec.pop(c))

    def do_out(c):
        stage_out(c, sta.pop(c))

    th.append(functools.partial(do_decay, 0))
    for c in range(nchunk):
        if c + 1 < nchunk:
            th.append(functools.partial(do_decay, c + 1))
        th.append(functools.partial(do_state, c))
        if c >= 1:
            th.append(functools.partial(do_out, c - 1))
    th.append(functools.partial(do_out, nchunk - 1))
    return th


def _hgrn_body(hb_ref, lbl_ref, g_ref, s0_ref, yb_ref, sfin_ref, st_s, *, blk, nchunk):
    for h in range(B_HEADS):
        st_s[h] = s0_ref[h].T

    def store_y(c, y):
        yb_ref[c * blk:(c + 1) * blk, :] = y

    for t in _hgrn_thunks(nchunk, blk, lambda c, lo, hi: hb_ref[c * blk:(c + 1) * blk, lo:hi],
                          store_y, st_s, lbl_ref[...], g_ref[...]):
        t()
    for h in range(B_HEADS):
        sfin_ref[h] = st_s[h].T


def _hgrn_cached(hb, lb_logits, g_out, s0, *, blk):
    bsz, t, _ = hb.shape
    body = functools.partial(_hgrn_body, blk=blk, nchunk=t // blk)
    st_spec = pl.BlockSpec((None, B_HEADS, B_KEY_DIM, B_VAL_DIM), lambda b: (b, 0, 0, 0))
    return pl.pallas_call(
        body,
        grid=(bsz,),
        in_specs=[pl.BlockSpec((None, t, _HB_W), lambda b: (b, 0, 0)),
                  _const_spec(lb_logits.shape), _const_spec((1, B_VAL_DIM)), st_spec],
        out_specs=[pl.BlockSpec((None, t, B_VAL_W), lambda b: (b, 0, 0)), st_spec],
        out_shape=[jax.ShapeDtypeStruct((bsz, t, B_VAL_W), BF16),
                   jax.ShapeDtypeStruct((bsz, B_HEADS, B_KEY_DIM, B_VAL_DIM), F32)],
        scratch_shapes=[pltpu.VMEM((B_HEADS, B_VAL_DIM, B_KEY_DIM), F32)],
        compiler_params=pltpu.CompilerParams(
            dimension_semantics=("parallel",), vmem_limit_bytes=_VMEM_LIMIT),
        name="hgrn2",
    )(hb, lb_logits, g_out, s0)


def _inproj_body(x_ref, g_ref, w_ref, qa_ref, kv_ref, hb_ref, gg_ref):
    h = _rms(x_ref[...], g_ref[...]).astype(BF16)
    step = 512

    def mm(lo, width):
        return _dot(h, w_ref[:, lo:lo + width])

    for j in range(0, A_Q_W, step):
        qa_ref[:, j:j + step] = mm(_QA0 + j, step).astype(BF16)
    kv_ref[...] = mm(_KV0, 2 * A_KV_W)
    for j in range(0, _HB_W, step):
        hb_ref[:, j:j + step] = mm(_HB0 + j, step)
    for j in range(0, 2 * D_MODEL, step):
        gg_ref[:, j:j + step] = mm(_GG0 + j, step)


def _inproj(x2d, g, w_bf, tm):
    n = x2d.shape[0]
    row = lambda w: pl.BlockSpec((tm, w), lambda i: (i, 0))
    return pl.pallas_call(
        _inproj_body,
        grid=(n // tm,),
        in_specs=[row(D_MODEL), _const_spec((1, D_MODEL)), _const_spec((D_MODEL, IN_COLS))],
        out_specs=[row(A_Q_W), row(2 * A_KV_W), row(_HB_W), row(2 * D_MODEL)],
        out_shape=[
            jax.ShapeDtypeStruct((n, A_Q_W), BF16),
            jax.ShapeDtypeStruct((n, 2 * A_KV_W), F32),
            jax.ShapeDtypeStruct((n, _HB_W), F32),
            jax.ShapeDtypeStruct((n, 2 * D_MODEL), F32),
        ],
        compiler_params=pltpu.CompilerParams(
            dimension_semantics=("parallel",), vmem_limit_bytes=_VMEM_LIMIT),
        name="inproj",
    )(x2d, g, w_bf)


def _mixer_body(x_ref, gpre_ref, w_ref, lbl_ref, ghg_ref, bias_ref,
                kv_ref, gg_ref, ya_ref, yb_ref, sfin_ref,
                h_s, qa_s, kvb_s, kvw_s, hb_s, st_s, *, tm, tiles_per_seq):
    s = pl.program_id(0)
    cur = s % 2
    prv = 1 - cur
    tib = (s + tiles_per_seq - 1) % tiles_per_seq
    nchunk = tm // CHUNK
    lk = WINDOW + CHUNK
    piece = 256

    @pl.when(s == 0)
    def _():
        qa_s[...] = jnp.zeros(qa_s.shape, BF16)
        kvb_s[...] = jnp.zeros(kvb_s.shape, BF16)
        kvw_s[...] = jnp.zeros(kvw_s.shape, BF16)
        hb_s[...] = jnp.zeros(hb_s.shape, F32)

    @pl.when((s == 0) | (tib == 0))
    def _():
        st_s[...] = jnp.zeros(st_s.shape, F32)

    kvw_s[0:WINDOW, :] = kvw_s[tm:tm + WINDOW, :]
    kvw_s[WINDOW:WINDOW + tm, :] = kvb_s[prv]
    h_s[...] = _rms(x_ref[...], gpre_ref[...]).astype(BF16)

    def dense_piece(lo):
        z = _dot(h_s[...], w_ref[:, lo:lo + piece])
        if lo < _KV0:
            qa_s[cur, :, lo:lo + piece] = z.astype(BF16)
        elif lo < _HB0:
            kv_ref[...] = z
            kvb_s[cur] = z.astype(BF16)
        elif lo < _GG0:
            hb_s[cur, :, lo - _HB0:lo - _HB0 + piece] = z
        else:
            gg_ref[:, lo - _GG0:lo - _GG0 + piece] = z

    dense = {lo: functools.partial(dense_piece, lo) for lo in range(0, IN_COLS, piece)}
    dense_hb = [t for lo, t in dense.items() if _HB0 <= lo < _GG0]
    dense_rest = [t for lo, t in dense.items() if not _HB0 <= lo < _GG0]

    def store_ya(c, o):
        ya_ref[c * CHUNK:(c + 1) * CHUNK, :] = o

    def store_yb(c, y):
        yb_ref[c * CHUNK:(c + 1) * CHUNK, :] = y

    zpad = jnp.zeros((_score_rows(lk) - lk, 2 * A_KV_W), BF16)

    def bias_strip(c, k, j):
        var = jnp.clip(WINDOW // CHUNK - (tib * nchunk + c), 0, WINDOW // CHUNK)
        return bias_ref[var, k, :, j:j + 128]

    attn = _attn_thunks(
        nchunk, CHUNK, lk,
        load_q=lambda c: qa_s[prv, c * CHUNK:(c + 1) * CHUNK, :],
        load_kw=lambda c: jnp.concatenate([kvw_s[c * CHUNK:c * CHUNK + lk, :], zpad], axis=0),
        bias_strip=bias_strip, store_o=store_ya)
    hgrn = _hgrn_thunks(
        nchunk, CHUNK, lambda c, lo, hi: hb_s[prv, c * CHUNK:(c + 1) * CHUNK, lo:hi],
        store_yb, st_s, lbl_ref[...], ghg_ref[...])
    for t in _interleave(dense_hb, attn) + _interleave(dense_rest, hgrn):
        t()

    @pl.when((tib == tiles_per_seq - 1) & (s > 0))
    def _():
        for h in range(B_HEADS):
            sfin_ref[h] = st_s[h].T


def _mixer(x2d, g_pre, w_bf, lb_logits, g_hgrn, bias_ext, *, tm, seq):
    n = x2d.shape[0]
    nt = n // tm
    tiles_per_seq = seq // tm
    dense_row = lambda w: pl.BlockSpec((tm, w), lambda s: (jnp.minimum(s, nt - 1), 0))
    lag_row = lambda w: pl.BlockSpec((tm, w), lambda s: (jnp.maximum(s - 1, 0), 0))
    body = functools.partial(_mixer_body, tm=tm, tiles_per_seq=tiles_per_seq)
    return pl.pallas_call(
        body,
        grid=(nt + 1,),
        in_specs=[dense_row(D_MODEL), _const_spec((1, D_MODEL)), _const_spec((D_MODEL, IN_COLS)),
                  _const_spec(lb_logits.shape), _const_spec((1, B_VAL_DIM)), _const_spec(bias_ext.shape)],
        out_specs=[dense_row(2 * A_KV_W), dense_row(2 * D_MODEL), lag_row(A_Q_W), lag_row(B_VAL_W),
                   pl.BlockSpec((None, B_HEADS, B_KEY_DIM, B_VAL_DIM),
                                lambda s: (jnp.maximum(s - 1, 0) // tiles_per_seq, 0, 0, 0))],
        out_shape=[jax.ShapeDtypeStruct((n, 2 * A_KV_W), F32),
                   jax.ShapeDtypeStruct((n, 2 * D_MODEL), F32),
                   jax.ShapeDtypeStruct((n, A_Q_W), BF16),
                   jax.ShapeDtypeStruct((n, B_VAL_W), BF16),
                   jax.ShapeDtypeStruct((n // seq, B_HEADS, B_KEY_DIM, B_VAL_DIM), F32)],
        scratch_shapes=[pltpu.VMEM((tm, D_MODEL), BF16),
                        pltpu.VMEM((2, tm, A_Q_W), BF16),
                        pltpu.VMEM((2, tm, 2 * A_KV_W), BF16),
                        pltpu.VMEM((WINDOW + tm, 2 * A_KV_W), BF16),
                        pltpu.VMEM((2, tm, _HB_W), F32),
                        pltpu.VMEM((B_HEADS, B_VAL_DIM, B_KEY_DIM), F32)],
        compiler_params=pltpu.CompilerParams(
            dimension_semantics=("arbitrary",), vmem_limit_bytes=_VMEM_LIMIT),
        name="mixer",
    )(x2d, g_pre, w_bf, lb_logits, g_hgrn, bias_ext)


def _merge_body(ya_ref, yb_ref, gg_ref, x_ref, wa_ref, wb_ref, wo_ref, g_ref, o_ref):
    ga = gg_ref[:, 0:D_MODEL]
    gb = gg_ref[:, D_MODEL:2 * D_MODEL]
    mix = (_sigmoid_of_twice(ga) * _dot(ya_ref[...], wa_ref[...])
           + _sigmoid_of_twice(gb) * _dot(yb_ref[...], wb_ref[...]))
    r = _dot(mix.astype(BF16), wo_ref[...])
    o_ref[...] = x_ref[...] + _rms(r, g_ref[...])


def _merge(ya, yb, gg, x2d, wa, wb, wo, g, tm):
    n = x2d.shape[0]
    row = lambda w: pl.BlockSpec((tm, w), lambda i: (i, 0))
    wspec = _const_spec((D_MODEL, D_MODEL))
    return pl.pallas_call(
        _merge_body,
        grid=(n // tm,),
        in_specs=[row(A_Q_W), row(B_VAL_W), row(2 * D_MODEL), row(D_MODEL),
                  wspec, wspec, wspec, _const_spec((1, D_MODEL))],
        out_specs=row(D_MODEL),
        out_shape=jax.ShapeDtypeStruct((n, D_MODEL), F32),
        compiler_params=pltpu.CompilerParams(
            dimension_semantics=("parallel",), vmem_limit_bytes=_VMEM_LIMIT),
        name="merge",
    )(ya, yb, gg, x2d, wa, wb, wo, g)


def _gelu_tanh(x):
    c = math.sqrt(2.0 / math.pi)
    return 0.5 * x * (1.0 + jnp.tanh(c * (x + 0.044715 * (x * x * x))))


def _ffn_body(*refs, tm, has_prev):
    if has_prev:
        (x_ref, pe_ref, prev_ref, gpre_ref, wup_ref, wconv_ref, bconv_ref, wdown_ref, gpost_ref,
         wple_ref, wgate_ref, o_ref, tail_ref, carry_s) = refs
    else:
        (x_ref, pe_ref, gpre_ref, wup_ref, wconv_ref, bconv_ref, wdown_ref, gpost_ref,
         wple_ref, wgate_ref, o_ref, tail_ref, carry_s) = refs
    i = pl.program_id(1)

    @pl.when(i == 0)
    def _():
        if has_prev:
            carry_s[...] = prev_ref[...]
        else:
            carry_s[...] = jnp.zeros((CONV_W - 1, D_FF), F32)

    x = x_ref[...]
    hf = _rms(x, gpre_ref[...]).astype(BF16)
    a = _dot(hf, wup_ref[:, 0:D_FF])
    u = _dot(hf, wup_ref[:, D_FF:2 * D_FF])
    row = lax.broadcasted_iota(jnp.int32, (tm, D_FF), 0)
    c0 = carry_s[0:1, :]
    c1 = carry_s[1:2, :]
    a1 = jnp.where(row == 0, c1, pltpu.roll(a, 1, 0))
    a2 = jnp.where(row == 0, c0, jnp.where(row == 1, c1, pltpu.roll(a, 2, 0)))
    ac = bconv_ref[...] + a2 * wconv_ref[0:1, :] + a1 * wconv_ref[1:2, :] + a * wconv_ref[2:3, :]
    tail = a[tm - (CONV_W - 1):tm, :]
    carry_s[...] = tail
    tail_ref[...] = tail
    gl = (_gelu_tanh(ac) * u).astype(BF16)
    x2 = x + _rms(_dot(gl, wdown_ref[...]), gpost_ref[...])
    pe = _dot(pe_ref[...].astype(BF16), wple_ref[...])
    gate = _sigmoid_of_twice(_dot(x2.astype(BF16), wgate_ref[...]))
    o_ref[...] = x2 + pe * gate


def _ffn(x, pe, prev, gpre, wup, wconv, bconv, wdown, gpost, wple, wgate, tm):
    bsz, t, _ = x.shape
    has_prev = prev is not None
    body = functools.partial(_ffn_body, tm=tm, has_prev=has_prev)
    tail_spec = pl.BlockSpec((None, CONV_W - 1, D_FF), lambda b, i: (b, 0, 0))
    in_specs = [pl.BlockSpec((None, tm, D_MODEL), lambda b, i: (b, i, 0)),
                pl.BlockSpec((None, tm, PLE_DIM), lambda b, i: (b, i, 0))]
    args = [x, pe]
    if has_prev:
        in_specs.append(tail_spec)
        args.append(prev)
    in_specs += [_const_spec((1, D_MODEL)), _const_spec((D_MODEL, 2 * D_FF)),
                 _const_spec((CONV_W, D_FF)), _const_spec((1, D_FF)),
                 _const_spec((D_FF, D_MODEL)), _const_spec((1, D_MODEL)),
                 _const_spec((PLE_DIM, D_MODEL)), _const_spec((D_MODEL, D_MODEL))]
    args += [gpre, wup, wconv, bconv, wdown, gpost, wple, wgate]
    return pl.pallas_call(
        body,
        grid=(bsz, t // tm),
        in_specs=in_specs,
        out_specs=[pl.BlockSpec((None, tm, D_MODEL), lambda b, i: (b, i, 0)), tail_spec],
        out_shape=[jax.ShapeDtypeStruct((bsz, t, D_MODEL), F32),
                   jax.ShapeDtypeStruct((bsz, CONV_W - 1, D_FF), F32)],
        scratch_shapes=[pltpu.VMEM((CONV_W - 1, D_FF), F32)],
        compiler_params=pltpu.CompilerParams(
            dimension_semantics=("parallel", "arbitrary"), vmem_limit_bytes=_VMEM_LIMIT),
        name="convffn",
    )(*args)


def _t5_bucket(rel):
    nb = NUM_BUCKETS // 2
    ret = jnp.where(rel > 0, nb, 0)
    n = jnp.abs(rel)
    max_exact = nb // 2
    large = max_exact + (jnp.log(jnp.maximum(n, max_exact).astype(jnp.float32) / max_exact)
                         / math.log(MAX_DISTANCE / max_exact) * (nb - max_exact)).astype(jnp.int32)
    large = jnp.minimum(large, nb - 1)
    return ret + jnp.where(n < max_exact, n, large)


def _bias_body(table_ref, sinks_ref, bk_ref, o_ref, *, nvar):
    bk = bk_ref[...]
    row = lax.broadcasted_iota(jnp.int32, bk.shape, 0)
    for k in range(A_KV_HEADS):
        acc = jnp.where(bk == -1, sinks_ref[k], NEG_INF)
        for b in range(NUM_BUCKETS):
            acc = jnp.where(bk == b, table_ref[k, b:b + 1, :], acc)
        for v in range(nvar):
            o_ref[v, k] = jnp.where(row < v * CHUNK, NEG_INF, acc)


def _bias_ext(table, sinks, lq, lk, nvar):
    rows = _score_rows(lk)
    q_pos = jnp.arange(lq) + WINDOW
    k_pos = jnp.arange(lk)
    buckets = _t5_bucket(k_pos[:, None] - q_pos[None, :]).astype(jnp.int32)
    bk = jnp.concatenate([buckets, jnp.full((1, lq), -1, jnp.int32),
                          jnp.full((rows - lk - 1, lq), -2, jnp.int32)], axis=0)
    bk = jnp.tile(bk, (1, A_GROUP))
    tab = jnp.repeat(table.astype(F32).reshape(NUM_BUCKETS, A_KV_HEADS, A_GROUP), lq, axis=2)
    tab = jnp.transpose(tab, (1, 0, 2))
    snk = jnp.repeat(sinks.astype(F32).reshape(A_KV_HEADS, 1, A_GROUP), lq, axis=2)
    vmem = pl.BlockSpec(memory_space=pltpu.VMEM)
    return pl.pallas_call(
        functools.partial(_bias_body, nvar=nvar),
        in_specs=[vmem, vmem, vmem],
        out_specs=vmem,
        out_shape=jax.ShapeDtypeStruct((nvar, A_KV_HEADS, rows, A_GROUP * lq), F32),
        name="relbias",
    )(tab, snk, bk)


def _prompt_layer(x, pe, w, *, tm_mix, tm_tok, tm_ffn):
    bsz, t, _ = x.shape
    n = bsz * t
    x2d = x.reshape(n, D_MODEL)
    bias_ext = _bias_ext(w["rel_table"], w["sinks"], CHUNK, WINDOW + CHUNK, WINDOW // CHUNK + 1)
    kv, gg, ya, yb, s_fin = _mixer(x2d, w["g_pre_mix"], w["w_in"], w["lb_logits"], w["g_hgrn_out"],
                                   bias_ext, tm=tm_mix, seq=t)
    x1 = _merge(ya, yb, gg, x2d, w["w_br_a"], w["w_br_b"], w["w_out"], w["g_post_mix"], tm_tok)
    y, conv_tail = _ffn(x1.reshape(bsz, t, D_MODEL), pe, None, w["g_pre_ffn"], w["w_up"],
                        w["w_conv"], w["b_conv"], w["w_down"], w["g_post_ffn"], w["w_ple"],
                        w["w_ple_gate"], tm_ffn)
    return y, kv.reshape(bsz, t, 2 * A_KV_W), s_fin, conv_tail


def _sample_layer(x, pe, kv_prev, s_prev, conv_prev, w, *, tm_tok):
    bsz, t, _ = x.shape
    n = bsz * t
    x2d = x.reshape(n, D_MODEL)
    qa, kv, hb, gg = _inproj(x2d, w["g_pre_mix"], w["w_in"], tm_tok)
    bias_ext = _bias_ext(w["rel_table"], w["sinks"], t, WINDOW + t, 1)
    kv3 = kv.reshape(bsz, t, 2 * A_KV_W)
    ya = _attention_cached(qa.reshape(bsz, t, A_Q_W), kv3, kv_prev, bias_ext, lq=t)
    yb, s_fin = _hgrn_cached(hb.reshape(bsz, t, _HB_W), w["lb_logits"], w["g_hgrn_out"], s_prev, blk=t)
    x1 = _merge(ya.reshape(n, A_Q_W), yb.reshape(n, B_VAL_W), gg, x2d,
                w["w_br_a"], w["w_br_b"], w["w_out"], w["g_post_mix"], tm_tok)
    y, conv_tail = _ffn(x1.reshape(bsz, t, D_MODEL), pe, conv_prev, w["g_pre_ffn"], w["w_up"],
                        w["w_conv"], w["b_conv"], w["w_down"], w["g_post_ffn"], w["w_ple"],
                        w["w_ple_gate"], t)
    return y, kv3, s_fin, conv_tail


def _scale_in_cols(w_in):
    h = B_KEY_W
    scale = jnp.concatenate([
        jnp.full((A_Q_W,), A_HEAD_DIM ** -0.5, F32), jnp.ones((2 * A_KV_W,), F32),
        jnp.full((2 * h,), 0.5, F32), jnp.ones((B_VAL_W,), F32), jnp.full((B_VAL_W,), 0.5, F32),
        jnp.full((2 * D_MODEL,), 0.5, F32)])
    return w_in * scale[None, :]


def kernel(x_prompt, x_sample, cache_win_k, cache_win_v, state_hgrn, cache_ffn_conv, p_prompt, p_sample,
           rel_bias_table, lb_logits, g_pre_mix, w_in, attn_sinks, g_hgrn_out, w_br_a, w_br_b, w_out,
           g_post_mix, g_pre_ffn, w_up, w_conv, b_conv, w_down, g_post_ffn, w_ple, w_ple_gate):
    bsz, seq, _ = x_prompt.shape
    dbsz, dseq, _ = x_sample.shape
    w = {
        "rel_table": rel_bias_table, "sinks": attn_sinks[0], "lb_logits": lb_logits.astype(F32),
        "g_pre_mix": g_pre_mix[0][None, :], "w_in": _scale_in_cols(w_in[0]).astype(BF16),
        "g_hgrn_out": g_hgrn_out[0][None, :],
        "w_br_a": w_br_a[0].astype(BF16), "w_br_b": w_br_b[0].astype(BF16), "w_out": w_out[0].astype(BF16),
        "g_post_mix": g_post_mix[0][None, :], "g_pre_ffn": g_pre_ffn[0][None, :],
        "w_up": w_up[0].astype(BF16), "w_conv": w_conv[0], "b_conv": b_conv[0][None, :],
        "w_down": w_down[0].astype(BF16), "g_post_ffn": g_post_ffn[0][None, :],
        "w_ple": w_ple[0].astype(BF16), "w_ple_gate": (0.5 * w_ple_gate[0]).astype(BF16),
    }
    yp, kvp, sp, cp = _prompt_layer(x_prompt, p_prompt[0], w, tm_mix=256, tm_tok=256, tm_ffn=256)
    wc = cache_win_k.shape[2]
    kv_cache = jnp.concatenate([cache_win_k[0].reshape(dbsz, wc, A_KV_W),
                                cache_win_v[0].reshape(dbsz, wc, A_KV_W)], axis=-1)
    ys, kvs, ss, cs = _sample_layer(x_sample, p_sample[0], kv_cache, state_hgrn[0], cache_ffn_conv[0], w,
                                    tm_tok=256)
    keep = min(WINDOW, seq)

    def heads(a):
        return a.reshape(a.shape[0], a.shape[1], A_KV_HEADS, A_HEAD_DIM)[None]

    return (yp, ys,
            heads(kvp[:, seq - keep:, 0:A_KV_W]), heads(kvp[:, seq - keep:, A_KV_W:]),
            sp[None], cp[None],
            heads(kvs[:, :, 0:A_KV_W]), heads(kvs[:, :, A_KV_W:]),
            ss[None], cs[None])
```

```python
import functools
import math

import jax
import jax.numpy as jnp
from jax import lax
from jax.experimental import pallas as pl
from jax.experimental.pallas import tpu as pltpu

D_MODEL = 1024
CHUNK = 64
A_HEADS = 16
A_KV_HEADS = 2
A_HEAD_DIM = 64
A_GROUP = A_HEADS // A_KV_HEADS
WINDOW = 128
A_Q_W = A_HEADS * A_HEAD_DIM
A_KV_W = A_KV_HEADS * A_HEAD_DIM
NUM_BUCKETS = 32
MAX_DISTANCE = 128
B_HEADS = 8
B_KEY_DIM = 128
B_VAL_DIM = D_MODEL // B_HEADS
B_KEY_W = B_HEADS * B_KEY_DIM
B_VAL_W = B_HEADS * B_VAL_DIM
D_FF = 2816
CONV_W = 3
PLE_DIM = 256
EPS = 1e-6
NEG_INF = -1e30

_QA0 = 0
_KV0 = A_Q_W
_HB0 = _KV0 + 2 * A_KV_W
_GG0 = _HB0 + 2 * B_KEY_W + 2 * B_VAL_W
IN_COLS = _GG0 + 2 * D_MODEL
_HB_W = _GG0 - _HB0

_BF16_ROWS = 16
_VMEM_LIMIT = 56 * 1024 * 1024

BF16 = jnp.bfloat16
F32 = jnp.float32


def _score_rows(lk):
    return lk + _BF16_ROWS


def _const_spec(shape):
    nd = len(shape)
    return pl.BlockSpec(shape, lambda *_: (0,) * nd, pipeline_mode=pl.Buffered(1))


def _rms(x, g):
    ms = jnp.mean(x * x, axis=-1, keepdims=True)
    return x * lax.rsqrt(ms + EPS) * g


def _sigmoid_of_twice(hx):
    return 0.5 * jnp.tanh(hx) + 0.5


def _dot(a, b):
    return jnp.dot(a, b, preferred_element_type=F32)


def _dot_nt(a, b):
    return lax.dot_general(a, b, (((1,), (1,)), ((), ())), preferred_element_type=F32)


def _dot_tn(a, b):
    return lax.dot_general(a, b, (((0,), (0,)), ((), ())), preferred_element_type=F32)


def _interleave(a, b):
    out, nb = [], 0
    for i, t in enumerate(a):
        out.append(t)
        want = ((i + 1) * len(b)) // len(a)
        out.extend(b[nb:want])
        nb = want
    return out + b[nb:]


def _attn_thunks(nchunk, lq, lk, load_q, load_kw, bias_strip, store_o):
    def scores(c):
        qc = load_q(c)
        kw = load_kw(c)
        st = []
        for k in range(A_KV_HEADS):
            qs = jnp.concatenate(
                [qc[:, (k * A_GROUP + g) * A_HEAD_DIM:(k * A_GROUP + g + 1) * A_HEAD_DIM]
                 for g in range(A_GROUP)], axis=0)
            st.append(_dot_nt(kw[:, k * A_HEAD_DIM:(k + 1) * A_HEAD_DIM], qs))
        return dict(kw=kw, st=st)

    def softmax(c, s):
        ot, rden = [], []
        for k in range(A_KV_HEADS):
            ps, rs = [], []
            for j in range(0, A_GROUP * lq, 128):
                t = s["st"][k][:, j:j + 128] + bias_strip(c, k, j)
                m = jnp.max(t, axis=0, keepdims=True)
                p = jnp.exp(t - m)
                rs.append(1.0 / jnp.sum(p, axis=0, keepdims=True))
                ps.append(p.astype(BF16))
            rden.append(jnp.concatenate(rs, axis=1))
            vv = s["kw"][:, A_KV_W + k * A_HEAD_DIM:A_KV_W + (k + 1) * A_HEAD_DIM]
            ot.append(_dot_tn(vv, jnp.concatenate(ps, axis=1)))
        return dict(ot=ot, rden=rden)

    def out(c, s):
        outs = []
        for k in range(A_KV_HEADS):
            o = (s["ot"][k] * s["rden"][k]).T
            outs.append(jnp.concatenate([o[g * lq:(g + 1) * lq, :] for g in range(A_GROUP)], axis=1))
        store_o(c, jnp.concatenate(outs, axis=1).astype(BF16))

    ahead = 2
    sc, sm, th = {}, {}, []

    def do_scores(c):
        sc[c] = scores(c)

    def do_softmax(c):
        sm[c] = softmax(c, sc.pop(c))

    def do_out(c):
        out(c, sm.pop(c))

    for c in range(min(ahead, nchunk)):
        th.append(functools.partial(do_scores, c))
    for c in range(nchunk):
        if c + ahead < nchunk:
            th.append(functools.partial(do_scores, c + ahead))
        th.append(functools.partial(do_softmax, c))
        if c >= 1:
            th.append(functools.partial(do_out, c - 1))
    th.append(functools.partial(do_out, nchunk - 1))
    return th


def _attn_body(q_ref, kvc_ref, kvp_ref, bias_ref, o_ref, kv_s, *, lq, lk, nchunk, tq):
    kv_s[0:WINDOW, :] = kvp_ref[...].astype(BF16)
    kv_s[WINDOW:WINDOW + tq, :] = kvc_ref[...].astype(BF16)
    zpad = jnp.zeros((_score_rows(lk) - lk, 2 * A_KV_W), BF16)

    def store_o(c, o):
        o_ref[c * lq:(c + 1) * lq, :] = o

    for t in _attn_thunks(
            nchunk, lq, lk,
            load_q=lambda c: q_ref[c * lq:(c + 1) * lq, :],
            load_kw=lambda c: jnp.concatenate([kv_s[c * lq:c * lq + lk, :], zpad], axis=0),
            bias_strip=lambda c, k, j: bias_ref[0, k, :, j:j + 128],
            store_o=store_o):
        t()


def _attention_cached(qa, kv, kv_prev, bias_ext, *, lq):
    bsz, t, _ = qa.shape
    lk = WINDOW + lq
    body = functools.partial(_attn_body, lq=lq, lk=lk, nchunk=t // lq, tq=t)
    return pl.pallas_call(
        body,
        grid=(bsz,),
        in_specs=[
            pl.BlockSpec((None, t, A_Q_W), lambda b: (b, 0, 0)),
            pl.BlockSpec((None, t, 2 * A_KV_W), lambda b: (b, 0, 0)),
            pl.BlockSpec((None, WINDOW, 2 * A_KV_W), lambda b: (b, 0, 0)),
            _const_spec(bias_ext.shape),
        ],
        out_specs=pl.BlockSpec((None, t, A_Q_W), lambda b: (b, 0, 0)),
        out_shape=jax.ShapeDtypeStruct((bsz, t, A_Q_W), BF16),
        scratch_shapes=[pltpu.VMEM((WINDOW + t, 2 * A_KV_W), BF16)],
        compiler_params=pltpu.CompilerParams(
            dimension_semantics=("parallel",), vmem_limit_bytes=_VMEM_LIMIT),
        name="attention",
    )(qa, kv, kv_prev, bias_ext)


def _cumsum_rows(x, tril3):
    hi = x.astype(BF16)
    r = x - hi.astype(F32)
    mid = r.astype(BF16)
    lo = (r - mid.astype(F32)).astype(BF16)
    return _dot(tril3, jnp.concatenate([hi, mid, lo], axis=0))


def _hgrn_thunks(nchunk, blk, load, store_y, st_s, lbl, g):
    e = jnp.exp(lbl - jnp.max(lbl, axis=0, keepdims=True))
    lb = e[0:1, :] / jnp.sum(e, axis=0, keepdims=True)
    fa = 0.5 * (1.0 + lb)
    fb = 0.5 * (1.0 - lb)
    ri = lax.broadcasted_iota(jnp.int32, (blk, blk), 0)
    ci = lax.broadcasted_iota(jnp.int32, (blk, blk), 1)
    tril = (ri >= ci).astype(BF16)
    tril3 = jnp.concatenate([tril, tril, tril], axis=1)
    ri2 = lax.broadcasted_iota(jnp.int32, (blk, 2 * blk), 0)
    ci2 = lax.broadcasted_iota(jnp.int32, (blk, 2 * blk), 1)
    causal2 = ri2 >= (ci2 & (blk - 1))
    mid = blk // 2
    w = B_KEY_W
    pw = 2 * B_KEY_DIM
    npair = B_HEADS // 2
    hs = [slice(h * B_KEY_DIM, (h + 1) * B_KEY_DIM) for h in range(B_HEADS)]
    ps = [slice(j * pw, (j + 1) * pw) for j in range(npair)]

    def blockdiag(x0, x1):
        z = jnp.zeros_like(x0)
        return jnp.concatenate([jnp.concatenate([x0, z], axis=1), jnp.concatenate([z, x1], axis=1)], axis=0)

    def stage_decay(c):
        bt = fb * jnp.tanh(load(c, w, 2 * w))
        f = fa + bt
        cum = _cumsum_rows(jnp.log2(f), tril3)
        return dict(kk=fb - bt, cum=cum)

    def stage_state(c, s):
        cum = s["cum"]
        hq = load(c, 0, w)
        qs = hq + hq * jnp.tanh(hq)
        b_last = cum[blk - 1:blk, :]
        b_mid = cum[mid:mid + 1, :]
        q2f = qs * jnp.exp2(cum - b_mid)
        k2f = s["kk"] * jnp.exp2(b_mid - cum)
        q1 = (q2f * jnp.exp2(b_mid)).astype(BF16)
        k3 = (k2f * jnp.exp2(b_last - b_mid)).astype(BF16)
        q2 = q2f.astype(BF16)
        k2 = k2f.astype(BF16)
        vb = load(c, 2 * w, 3 * w).astype(BF16)
        dec = jnp.exp2(b_last)
        a = [_dot_nt(q2[:, ps[j]], blockdiag(k2[:, hs[2 * j]], k2[:, hs[2 * j + 1]])) for j in range(npair)]
        st = [st_s[h] for h in range(B_HEADS)]
        o1 = [_dot_nt(q1[:, ps[j]], blockdiag(st[2 * j].astype(BF16), st[2 * j + 1].astype(BF16)))
              for j in range(npair)]
        upd = []
        for q in range(0, B_HEADS, 4):
            vstack = jnp.concatenate([vb[:, hs[h]] for h in range(q, q + 4)], axis=0)
            kdiag = jnp.concatenate(
                [jnp.concatenate([k3[:, hs[h]] if h == h2 else jnp.zeros((blk, B_KEY_DIM), BF16)
                                  for h2 in range(q, q + 4)], axis=1) for h in range(q, q + 4)], axis=0)
            upd.append(_dot_tn(vstack, kdiag))
        for h, sl in enumerate(hs):
            st_s[h] = dec[:, sl] * st[h] + upd[h // 4][:, (h % 4) * B_KEY_DIM:(h % 4 + 1) * B_KEY_DIM]
        return dict(a=a, o1=o1, vb=vb)

    def stage_out(c, s):
        vb = s["vb"]
        am = [jnp.where(causal2, s["a"][j], 0.0).astype(BF16) for j in range(npair)]
        o2 = [_dot(am[j], blockdiag(vb[:, hs[2 * j]], vb[:, hs[2 * j + 1]])) for j in range(npair)]
        o = [s["o1"][j] + o2[j] for j in range(npair)]
        ys = [_rms(o[h // 2][:, (h % 2) * B_VAL_DIM:(h % 2 + 1) * B_VAL_DIM], g) for h in range(B_HEADS)]
        hog = load(c, 3 * w, 4 * w)
        y = jnp.concatenate(ys, axis=1) * (hog + hog * jnp.tanh(hog))
        store_y(c, y.astype(BF16))

    dec, sta, th = {}, {}, []

    def do_decay(c):
        dec[c] = stage_decay(c)

    def do_state(c):
        sta[c] = stage_state(c, dec.pop(c))

    def do_out(c):
        stage_out(c, sta.pop(c))

    th.append(functools.partial(do_decay, 0))
    for c in range(nchunk):
        if c + 1 < nchunk:
            th.append(functools.partial(do_decay, c + 1))
        th.append(functools.partial(do_state, c))
        if c >= 1:
            th.append(functools.partial(do_out, c - 1))
    th.append(functools.partial(do_out, nchunk - 1))
    return th


def _hgrn_body(hb_ref, lbl_ref, g_ref, s0_ref, yb_ref, sfin_ref, st_s, *, blk, nchunk):
    for h in range(B_HEADS):
        st_s[h] = s0_ref[h].T

    def store_y(c, y):
        yb_ref[c * blk:(c + 1) * blk, :] = y

    for t in _hgrn_thunks(nchunk, blk, lambda c, lo, hi: hb_ref[c * blk:(c + 1) * blk, lo:hi],
                          store_y, st_s, lbl_ref[...], g_ref[...]):
        t()
    for h in range(B_HEADS):
        sfin_ref[h] = st_s[h].T


def _hgrn_cached(hb, lb_logits, g_out, s0, *, blk):
    bsz, t, _ = hb.shape
    body = functools.partial(_hgrn_body, blk=blk, nchunk=t // blk)
    st_spec = pl.BlockSpec((None, B_HEADS, B_KEY_DIM, B_VAL_DIM), lambda b: (b, 0, 0, 0))
    return pl.pallas_call(
        body,
        grid=(bsz,),
        in_specs=[pl.BlockSpec((None, t, _HB_W), lambda b: (b, 0, 0)),
                  _const_spec(lb_logits.shape), _const_spec((1, B_VAL_DIM)), st_spec],
        out_specs=[pl.BlockSpec((None, t, B_VAL_W), lambda b: (b, 0, 0)), st_spec],
        out_shape=[jax.ShapeDtypeStruct((bsz, t, B_VAL_W), BF16),
                   jax.ShapeDtypeStruct((bsz, B_HEADS, B_KEY_DIM, B_VAL_DIM), F32)],
        scratch_shapes=[pltpu.VMEM((B_HEADS, B_VAL_DIM, B_KEY_DIM), F32)],
        compiler_params=pltpu.CompilerParams(
            dimension_semantics=("parallel",), vmem_limit_bytes=_VMEM_LIMIT),
        name="hgrn2",
    )(hb, lb_logits, g_out, s0)


def _inproj_body(x_ref, g_ref, w_ref, qa_ref, kv_ref, hb_ref, gg_ref):
    h = _rms(x_ref[...], g_ref[...]).astype(BF16)
    step = 512

    def mm(lo, width):
        return _dot(h, w_ref[:, lo:lo + width])

    for j in range(0, A_Q_W, step):
        qa_ref[:, j:j + step] = mm(_QA0 + j, step).astype(BF16)
    kv_ref[...] = mm(_KV0, 2 * A_KV_W)
    for j in range(0, _HB_W, step):
        hb_ref[:, j:j + step] = mm(_HB0 + j, step)
    for j in range(0, 2 * D_MODEL, step):
        gg_ref[:, j:j + step] = mm(_GG0 + j, step)


def _inproj(x2d, g, w_bf, tm):
    n = x2d.shape[0]
    row = lambda w: pl.BlockSpec((tm, w), lambda i: (i, 0))
    return pl.pallas_call(
        _inproj_body,
        grid=(n // tm,),
        in_specs=[row(D_MODEL), _const_spec((1, D_MODEL)), _const_spec((D_MODEL, IN_COLS))],
        out_specs=[row(A_Q_W), row(2 * A_KV_W), row(_HB_W), row(2 * D_MODEL)],
        out_shape=[
            jax.ShapeDtypeStruct((n, A_Q_W), BF16),
            jax.ShapeDtypeStruct((n, 2 * A_KV_W), F32),
            jax.ShapeDtypeStruct((n, _HB_W), F32),
            jax.ShapeDtypeStruct((n, 2 * D_MODEL), F32),
        ],
        compiler_params=pltpu.CompilerParams(
            dimension_semantics=("parallel",), vmem_limit_bytes=_VMEM_LIMIT),
        name="inproj",
    )(x2d, g, w_bf)


def _mixer_body(x_ref, gpre_ref, w_ref, lbl_ref, ghg_ref, bias_ref,
                kv_ref, gg_ref, ya_ref, yb_ref, sfin_ref,
                h_s, qa_s, kvb_s, kvw_s, hb_s, st_s, *, tm, tiles_per_seq):
    s = pl.program_id(0)
    cur = s % 2
    prv = 1 - cur
    tib = (s + tiles_per_seq - 1) % tiles_per_seq
    nchunk = tm // CHUNK
    lk = WINDOW + CHUNK
    piece = 256

    @pl.when(s == 0)
    def _():
        qa_s[...] = jnp.zeros(qa_s.shape, BF16)
        kvb_s[...] = jnp.zeros(kvb_s.shape, BF16)
        kvw_s[...] = jnp.zeros(kvw_s.shape, BF16)
        hb_s[...] = jnp.zeros(hb_s.shape, F32)

    @pl.when((s == 0) | (tib == 0))
    def _():
        st_s[...] = jnp.zeros(st_s.shape, F32)

    kvw_s[0:WINDOW, :] = kvw_s[tm:tm + WINDOW, :]
    kvw_s[WINDOW:WINDOW + tm, :] = kvb_s[prv]

    def dense_piece(lo):
        z = _dot(h_s[...], w_ref[:, lo:lo + piece])
        if lo < _KV0:
            qa_s[cur, :, lo:lo + piece] = z.astype(BF16)
        elif lo < _HB0:
            kv_ref[...] = z
            kvb_s[cur] = z.astype(BF16)
        elif lo < _GG0:
            hb_s[cur, :, lo - _HB0:lo - _HB0 + piece] = z
        else:
            gg_ref[:, lo - _GG0:lo - _GG0 + piece] = z

    dense = {lo: functools.partial(dense_piece, lo) for lo in range(0, IN_COLS, piece)}
    dense_hb = [t for lo, t in dense.items() if _HB0 <= lo < _GG0]
    dense_rest = [t for lo, t in dense.items() if not _HB0 <= lo < _GG0]

    def store_ya(c, o):
        ya_ref[c * CHUNK:(c + 1) * CHUNK, :] = o

    def store_yb(c, y):
        yb_ref[c * CHUNK:(c + 1) * CHUNK, :] = y

    zpad = jnp.zeros((_score_rows(lk) - lk, 2 * A_KV_W), BF16)

    def bias_strip(c, k, j):
        var = jnp.clip(WINDOW // CHUNK - (tib * nchunk + c), 0, WINDOW // CHUNK)
        return bias_ref[var, k, :, j:j + 128]

    attn = _attn_thunks(
        nchunk, CHUNK, lk,
        load_q=lambda c: qa_s[prv, c * CHUNK:(c + 1) * CHUNK, :],
        load_kw=lambda c: jnp.concatenate([kvw_s[c * CHUNK:c * CHUNK + lk, :], zpad], axis=0),
        bias_strip=bias_strip, store_o=store_ya)
    hgrn = _hgrn_thunks(
        nchunk, CHUNK, lambda c, lo, hi: hb_s[prv, c * CHUNK:(c + 1) * CHUNK, lo:hi],
        store_yb, st_s, lbl_ref[...], ghg_ref[...])
    for t in attn[:2]:
        t()
    h_s[...] = _rms(x_ref[...], gpre_ref[...]).astype(BF16)
    for t in _interleave(dense_hb, attn[2:]) + _interleave(dense_rest, hgrn):
        t()

    @pl.when((tib == tiles_per_seq - 1) & (s > 0))
    def _():
        for h in range(B_HEADS):
            sfin_ref[h] = st_s[h].T


def _mixer(x2d, g_pre, w_bf, lb_logits, g_hgrn, bias_ext, *, tm, seq):
    n = x2d.shape[0]
    nt = n // tm
    tiles_per_seq = seq // tm
    dense_row = lambda w: pl.BlockSpec((tm, w), lambda s: (jnp.minimum(s, nt - 1), 0))
    lag_row = lambda w: pl.BlockSpec((tm, w), lambda s: (jnp.maximum(s - 1, 0), 0))
    body = functools.partial(_mixer_body, tm=tm, tiles_per_seq=tiles_per_seq)
    return pl.pallas_call(
        body,
        grid=(nt + 1,),
        in_specs=[dense_row(D_MODEL), _const_spec((1, D_MODEL)), _const_spec((D_MODEL, IN_COLS)),
                  _const_spec(lb_logits.shape), _const_spec((1, B_VAL_DIM)), _const_spec(bias_ext.shape)],
        out_specs=[dense_row(2 * A_KV_W), dense_row(2 * D_MODEL), lag_row(A_Q_W), lag_row(B_VAL_W),
                   pl.BlockSpec((None, B_HEADS, B_KEY_DIM, B_VAL_DIM),
                                lambda s: (jnp.maximum(s - 1, 0) // tiles_per_seq, 0, 0, 0))],
        out_shape=[jax.ShapeDtypeStruct((n, 2 * A_KV_W), F32),
                   jax.ShapeDtypeStruct((n, 2 * D_MODEL), F32),
                   jax.ShapeDtypeStruct((n, A_Q_W), BF16),
                   jax.ShapeDtypeStruct((n, B_VAL_W), BF16),
                   jax.ShapeDtypeStruct((n // seq, B_HEADS, B_KEY_DIM, B_VAL_DIM), F32)],
        scratch_shapes=[pltpu.VMEM((tm, D_MODEL), BF16),
                        pltpu.VMEM((2, tm, A_Q_W), BF16),
                        pltpu.VMEM((2, tm, 2 * A_KV_W), BF16),
                        pltpu.VMEM((WINDOW + tm, 2 * A_KV_W), BF16),
                        pltpu.VMEM((2, tm, _HB_W), F32),
                        pltpu.VMEM((B_HEADS, B_VAL_DIM, B_KEY_DIM), F32)],
        compiler_params=pltpu.CompilerParams(
            dimension_semantics=("arbitrary",), vmem_limit_bytes=_VMEM_LIMIT),
        name="mixer",
    )(x2d, g_pre, w_bf, lb_logits, g_hgrn, bias_ext)


def _merge_body(ya_ref, yb_ref, gg_ref, x_ref, wa_ref, wb_ref, wo_ref, g_ref, o_ref, *, tm, strip):
    def branches(r):
        rows = slice(r * strip, (r + 1) * strip)
        ga = gg_ref[rows, 0:D_MODEL]
        gb = gg_ref[rows, D_MODEL:2 * D_MODEL]
        mix = (_sigmoid_of_twice(ga) * _dot(ya_ref[rows, :], wa_ref[...])
               + _sigmoid_of_twice(gb) * _dot(yb_ref[rows, :], wb_ref[...]))
        return mix.astype(BF16)

    def project(r, mix):
        rows = slice(r * strip, (r + 1) * strip)
        o_ref[rows, :] = x_ref[rows, :] + _rms(_dot(mix, wo_ref[...]), g_ref[...])

    nstrip = tm // strip
    mix = branches(0)
    for r in range(nstrip):
        nxt = branches(r + 1) if r + 1 < nstrip else None
        project(r, mix)
        mix = nxt


def _merge(ya, yb, gg, x2d, wa, wb, wo, g, tm):
    n = x2d.shape[0]
    row = lambda w: pl.BlockSpec((tm, w), lambda i: (i, 0))
    wspec = _const_spec((D_MODEL, D_MODEL))
    return pl.pallas_call(
        functools.partial(_merge_body, tm=tm, strip=min(tm, 256)),
        grid=(n // tm,),
        in_specs=[row(A_Q_W), row(B_VAL_W), row(2 * D_MODEL), row(D_MODEL),
                  wspec, wspec, wspec, _const_spec((1, D_MODEL))],
        out_specs=row(D_MODEL),
        out_shape=jax.ShapeDtypeStruct((n, D_MODEL), F32),
        compiler_params=pltpu.CompilerParams(
            dimension_semantics=("parallel",), vmem_limit_bytes=_VMEM_LIMIT),
        name="merge",
    )(ya, yb, gg, x2d, wa, wb, wo, g)


def _gelu_tanh(x):
    c = math.sqrt(2.0 / math.pi)
    return 0.5 * x * (1.0 + jnp.tanh(c * (x + 0.044715 * (x * x * x))))


def _ffn_tile(x, pe, shifted, gpre_ref, wup_ref, wconv_ref, bconv_ref, wdown_ref, gpost_ref,
              wple_ref, wgate_ref):
    hf = _rms(x, gpre_ref[...]).astype(BF16)
    a = _dot(hf, wup_ref[:, 0:D_FF])
    u = _dot(hf, wup_ref[:, D_FF:2 * D_FF])
    a1, a2 = shifted(a)
    ac = bconv_ref[...] + a2 * wconv_ref[0:1, :] + a1 * wconv_ref[1:2, :] + a * wconv_ref[2:3, :]
    gl = (_gelu_tanh(ac) * u).astype(BF16)
    x2 = x + _rms(_dot(gl, wdown_ref[...]), gpost_ref[...])
    pex = _dot(pe.astype(BF16), wple_ref[...])
    gate = _sigmoid_of_twice(_dot(x2.astype(BF16), wgate_ref[...]))
    return x2 + pex * gate, a


def _ffn_body(x_ref, pe_ref, *refs, tm):
    w_refs, (o_ref, tail_ref, carry_s) = refs[:8], refs[8:]
    i = pl.program_id(1)

    @pl.when(i == 0)
    def _():
        carry_s[...] = jnp.zeros((CONV_W - 1, D_FF), F32)

    row = lax.broadcasted_iota(jnp.int32, (tm, D_FF), 0)
    c0 = carry_s[0:1, :]
    c1 = carry_s[1:2, :]

    def shifted(a):
        a1 = jnp.where(row == 0, c1, pltpu.roll(a, 1, 0))
        a2 = jnp.where(row == 0, c0, jnp.where(row == 1, c1, pltpu.roll(a, 2, 0)))
        return a1, a2

    out, a = _ffn_tile(x_ref[...], pe_ref[...], shifted, *w_refs)
    o_ref[...] = out
    tail = a[tm - (CONV_W - 1):tm, :]
    carry_s[...] = tail
    tail_ref[...] = tail


def _ffn_cached_body(x_ref, pe_ref, prev_ref, *refs, t, nseq):
    w_refs, (o_ref, tail_ref) = refs[:8], refs[8:]
    pos = lax.broadcasted_iota(jnp.int32, (nseq * t, D_FF), 0) & (t - 1)

    def history(j):
        return jnp.concatenate([jnp.broadcast_to(prev_ref[q, j:j + 1, :], (t, D_FF)) for q in range(nseq)],
                               axis=0)

    def shifted(a):
        p0, p1 = history(0), history(1)
        a1 = jnp.where(pos == 0, p1, pltpu.roll(a, 1, 0))
        a2 = jnp.where(pos == 0, p0, jnp.where(pos == 1, p1, pltpu.roll(a, 2, 0)))
        return a1, a2

    x = x_ref[...].reshape(nseq * t, D_MODEL)
    pe = pe_ref[...].reshape(nseq * t, PLE_DIM)
    out, a = _ffn_tile(x, pe, shifted, *w_refs)
    o_ref[...] = out.reshape(nseq, t, D_MODEL)
    for q in range(nseq):
        tail_ref[q] = a[(q + 1) * t - (CONV_W - 1):(q + 1) * t, :]


def _ffn_weight_specs():
    return [_const_spec((1, D_MODEL)), _const_spec((D_MODEL, 2 * D_FF)),
            _const_spec((CONV_W, D_FF)), _const_spec((1, D_FF)),
            _const_spec((D_FF, D_MODEL)), _const_spec((1, D_MODEL)),
            _const_spec((PLE_DIM, D_MODEL)), _const_spec((D_MODEL, D_MODEL))]


def _ffn(x, pe, weights, tm):
    bsz, t, _ = x.shape
    tail_spec = pl.BlockSpec((None, CONV_W - 1, D_FF), lambda b, i: (b, 0, 0))
    return pl.pallas_call(
        functools.partial(_ffn_body, tm=tm),
        grid=(bsz, t // tm),
        in_specs=[pl.BlockSpec((None, tm, D_MODEL), lambda b, i: (b, i, 0)),
                  pl.BlockSpec((None, tm, PLE_DIM), lambda b, i: (b, i, 0))] + _ffn_weight_specs(),
        out_specs=[pl.BlockSpec((None, tm, D_MODEL), lambda b, i: (b, i, 0)), tail_spec],
        out_shape=[jax.ShapeDtypeStruct((bsz, t, D_MODEL), F32),
                   jax.ShapeDtypeStruct((bsz, CONV_W - 1, D_FF), F32)],
        scratch_shapes=[pltpu.VMEM((CONV_W - 1, D_FF), F32)],
        compiler_params=pltpu.CompilerParams(
            dimension_semantics=("parallel", "arbitrary"), vmem_limit_bytes=_VMEM_LIMIT),
        name="convffn",
    )(x, pe, *weights)


def _ffn_cached(x, pe, prev, weights, nseq):
    bsz, t, _ = x.shape
    tail_spec = pl.BlockSpec((nseq, CONV_W - 1, D_FF), lambda b: (b, 0, 0))
    return pl.pallas_call(
        functools.partial(_ffn_cached_body, t=t, nseq=nseq),
        grid=(bsz // nseq,),
        in_specs=[pl.BlockSpec((nseq, t, D_MODEL), lambda b: (b, 0, 0)),
                  pl.BlockSpec((nseq, t, PLE_DIM), lambda b: (b, 0, 0)), tail_spec] + _ffn_weight_specs(),
        out_specs=[pl.BlockSpec((nseq, t, D_MODEL), lambda b: (b, 0, 0)), tail_spec],
        out_shape=[jax.ShapeDtypeStruct((bsz, t, D_MODEL), F32),
                   jax.ShapeDtypeStruct((bsz, CONV_W - 1, D_FF), F32)],
        compiler_params=pltpu.CompilerParams(
            dimension_semantics=("parallel",), vmem_limit_bytes=_VMEM_LIMIT),
        name="convffn_cached",
    )(x, pe, prev, *weights)


def _t5_bucket(rel):
    nb = NUM_BUCKETS // 2
    ret = jnp.where(rel > 0, nb, 0)
    n = jnp.abs(rel)
    max_exact = nb // 2
    large = max_exact + (jnp.log(jnp.maximum(n, max_exact).astype(jnp.float32) / max_exact)
                         / math.log(MAX_DISTANCE / max_exact) * (nb - max_exact)).astype(jnp.int32)
    large = jnp.minimum(large, nb - 1)
    return ret + jnp.where(n < max_exact, n, large)


def _bias_body(table_ref, sinks_ref, bk_ref, o_ref, *, nvar):
    bk = bk_ref[...]
    row = lax.broadcasted_iota(jnp.int32, bk.shape, 0)
    for k in range(A_KV_HEADS):
        acc = jnp.where(bk == -1, sinks_ref[k], NEG_INF)
        for b in range(NUM_BUCKETS):
            acc = jnp.where(bk == b, table_ref[k, b:b + 1, :], acc)
        for v in range(nvar):
            o_ref[v, k] = jnp.where(row < v * CHUNK, NEG_INF, acc)


def _bias_ext(table, sinks, lq, lk, nvar):
    rows = _score_rows(lk)
    q_pos = jnp.arange(lq) + WINDOW
    k_pos = jnp.arange(lk)
    buckets = _t5_bucket(k_pos[:, None] - q_pos[None, :]).astype(jnp.int32)
    bk = jnp.concatenate([buckets, jnp.full((1, lq), -1, jnp.int32),
                          jnp.full((rows - lk - 1, lq), -2, jnp.int32)], axis=0)
    bk = jnp.tile(bk, (1, A_GROUP))
    tab = jnp.repeat(table.astype(F32).reshape(NUM_BUCKETS, A_KV_HEADS, A_GROUP), lq, axis=2)
    tab = jnp.transpose(tab, (1, 0, 2))
    snk = jnp.repeat(sinks.astype(F32).reshape(A_KV_HEADS, 1, A_GROUP), lq, axis=2)
    vmem = pl.BlockSpec(memory_space=pltpu.VMEM)
    return pl.pallas_call(
        functools.partial(_bias_body, nvar=nvar),
        in_specs=[vmem, vmem, vmem],
        out_specs=vmem,
        out_shape=jax.ShapeDtypeStruct((nvar, A_KV_HEADS, rows, A_GROUP * lq), F32),
        name="relbias",
    )(tab, snk, bk)


def _ffn_weights(w):
    return [w[k] for k in ("g_pre_ffn", "w_up", "w_conv", "b_conv", "w_down", "g_post_ffn", "w_ple", "w_ple_gate")]


def _prompt_layer(x, pe, w, *, tm_mix, tm_tok, tm_ffn):
    bsz, t, _ = x.shape
    n = bsz * t
    x2d = x.reshape(n, D_MODEL)
    bias_ext = _bias_ext(w["rel_table"], w["sinks"], CHUNK, WINDOW + CHUNK, WINDOW // CHUNK + 1)
    kv, gg, ya, yb, s_fin = _mixer(x2d, w["g_pre_mix"], w["w_in"], w["lb_logits"], w["g_hgrn_out"],
                                   bias_ext, tm=tm_mix, seq=t)
    x1 = _merge(ya, yb, gg, x2d, w["w_br_a"], w["w_br_b"], w["w_out"], w["g_post_mix"], tm_tok)
    y, conv_tail = _ffn(x1.reshape(bsz, t, D_MODEL), pe, _ffn_weights(w), tm_ffn)
    return y, kv.reshape(bsz, t, 2 * A_KV_W), s_fin, conv_tail


def _sample_layer(x, pe, kv_prev, s_prev, conv_prev, w, *, tm_tok):
    bsz, t, _ = x.shape
    n = bsz * t
    x2d = x.reshape(n, D_MODEL)
    qa, kv, hb, gg = _inproj(x2d, w["g_pre_mix"], w["w_in"], tm_tok)
    bias_ext = _bias_ext(w["rel_table"], w["sinks"], t, WINDOW + t, 1)
    kv3 = kv.reshape(bsz, t, 2 * A_KV_W)
    ya = _attention_cached(qa.reshape(bsz, t, A_Q_W), kv3, kv_prev, bias_ext, lq=t)
    yb, s_fin = _hgrn_cached(hb.reshape(bsz, t, _HB_W), w["lb_logits"], w["g_hgrn_out"], s_prev, blk=t)
    x1 = _merge(ya.reshape(n, A_Q_W), yb.reshape(n, B_VAL_W), gg, x2d,
                w["w_br_a"], w["w_br_b"], w["w_out"], w["g_post_mix"], tm_tok)
    y, conv_tail = _ffn_cached(x1.reshape(bsz, t, D_MODEL), pe, conv_prev, _ffn_weights(w), tm_tok // t)
    return y, kv3, s_fin, conv_tail


def _scale_in_cols(w_in):
    h = B_KEY_W
    scale = jnp.concatenate([
        jnp.full((A_Q_W,), A_HEAD_DIM ** -0.5, F32), jnp.ones((2 * A_KV_W,), F32),
        jnp.full((2 * h,), 0.5, F32), jnp.ones((B_VAL_W,), F32), jnp.full((B_VAL_W,), 0.5, F32),
        jnp.full((2 * D_MODEL,), 0.5, F32)])
    return w_in * scale[None, :]


def kernel(x_prompt, x_sample, cache_win_k, cache_win_v, state_hgrn, cache_ffn_conv, p_prompt, p_sample,
           rel_bias_table, lb_logits, g_pre_mix, w_in, attn_sinks, g_hgrn_out, w_br_a, w_br_b, w_out,
           g_post_mix, g_pre_ffn, w_up, w_conv, b_conv, w_down, g_post_ffn, w_ple, w_ple_gate):
    bsz, seq, _ = x_prompt.shape
    dbsz, dseq, _ = x_sample.shape
    w = {
        "rel_table": rel_bias_table, "sinks": attn_sinks[0], "lb_logits": lb_logits.astype(F32),
        "g_pre_mix": g_pre_mix[0][None, :], "w_in": _scale_in_cols(w_in[0]).astype(BF16),
        "g_hgrn_out": g_hgrn_out[0][None, :],
        "w_br_a": w_br_a[0].astype(BF16), "w_br_b": w_br_b[0].astype(BF16), "w_out": w_out[0].astype(BF16),
        "g_post_mix": g_post_mix[0][None, :], "g_pre_ffn": g_pre_ffn[0][None, :],
        "w_up": w_up[0].astype(BF16), "w_conv": w_conv[0], "b_conv": b_conv[0][None, :],
        "w_down": w_down[0].astype(BF16), "g_post_ffn": g_post_ffn[0][None, :],
        "w_ple": w_ple[0].astype(BF16), "w_ple_gate": (0.5 * w_ple_gate[0]).astype(BF16),
    }
    yp, kvp, sp, cp = _prompt_layer(x_prompt, p_prompt[0], w, tm_mix=256, tm_tok=512, tm_ffn=256)
    wc = cache_win_k.shape[2]
    kv_cache = jnp.concatenate([cache_win_k[0].reshape(dbsz, wc, A_KV_W),
                                cache_win_v[0].reshape(dbsz, wc, A_KV_W)], axis=-1)
    ys, kvs, ss, cs = _sample_layer(x_sample, p_sample[0], kv_cache, state_hgrn[0], cache_ffn_conv[0], w,
                                    tm_tok=256)
    keep = min(WINDOW, seq)

    def heads(a):
        return a.reshape(a.shape[0], a.shape[1], A_KV_HEADS, A_HEAD_DIM)[None]

    return (yp, ys,
            heads(kvp[:, seq - keep:, 0:A_KV_W]), heads(kvp[:, seq - keep:, A_KV_W:]),
            sp[None], cp[None],
            heads(kvs[:, :, 0:A_KV_W]), heads(kvs[:, :, A_KV_W:]),
            ss[None], cs[None])
```

```python
import functools
import math

import jax
import jax.numpy as jnp
from jax import lax
from jax.experimental import pallas as pl
from jax.experimental.pallas import tpu as pltpu

D_MODEL = 1024
CHUNK = 64
A_HEADS = 16
A_KV_HEADS = 2
A_HEAD_DIM = 64
A_GROUP = A_HEADS // A_KV_HEADS
WINDOW = 128
A_Q_W = A_HEADS * A_HEAD_DIM
A_KV_W = A_KV_HEADS * A_HEAD_DIM
NUM_BUCKETS = 32
MAX_DISTANCE = 128
B_HEADS = 8
B_KEY_DIM = 128
B_VAL_DIM = D_MODEL // B_HEADS
B_KEY_W = B_HEADS * B_KEY_DIM
B_VAL_W = B_HEADS * B_VAL_DIM
D_FF = 2816
CONV_W = 3
PLE_DIM = 256
EPS = 1e-6
NEG_INF = -1e30

_QA0 = 0
_KV0 = A_Q_W
_HB0 = _KV0 + 2 * A_KV_W
_GG0 = _HB0 + 2 * B_KEY_W + 2 * B_VAL_W
IN_COLS = _GG0 + 2 * D_MODEL
_HB_W = _GG0 - _HB0

_BF16_ROWS = 16
_VMEM_LIMIT = 56 * 1024 * 1024

BF16 = jnp.bfloat16
F32 = jnp.float32


def _score_rows(lk):
    return lk + _BF16_ROWS


def _const_spec(shape):
    nd = len(shape)
    return pl.BlockSpec(shape, lambda *_: (0,) * nd, pipeline_mode=pl.Buffered(1))


def _rms(x, g):
    ms = jnp.mean(x * x, axis=-1, keepdims=True)
    return x * lax.rsqrt(ms + EPS) * g


def _sigmoid_of_twice(hx):
    return 0.5 * jnp.tanh(hx) + 0.5


def _dot(a, b):
    return jnp.dot(a, b, preferred_element_type=F32)


def _dot_nt(a, b):
    return lax.dot_general(a, b, (((1,), (1,)), ((), ())), preferred_element_type=F32)


def _dot_tn(a, b):
    return lax.dot_general(a, b, (((0,), (0,)), ((), ())), preferred_element_type=F32)


def _interleave(a, b):
    out, nb = [], 0
    for i, t in enumerate(a):
        out.append(t)
        want = ((i + 1) * len(b)) // len(a)
        out.extend(b[nb:want])
        nb = want
    return out + b[nb:]


def _attn_thunks(nchunk, lq, lk, load_q, load_kw, bias_strip, store_o):
    def scores(c):
        qc = load_q(c)
        kw = load_kw(c)
        st = []
        for k in range(A_KV_HEADS):
            qs = jnp.concatenate(
                [qc[:, (k * A_GROUP + g) * A_HEAD_DIM:(k * A_GROUP + g + 1) * A_HEAD_DIM]
                 for g in range(A_GROUP)], axis=0)
            st.append(_dot_nt(kw[:, k * A_HEAD_DIM:(k + 1) * A_HEAD_DIM], qs))
        return dict(kw=kw, st=st)

    def softmax(c, s):
        ot, rden = [], []
        for k in range(A_KV_HEADS):
            ps, rs = [], []
            for j in range(0, A_GROUP * lq, 128):
                t = s["st"][k][:, j:j + 128] + bias_strip(c, k, j)
                m = jnp.max(t, axis=0, keepdims=True)
                p = jnp.exp(t - m)
                rs.append(1.0 / jnp.sum(p, axis=0, keepdims=True))
                ps.append(p.astype(BF16))
            rden.append(jnp.concatenate(rs, axis=1))
            vv = s["kw"][:, A_KV_W + k * A_HEAD_DIM:A_KV_W + (k + 1) * A_HEAD_DIM]
            ot.append(_dot_tn(vv, jnp.concatenate(ps, axis=1)))
        return dict(ot=ot, rden=rden)

    def out(c, s):
        outs = []
        for k in range(A_KV_HEADS):
            o = (s["ot"][k] * s["rden"][k]).T
            outs.append(jnp.concatenate([o[g * lq:(g + 1) * lq, :] for g in range(A_GROUP)], axis=1))
        store_o(c, jnp.concatenate(outs, axis=1).astype(BF16))

    ahead = 2
    sc, sm, th = {}, {}, []

    def do_scores(c):
        sc[c] = scores(c)

    def do_softmax(c):
        sm[c] = softmax(c, sc.pop(c))

    def do_out(c):
        out(c, sm.pop(c))

    for c in range(min(ahead, nchunk)):
        th.append(functools.partial(do_scores, c))
    for c in range(nchunk):
        if c + ahead < nchunk:
            th.append(functools.partial(do_scores, c + ahead))
        th.append(functools.partial(do_softmax, c))
        if c >= 1:
            th.append(functools.partial(do_out, c - 1))
    th.append(functools.partial(do_out, nchunk - 1))
    return th


def _attn_body(q_ref, kvc_ref, kvp_ref, bias_ref, o_ref, kv_s, *, lq, lk, nchunk, tq, nseq):
    zpad = jnp.zeros((_score_rows(lk) - lk, 2 * A_KV_W), BF16)
    streams = []
    for n in range(nseq):
        kv_s[n, 0:WINDOW, :] = kvp_ref[n].astype(BF16)
        kv_s[n, WINDOW:WINDOW + tq, :] = kvc_ref[n].astype(BF16)

        def store_o(c, o, n=n):
            o_ref[n, c * lq:(c + 1) * lq, :] = o

        streams.append(_attn_thunks(
            nchunk, lq, lk,
            load_q=lambda c, n=n: q_ref[n, c * lq:(c + 1) * lq, :],
            load_kw=lambda c, n=n: jnp.concatenate([kv_s[n, c * lq:c * lq + lk, :], zpad], axis=0),
            bias_strip=lambda c, k, j: bias_ref[0, k, :, j:j + 128],
            store_o=store_o))
    for group in zip(*streams):
        for t in group:
            t()


def _attention_cached(qa, kv, kv_prev, bias_ext, *, lq, nseq):
    bsz, t, _ = qa.shape
    lk = WINDOW + lq
    body = functools.partial(_attn_body, lq=lq, lk=lk, nchunk=t // lq, tq=t, nseq=nseq)
    return pl.pallas_call(
        body,
        grid=(bsz // nseq,),
        in_specs=[
            pl.BlockSpec((nseq, t, A_Q_W), lambda b: (b, 0, 0)),
            pl.BlockSpec((nseq, t, 2 * A_KV_W), lambda b: (b, 0, 0)),
            pl.BlockSpec((nseq, WINDOW, 2 * A_KV_W), lambda b: (b, 0, 0)),
            _const_spec(bias_ext.shape),
        ],
        out_specs=pl.BlockSpec((nseq, t, A_Q_W), lambda b: (b, 0, 0)),
        out_shape=jax.ShapeDtypeStruct((bsz, t, A_Q_W), BF16),
        scratch_shapes=[pltpu.VMEM((nseq, WINDOW + t, 2 * A_KV_W), BF16)],
        compiler_params=pltpu.CompilerParams(
            dimension_semantics=("parallel",), vmem_limit_bytes=_VMEM_LIMIT),
        name="attention",
    )(qa, kv, kv_prev, bias_ext)


def _cumsum_rows(x, tril3):
    hi = x.astype(BF16)
    r = x - hi.astype(F32)
    mid = r.astype(BF16)
    lo = (r - mid.astype(F32)).astype(BF16)
    return _dot(tril3, jnp.concatenate([hi, mid, lo], axis=0))


def _hgrn_thunks(nchunk, blk, load, store_y, st_s, lbl, g):
    e = jnp.exp(lbl - jnp.max(lbl, axis=0, keepdims=True))
    lb = e[0:1, :] / jnp.sum(e, axis=0, keepdims=True)
    fa = 0.5 * (1.0 + lb)
    fb = 0.5 * (1.0 - lb)
    ri = lax.broadcasted_iota(jnp.int32, (blk, blk), 0)
    ci = lax.broadcasted_iota(jnp.int32, (blk, blk), 1)
    tril = (ri >= ci).astype(BF16)
    tril3 = jnp.concatenate([tril, tril, tril], axis=1)
    ri2 = lax.broadcasted_iota(jnp.int32, (blk, 2 * blk), 0)
    ci2 = lax.broadcasted_iota(jnp.int32, (blk, 2 * blk), 1)
    causal2 = ri2 >= (ci2 & (blk - 1))
    mid = blk // 2
    w = B_KEY_W
    pw = 2 * B_KEY_DIM
    npair = B_HEADS // 2
    hs = [slice(h * B_KEY_DIM, (h + 1) * B_KEY_DIM) for h in range(B_HEADS)]
    ps = [slice(j * pw, (j + 1) * pw) for j in range(npair)]

    def blockdiag(x0, x1):
        z = jnp.zeros_like(x0)
        return jnp.concatenate([jnp.concatenate([x0, z], axis=1), jnp.concatenate([z, x1], axis=1)], axis=0)

    def stage_decay(c):
        bt = fb * jnp.tanh(load(c, w, 2 * w))
        f = fa + bt
        cum = _cumsum_rows(jnp.log2(f), tril3)
        return dict(kk=fb - bt, cum=cum)

    def stage_state(c, s):
        cum = s["cum"]
        hq = load(c, 0, w)
        qs = hq + hq * jnp.tanh(hq)
        b_last = cum[blk - 1:blk, :]
        b_mid = cum[mid:mid + 1, :]
        q2f = qs * jnp.exp2(cum - b_mid)
        k2f = s["kk"] * jnp.exp2(b_mid - cum)
        q1 = (q2f * jnp.exp2(b_mid)).astype(BF16)
        k3 = (k2f * jnp.exp2(b_last - b_mid)).astype(BF16)
        q2 = q2f.astype(BF16)
        k2 = k2f.astype(BF16)
        vb = load(c, 2 * w, 3 * w).astype(BF16)
        dec = jnp.exp2(b_last)
        a = [_dot_nt(q2[:, ps[j]], blockdiag(k2[:, hs[2 * j]], k2[:, hs[2 * j + 1]])) for j in range(npair)]
        st = [st_s[h] for h in range(B_HEADS)]
        o1 = [_dot_nt(q1[:, ps[j]], blockdiag(st[2 * j].astype(BF16), st[2 * j + 1].astype(BF16)))
              for j in range(npair)]
        upd = []
        for q in range(0, B_HEADS, 4):
            vstack = jnp.concatenate([vb[:, hs[h]] for h in range(q, q + 4)], axis=0)
            kdiag = jnp.concatenate(
                [jnp.concatenate([k3[:, hs[h]] if h == h2 else jnp.zeros((blk, B_KEY_DIM), BF16)
                                  for h2 in range(q, q + 4)], axis=1) for h in range(q, q + 4)], axis=0)
            upd.append(_dot_tn(vstack, kdiag))
        for h, sl in enumerate(hs):
            st_s[h] = dec[:, sl] * st[h] + upd[h // 4][:, (h % 4) * B_KEY_DIM:(h % 4 + 1) * B_KEY_DIM]
        return dict(a=a, o1=o1, vb=vb)

    def stage_out(c, s):
        vb = s["vb"]
        am = [jnp.where(causal2, s["a"][j], 0.0).astype(BF16) for j in range(npair)]
        o2 = [_dot(am[j], blockdiag(vb[:, hs[2 * j]], vb[:, hs[2 * j + 1]])) for j in range(npair)]
        o = [s["o1"][j] + o2[j] for j in range(npair)]
        ys = [_rms(o[h // 2][:, (h % 2) * B_VAL_DIM:(h % 2 + 1) * B_VAL_DIM], g) for h in range(B_HEADS)]
        hog = load(c, 3 * w, 4 * w)
        y = jnp.concatenate(ys, axis=1) * (hog + hog * jnp.tanh(hog))
        store_y(c, y.astype(BF16))

    dec, sta, th = {}, {}, []

    def do_decay(c):
        dec[c] = stage_decay(c)

    def do_state(c):
        sta[c] = stage_state(c, dec.pop(c))

    def do_out(c):
        stage_out(c, sta.pop(c))

    th.append(functools.partial(do_decay, 0))
    for c in range(nchunk):
        if c + 1 < nchunk:
            th.append(functools.partial(do_decay, c + 1))
        th.append(functools.partial(do_state, c))
        if c >= 1:
            th.append(functools.partial(do_out, c - 1))
    th.append(functools.partial(do_out, nchunk - 1))
    return th


def _hgrn_body(hb_ref, lbl_ref, g_ref, s0_ref, yb_ref, sfin_ref, st_s, *, blk, nchunk, nseq):
    streams = []
    for q in range(nseq):
        for h in range(B_HEADS):
            st_s[q, h] = s0_ref[q, h].T

        def store_y(c, y, q=q):
            yb_ref[q, c * blk:(c + 1) * blk, :] = y

        streams.append(_hgrn_thunks(
            nchunk, blk, lambda c, lo, hi, q=q: hb_ref[q, c * blk:(c + 1) * blk, lo:hi],
            store_y, st_s.at[q], lbl_ref[...], g_ref[...]))
    for group in zip(*streams):
        for t in group:
            t()
    for q in range(nseq):
        for h in range(B_HEADS):
            sfin_ref[q, h] = st_s[q, h].T


def _hgrn_cached(hb, lb_logits, g_out, s0, *, blk, nseq):
    bsz, t, _ = hb.shape
    body = functools.partial(_hgrn_body, blk=blk, nchunk=t // blk, nseq=nseq)
    st_spec = pl.BlockSpec((nseq, B_HEADS, B_KEY_DIM, B_VAL_DIM), lambda b: (b, 0, 0, 0))
    return pl.pallas_call(
        body,
        grid=(bsz // nseq,),
        in_specs=[pl.BlockSpec((nseq, t, _HB_W), lambda b: (b, 0, 0)),
                  _const_spec(lb_logits.shape), _const_spec((1, B_VAL_DIM)), st_spec],
        out_specs=[pl.BlockSpec((nseq, t, B_VAL_W), lambda b: (b, 0, 0)), st_spec],
        out_shape=[jax.ShapeDtypeStruct((bsz, t, B_VAL_W), BF16),
                   jax.ShapeDtypeStruct((bsz, B_HEADS, B_KEY_DIM, B_VAL_DIM), F32)],
        scratch_shapes=[pltpu.VMEM((nseq, B_HEADS, B_VAL_DIM, B_KEY_DIM), F32)],
        compiler_params=pltpu.CompilerParams(
            dimension_semantics=("parallel",), vmem_limit_bytes=_VMEM_LIMIT),
        name="hgrn2",
    )(hb, lb_logits, g_out, s0)


def _inproj_body(x_ref, g_ref, w_ref, qa_ref, kv_ref, hb_ref, gg_ref):
    h = _rms(x_ref[...], g_ref[...]).astype(BF16)
    step = 512

    def mm(lo, width):
        return _dot(h, w_ref[:, lo:lo + width])

    for j in range(0, A_Q_W, step):
        qa_ref[:, j:j + step] = mm(_QA0 + j, step).astype(BF16)
    kv_ref[...] = mm(_KV0, 2 * A_KV_W)
    for j in range(0, _HB_W, step):
        hb_ref[:, j:j + step] = mm(_HB0 + j, step)
    for j in range(0, 2 * D_MODEL, step):
        gg_ref[:, j:j + step] = mm(_GG0 + j, step)


def _inproj(x2d, g, w_bf, tm):
    n = x2d.shape[0]
    row = lambda w: pl.BlockSpec((tm, w), lambda i: (i, 0))
    return pl.pallas_call(
        _inproj_body,
        grid=(n // tm,),
        in_specs=[row(D_MODEL), _const_spec((1, D_MODEL)), _const_spec((D_MODEL, IN_COLS))],
        out_specs=[row(A_Q_W), row(2 * A_KV_W), row(_HB_W), row(2 * D_MODEL)],
        out_shape=[
            jax.ShapeDtypeStruct((n, A_Q_W), BF16),
            jax.ShapeDtypeStruct((n, 2 * A_KV_W), F32),
            jax.ShapeDtypeStruct((n, _HB_W), F32),
            jax.ShapeDtypeStruct((n, 2 * D_MODEL), F32),
        ],
        compiler_params=pltpu.CompilerParams(
            dimension_semantics=("parallel",), vmem_limit_bytes=_VMEM_LIMIT),
        name="inproj",
    )(x2d, g, w_bf)


def _mixer_body(x_ref, gpre_ref, w_ref, lbl_ref, ghg_ref, bias_ref,
                kv_ref, gg_ref, ya_ref, yb_ref, sfin_ref,
                h_s, qa_s, kvb_s, kvw_s, hb_s, st_s, *, tm, tiles_per_seq):
    s = pl.program_id(0)
    cur = s % 2
    prv = 1 - cur
    tib = (s + tiles_per_seq - 1) % tiles_per_seq
    nchunk = tm // CHUNK
    lk = WINDOW + CHUNK
    piece = 512

    @pl.when(s == 0)
    def _():
        qa_s[...] = jnp.zeros(qa_s.shape, BF16)
        kvb_s[...] = jnp.zeros(kvb_s.shape, BF16)
        kvw_s[...] = jnp.zeros(kvw_s.shape, BF16)
        hb_s[...] = jnp.zeros(hb_s.shape, F32)

    @pl.when((s == 0) | (tib == 0))
    def _():
        st_s[...] = jnp.zeros(st_s.shape, F32)

    kvw_s[0:WINDOW, :] = kvw_s[tm:tm + WINDOW, :]
    kvw_s[WINDOW:WINDOW + tm, :] = kvb_s[prv]

    def dense_piece(lo, width):
        z = _dot(h_s[...], w_ref[:, lo:lo + width])
        if lo < _KV0:
            qa_s[cur, :, lo:lo + width] = z.astype(BF16)
        elif lo < _HB0:
            kv_ref[...] = z
            kvb_s[cur] = z.astype(BF16)
        elif lo < _GG0:
            hb_s[cur, :, lo - _HB0:lo - _HB0 + width] = z
        else:
            gg_ref[:, lo - _GG0:lo - _GG0 + width] = z

    def pieces(lo, hi):
        return [functools.partial(dense_piece, c, min(piece, hi - c)) for c in range(lo, hi, piece)]

    dense_hb = pieces(_HB0, _GG0)
    dense_rest = pieces(_QA0, _KV0) + pieces(_KV0, _HB0) + pieces(_GG0, IN_COLS)

    def store_ya(c, o):
        ya_ref[c * CHUNK:(c + 1) * CHUNK, :] = o

    def store_yb(c, y):
        yb_ref[c * CHUNK:(c + 1) * CHUNK, :] = y

    zpad = jnp.zeros((_score_rows(lk) - lk, 2 * A_KV_W), BF16)

    def bias_strip(c, k, j):
        var = jnp.clip(WINDOW // CHUNK - (tib * nchunk + c), 0, WINDOW // CHUNK)
        return bias_ref[var, k, :, j:j + 128]

    attn = _attn_thunks(
        nchunk, CHUNK, lk,
        load_q=lambda c: qa_s[prv, c * CHUNK:(c + 1) * CHUNK, :],
        load_kw=lambda c: jnp.concatenate([kvw_s[c * CHUNK:c * CHUNK + lk, :], zpad], axis=0),
        bias_strip=bias_strip, store_o=store_ya)
    hgrn = _hgrn_thunks(
        nchunk, CHUNK, lambda c, lo, hi: hb_s[prv, c * CHUNK:(c + 1) * CHUNK, lo:hi],
        store_yb, st_s, lbl_ref[...], ghg_ref[...])
    for t in attn[:2]:
        t()
    h_s[...] = _rms(x_ref[...], gpre_ref[...]).astype(BF16)
    for t in _interleave(dense_hb, attn[2:]) + _interleave(dense_rest, hgrn):
        t()

    @pl.when((tib == tiles_per_seq - 1) & (s > 0))
    def _():
        for h in range(B_HEADS):
            sfin_ref[h] = st_s[h].T


def _mixer(x2d, g_pre, w_bf, lb_logits, g_hgrn, bias_ext, *, tm, seq):
    n = x2d.shape[0]
    nt = n // tm
    tiles_per_seq = seq // tm
    dense_row = lambda w: pl.BlockSpec((tm, w), lambda s: (jnp.minimum(s, nt - 1), 0))
    lag_row = lambda w: pl.BlockSpec((tm, w), lambda s: (jnp.maximum(s - 1, 0), 0))
    body = functools.partial(_mixer_body, tm=tm, tiles_per_seq=tiles_per_seq)
    return pl.pallas_call(
        body,
        grid=(nt + 1,),
        in_specs=[dense_row(D_MODEL), _const_spec((1, D_MODEL)), _const_spec((D_MODEL, IN_COLS)),
                  _const_spec(lb_logits.shape), _const_spec((1, B_VAL_DIM)), _const_spec(bias_ext.shape)],
        out_specs=[dense_row(2 * A_KV_W), dense_row(2 * D_MODEL), lag_row(A_Q_W), lag_row(B_VAL_W),
                   pl.BlockSpec((None, B_HEADS, B_KEY_DIM, B_VAL_DIM),
                                lambda s: (jnp.maximum(s - 1, 0) // tiles_per_seq, 0, 0, 0))],
        out_shape=[jax.ShapeDtypeStruct((n, 2 * A_KV_W), F32),
                   jax.ShapeDtypeStruct((n, 2 * D_MODEL), F32),
                   jax.ShapeDtypeStruct((n, A_Q_W), BF16),
                   jax.ShapeDtypeStruct((n, B_VAL_W), BF16),
                   jax.ShapeDtypeStruct((n // seq, B_HEADS, B_KEY_DIM, B_VAL_DIM), F32)],
        scratch_shapes=[pltpu.VMEM((tm, D_MODEL), BF16),
                        pltpu.VMEM((2, tm, A_Q_W), BF16),
                        pltpu.VMEM((2, tm, 2 * A_KV_W), BF16),
                        pltpu.VMEM((WINDOW + tm, 2 * A_KV_W), BF16),
                        pltpu.VMEM((2, tm, _HB_W), F32),
                        pltpu.VMEM((B_HEADS, B_VAL_DIM, B_KEY_DIM), F32)],
        compiler_params=pltpu.CompilerParams(
            dimension_semantics=("arbitrary",), vmem_limit_bytes=_VMEM_LIMIT),
        name="mixer",
    )(x2d, g_pre, w_bf, lb_logits, g_hgrn, bias_ext)


def _merge_body(ya_ref, yb_ref, gg_ref, x_ref, wa_ref, wb_ref, wo_ref, g_ref, o_ref, *, tm, strip):
    def branches(r):
        rows = slice(r * strip, (r + 1) * strip)
        ga = gg_ref[rows, 0:D_MODEL]
        gb = gg_ref[rows, D_MODEL:2 * D_MODEL]
        mix = (_sigmoid_of_twice(ga) * _dot(ya_ref[rows, :], wa_ref[...])
               + _sigmoid_of_twice(gb) * _dot(yb_ref[rows, :], wb_ref[...]))
        return mix.astype(BF16)

    def project(r, mix):
        rows = slice(r * strip, (r + 1) * strip)
        o_ref[rows, :] = x_ref[rows, :] + _rms(_dot(mix, wo_ref[...]), g_ref[...])

    nstrip = tm // strip
    mix = branches(0)
    for r in range(nstrip):
        nxt = branches(r + 1) if r + 1 < nstrip else None
        project(r, mix)
        mix = nxt


def _merge(ya, yb, gg, x2d, wa, wb, wo, g, tm):
    n = x2d.shape[0]
    row = lambda w: pl.BlockSpec((tm, w), lambda i: (i, 0))
    wspec = _const_spec((D_MODEL, D_MODEL))
    return pl.pallas_call(
        functools.partial(_merge_body, tm=tm, strip=min(tm, 256)),
        grid=(n // tm,),
        in_specs=[row(A_Q_W), row(B_VAL_W), row(2 * D_MODEL), row(D_MODEL),
                  wspec, wspec, wspec, _const_spec((1, D_MODEL))],
        out_specs=row(D_MODEL),
        out_shape=jax.ShapeDtypeStruct((n, D_MODEL), F32),
        compiler_params=pltpu.CompilerParams(
            dimension_semantics=("parallel",), vmem_limit_bytes=_VMEM_LIMIT),
        name="merge",
    )(ya, yb, gg, x2d, wa, wb, wo, g)


def _gelu_tanh(x):
    c = math.sqrt(2.0 / math.pi)
    return 0.5 * x * (1.0 + jnp.tanh(c * (x + 0.044715 * (x * x * x))))


def _ffn_up(x, gpre_ref, wup_ref):
    hf = _rms(x, gpre_ref[...]).astype(BF16)
    return _dot(hf, wup_ref[:, 0:D_FF]), _dot(hf, wup_ref[:, D_FF:2 * D_FF])


def _ffn_down(x, pe, a, u, a1, a2, wconv_ref, bconv_ref, wdown_ref, gpost_ref, wple_ref, wgate_ref):
    ac = bconv_ref[...] + a2 * wconv_ref[0:1, :] + a1 * wconv_ref[1:2, :] + a * wconv_ref[2:3, :]
    gl = (_gelu_tanh(ac) * u).astype(BF16)
    x2 = x + _rms(_dot(gl, wdown_ref[...]), gpost_ref[...])
    pex = _dot(pe.astype(BF16), wple_ref[...])
    gate = _sigmoid_of_twice(_dot(x2.astype(BF16), wgate_ref[...]))
    return x2 + pex * gate


def _ffn_tile(x, pe, shifted, gpre_ref, wup_ref, *rest):
    a, u = _ffn_up(x, gpre_ref, wup_ref)
    a1, a2 = shifted(a)
    return _ffn_down(x, pe, a, u, a1, a2, *rest), a


def _ffn_body(x_ref, pe_ref, *refs, tm, strip):
    (gpre_ref, wup_ref), rest, (o_ref, tail_ref, carry_s) = refs[:2], refs[2:8], refs[8:]
    i = pl.program_id(1)

    @pl.when(i == 0)
    def _():
        carry_s[...] = jnp.zeros((CONV_W - 1, D_FF), F32)

    row = lax.broadcasted_iota(jnp.int32, (strip, D_FF), 0)
    nstrip = tm // strip
    rows = [slice(r * strip, (r + 1) * strip) for r in range(nstrip)]
    hist = carry_s[...]
    au = _ffn_up(x_ref[rows[0], :], gpre_ref, wup_ref)
    for r in range(nstrip):
        nxt = _ffn_up(x_ref[rows[r + 1], :], gpre_ref, wup_ref) if r + 1 < nstrip else None
        a, u = au
        c0, c1 = hist[0:1, :], hist[1:2, :]
        a1 = jnp.where(row == 0, c1, pltpu.roll(a, 1, 0))
        a2 = jnp.where(row == 0, c0, jnp.where(row == 1, c1, pltpu.roll(a, 2, 0)))
        o_ref[rows[r], :] = _ffn_down(x_ref[rows[r], :], pe_ref[rows[r], :], a, u, a1, a2, *rest)
        hist = a[strip - (CONV_W - 1):strip, :]
        au = nxt
    carry_s[...] = hist
    tail_ref[...] = hist


def _ffn_cached_body(x_ref, pe_ref, prev_ref, *refs, t, nseq):
    w_refs, (o_ref, tail_ref) = refs[:8], refs[8:]
    pos = lax.broadcasted_iota(jnp.int32, (nseq * t, D_FF), 0) & (t - 1)

    def history(j):
        return jnp.concatenate([jnp.broadcast_to(prev_ref[q, j:j + 1, :], (t, D_FF)) for q in range(nseq)],
                               axis=0)

    def shifted(a):
        p0, p1 = history(0), history(1)
        a1 = jnp.where(pos == 0, p1, pltpu.roll(a, 1, 0))
        a2 = jnp.where(pos == 0, p0, jnp.where(pos == 1, p1, pltpu.roll(a, 2, 0)))
        return a1, a2

    x = x_ref[...].reshape(nseq * t, D_MODEL)
    pe = pe_ref[...].reshape(nseq * t, PLE_DIM)
    out, a = _ffn_tile(x, pe, shifted, *w_refs)
    o_ref[...] = out.reshape(nseq, t, D_MODEL)
    for q in range(nseq):
        tail_ref[q] = a[(q + 1) * t - (CONV_W - 1):(q + 1) * t, :]


def _ffn_weight_specs():
    return [_const_spec((1, D_MODEL)), _const_spec((D_MODEL, 2 * D_FF)),
            _const_spec((CONV_W, D_FF)), _const_spec((1, D_FF)),
            _const_spec((D_FF, D_MODEL)), _const_spec((1, D_MODEL)),
            _const_spec((PLE_DIM, D_MODEL)), _const_spec((D_MODEL, D_MODEL))]


def _ffn(x, pe, weights, tm):
    bsz, t, _ = x.shape
    tail_spec = pl.BlockSpec((None, CONV_W - 1, D_FF), lambda b, i: (b, 0, 0))
    return pl.pallas_call(
        functools.partial(_ffn_body, tm=tm, strip=min(tm, 256)),
        grid=(bsz, t // tm),
        in_specs=[pl.BlockSpec((None, tm, D_MODEL), lambda b, i: (b, i, 0)),
                  pl.BlockSpec((None, tm, PLE_DIM), lambda b, i: (b, i, 0))] + _ffn_weight_specs(),
        out_specs=[pl.BlockSpec((None, tm, D_MODEL), lambda b, i: (b, i, 0)), tail_spec],
        out_shape=[jax.ShapeDtypeStruct((bsz, t, D_MODEL), F32),
                   jax.ShapeDtypeStruct((bsz, CONV_W - 1, D_FF), F32)],
        scratch_shapes=[pltpu.VMEM((CONV_W - 1, D_FF), F32)],
        compiler_params=pltpu.CompilerParams(
            dimension_semantics=("parallel", "arbitrary"), vmem_limit_bytes=_VMEM_LIMIT),
        name="convffn",
    )(x, pe, *weights)


def _ffn_cached(x, pe, prev, weights, nseq):
    bsz, t, _ = x.shape
    tail_spec = pl.BlockSpec((nseq, CONV_W - 1, D_FF), lambda b: (b, 0, 0))
    return pl.pallas_call(
        functools.partial(_ffn_cached_body, t=t, nseq=nseq),
        grid=(bsz // nseq,),
        in_specs=[pl.BlockSpec((nseq, t, D_MODEL), lambda b: (b, 0, 0)),
                  pl.BlockSpec((nseq, t, PLE_DIM), lambda b: (b, 0, 0)), tail_spec] + _ffn_weight_specs(),
        out_specs=[pl.BlockSpec((nseq, t, D_MODEL), lambda b: (b, 0, 0)), tail_spec],
        out_shape=[jax.ShapeDtypeStruct((bsz, t, D_MODEL), F32),
                   jax.ShapeDtypeStruct((bsz, CONV_W - 1, D_FF), F32)],
        compiler_params=pltpu.CompilerParams(
            dimension_semantics=("parallel",), vmem_limit_bytes=_VMEM_LIMIT),
        name="convffn_cached",
    )(x, pe, prev, *weights)


def _t5_bucket(rel):
    nb = NUM_BUCKETS // 2
    ret = jnp.where(rel > 0, nb, 0)
    n = jnp.abs(rel)
    max_exact = nb // 2
    large = max_exact + (jnp.log(jnp.maximum(n, max_exact).astype(jnp.float32) / max_exact)
                         / math.log(MAX_DISTANCE / max_exact) * (nb - max_exact)).astype(jnp.int32)
    large = jnp.minimum(large, nb - 1)
    return ret + jnp.where(n < max_exact, n, large)


def _bias_body(table_ref, sinks_ref, bk_ref, o_ref, *, nvar):
    bk = bk_ref[...]
    row = lax.broadcasted_iota(jnp.int32, bk.shape, 0)
    for k in range(A_KV_HEADS):
        acc = jnp.where(bk == -1, sinks_ref[k], NEG_INF)
        for b in range(NUM_BUCKETS):
            acc = jnp.where(bk == b, table_ref[k, b:b + 1, :], acc)
        for v in range(nvar):
            o_ref[v, k] = jnp.where(row < v * CHUNK, NEG_INF, acc)


def _bias_ext(table, sinks, lq, lk, nvar):
    rows = _score_rows(lk)
    q_pos = jnp.arange(lq) + WINDOW
    k_pos = jnp.arange(lk)
    buckets = _t5_bucket(k_pos[:, None] - q_pos[None, :]).astype(jnp.int32)
    bk = jnp.concatenate([buckets, jnp.full((1, lq), -1, jnp.int32),
                          jnp.full((rows - lk - 1, lq), -2, jnp.int32)], axis=0)
    bk = jnp.tile(bk, (1, A_GROUP))
    tab = jnp.repeat(table.astype(F32).reshape(NUM_BUCKETS, A_KV_HEADS, A_GROUP), lq, axis=2)
    tab = jnp.transpose(tab, (1, 0, 2))
    snk = jnp.repeat(sinks.astype(F32).reshape(A_KV_HEADS, 1, A_GROUP), lq, axis=2)
    vmem = pl.BlockSpec(memory_space=pltpu.VMEM)
    return pl.pallas_call(
        functools.partial(_bias_body, nvar=nvar),
        in_specs=[vmem, vmem, vmem],
        out_specs=vmem,
        out_shape=jax.ShapeDtypeStruct((nvar, A_KV_HEADS, rows, A_GROUP * lq), F32),
        name="relbias",
    )(tab, snk, bk)


def _ffn_weights(w):
    return [w[k] for k in ("g_pre_ffn", "w_up", "w_conv", "b_conv", "w_down", "g_post_ffn", "w_ple", "w_ple_gate")]


def _prompt_layer(x, pe, w, *, tm_mix, tm_tok, tm_ffn):
    bsz, t, _ = x.shape
    n = bsz * t
    x2d = x.reshape(n, D_MODEL)
    bias_ext = _bias_ext(w["rel_table"], w["sinks"], CHUNK, WINDOW + CHUNK, WINDOW // CHUNK + 1)
    kv, gg, ya, yb, s_fin = _mixer(x2d, w["g_pre_mix"], w["w_in"], w["lb_logits"], w["g_hgrn_out"],
                                   bias_ext, tm=tm_mix, seq=t)
    x1 = _merge(ya, yb, gg, x2d, w["w_br_a"], w["w_br_b"], w["w_out"], w["g_post_mix"], tm_tok)
    y, conv_tail = _ffn(x1.reshape(bsz, t, D_MODEL), pe, _ffn_weights(w), tm_ffn)
    return y, kv.reshape(bsz, t, 2 * A_KV_W), s_fin, conv_tail


def _sample_layer(x, pe, kv_prev, s_prev, conv_prev, w, *, tm_tok, nseq_mix):
    bsz, t, _ = x.shape
    n = bsz * t
    x2d = x.reshape(n, D_MODEL)
    qa, kv, hb, gg = _inproj(x2d, w["g_pre_mix"], w["w_in"], tm_tok)
    bias_ext = _bias_ext(w["rel_table"], w["sinks"], t, WINDOW + t, 1)
    kv3 = kv.reshape(bsz, t, 2 * A_KV_W)
    ya = _attention_cached(qa.reshape(bsz, t, A_Q_W), kv3, kv_prev, bias_ext, lq=t, nseq=nseq_mix)
    yb, s_fin = _hgrn_cached(hb.reshape(bsz, t, _HB_W), w["lb_logits"], w["g_hgrn_out"], s_prev,
                             blk=t, nseq=nseq_mix)
    x1 = _merge(ya.reshape(n, A_Q_W), yb.reshape(n, B_VAL_W), gg, x2d,
                w["w_br_a"], w["w_br_b"], w["w_out"], w["g_post_mix"], tm_tok)
    y, conv_tail = _ffn_cached(x1.reshape(bsz, t, D_MODEL), pe, conv_prev, _ffn_weights(w), tm_tok // t)
    return y, kv3, s_fin, conv_tail


def _scale_in_cols(w_in):
    h = B_KEY_W
    scale = jnp.concatenate([
        jnp.full((A_Q_W,), A_HEAD_DIM ** -0.5, F32), jnp.ones((2 * A_KV_W,), F32),
        jnp.full((2 * h,), 0.5, F32), jnp.ones((B_VAL_W,), F32), jnp.full((B_VAL_W,), 0.5, F32),
        jnp.full((2 * D_MODEL,), 0.5, F32)])
    return w_in * scale[None, :]


def kernel(x_prompt, x_sample, cache_win_k, cache_win_v, state_hgrn, cache_ffn_conv, p_prompt, p_sample,
           rel_bias_table, lb_logits, g_pre_mix, w_in, attn_sinks, g_hgrn_out, w_br_a, w_br_b, w_out,
           g_post_mix, g_pre_ffn, w_up, w_conv, b_conv, w_down, g_post_ffn, w_ple, w_ple_gate):
    bsz, seq, _ = x_prompt.shape
    dbsz, dseq, _ = x_sample.shape
    w = {
        "rel_table": rel_bias_table, "sinks": attn_sinks[0], "lb_logits": lb_logits.astype(F32),
        "g_pre_mix": g_pre_mix[0][None, :], "w_in": _scale_in_cols(w_in[0]).astype(BF16),
        "g_hgrn_out": g_hgrn_out[0][None, :],
        "w_br_a": w_br_a[0].astype(BF16), "w_br_b": w_br_b[0].astype(BF16), "w_out": w_out[0].astype(BF16),
        "g_post_mix": g_post_mix[0][None, :], "g_pre_ffn": g_pre_ffn[0][None, :],
        "w_up": w_up[0].astype(BF16), "w_conv": w_conv[0], "b_conv": b_conv[0][None, :],
        "w_down": w_down[0].astype(BF16), "g_post_ffn": g_post_ffn[0][None, :],
        "w_ple": w_ple[0].astype(BF16), "w_ple_gate": (0.5 * w_ple_gate[0]).astype(BF16),
    }
    yp, kvp, sp, cp = _prompt_layer(x_prompt, p_prompt[0], w, tm_mix=256, tm_tok=512, tm_ffn=512)
    wc = cache_win_k.shape[2]
    kv_cache = jnp.concatenate([cache_win_k[0].reshape(dbsz, wc, A_KV_W),
                                cache_win_v[0].reshape(dbsz, wc, A_KV_W)], axis=-1)
    ys, kvs, ss, cs = _sample_layer(x_sample, p_sample[0], kv_cache, state_hgrn[0], cache_ffn_conv[0], w,
                                    tm_tok=256, nseq_mix=4)
    keep = min(WINDOW, seq)

    def heads(a):
        return a.reshape(a.shape[0], a.shape[1], A_KV_HEADS, A_HEAD_DIM)[None]

    return (yp, ys,
            heads(kvp[:, seq - keep:, 0:A_KV_W]), heads(kvp[:, seq - keep:, A_KV_W:]),
            sp[None], cp[None],
            heads(kvs[:, :, 0:A_KV_W]), heads(kvs[:, :, A_KV_W:]),
            ss[None], cs[None])
```

```python
import functools
import math

import jax
import jax.numpy as jnp
from jax import lax
from jax.experimental import pallas as pl
from jax.experimental.pallas import tpu as pltpu

D_MODEL = 1024
CHUNK = 64
A_HEADS = 16
A_KV_HEADS = 2
A_HEAD_DIM = 64
A_GROUP = A_HEADS // A_KV_HEADS
WINDOW = 128
A_Q_W = A_HEADS * A_HEAD_DIM
A_KV_W = A_KV_HEADS * A_HEAD_DIM
NUM_BUCKETS = 32
MAX_DISTANCE = 128
B_HEADS = 8
B_KEY_DIM = 128
B_VAL_DIM = D_MODEL // B_HEADS
B_KEY_W = B_HEADS * B_KEY_DIM
B_VAL_W = B_HEADS * B_VAL_DIM
D_FF = 2816
CONV_W = 3
PLE_DIM = 256
EPS = 1e-6
NEG_INF = -1e30

_QA0 = 0
_KV0 = A_Q_W
_HB0 = _KV0 + 2 * A_KV_W
_GG0 = _HB0 + 2 * B_KEY_W + 2 * B_VAL_W
IN_COLS = _GG0 + 2 * D_MODEL
_HB_W = _GG0 - _HB0

_BF16_ROWS = 16
_VMEM_LIMIT = 56 * 1024 * 1024

BF16 = jnp.bfloat16
F32 = jnp.float32


def _score_rows(lk):
    return lk + _BF16_ROWS


def _const_spec(shape):
    nd = len(shape)
    return pl.BlockSpec(shape, lambda *_: (0,) * nd, pipeline_mode=pl.Buffered(1))


def _rms(x, g):
    ms = jnp.mean(x * x, axis=-1, keepdims=True)
    return x * lax.rsqrt(ms + EPS) * g


def _sigmoid_of_twice(hx):
    return 0.5 * jnp.tanh(hx) + 0.5


def _dot(a, b):
    return jnp.dot(a, b, preferred_element_type=F32)


def _dot_nt(a, b):
    return lax.dot_general(a, b, (((1,), (1,)), ((), ())), preferred_element_type=F32)


def _dot_tn(a, b):
    return lax.dot_general(a, b, (((0,), (0,)), ((), ())), preferred_element_type=F32)


def _interleave(a, b):
    out, nb = [], 0
    for i, t in enumerate(a):
        out.append(t)
        want = ((i + 1) * len(b)) // len(a)
        out.extend(b[nb:want])
        nb = want
    return out + b[nb:]


def _attn_thunks(nchunk, lq, lk, load_q, load_kw, bias_strip, store_o):
    def scores(c):
        qc = load_q(c)
        kw = load_kw(c)
        st = []
        for k in range(A_KV_HEADS):
            qs = jnp.concatenate(
                [qc[:, (k * A_GROUP + g) * A_HEAD_DIM:(k * A_GROUP + g + 1) * A_HEAD_DIM]
                 for g in range(A_GROUP)], axis=0)
            st.append(_dot_nt(kw[:, k * A_HEAD_DIM:(k + 1) * A_HEAD_DIM], qs))
        return dict(kw=kw, st=st)

    def softmax(c, s):
        ot, rden = [], []
        for k in range(A_KV_HEADS):
            ps, rs = [], []
            for j in range(0, A_GROUP * lq, 128):
                t = s["st"][k][:, j:j + 128] + bias_strip(c, k, j)
                m = jnp.max(t, axis=0, keepdims=True)
                p = jnp.exp(t - m)
                rs.append(1.0 / jnp.sum(p, axis=0, keepdims=True))
                ps.append(p.astype(BF16))
            rden.append(jnp.concatenate(rs, axis=1))
            vv = s["kw"][:, A_KV_W + k * A_HEAD_DIM:A_KV_W + (k + 1) * A_HEAD_DIM]
            ot.append(_dot_tn(vv, jnp.concatenate(ps, axis=1)))
        return dict(ot=ot, rden=rden)

    def out(c, s):
        outs = []
        for k in range(A_KV_HEADS):
            o = (s["ot"][k] * s["rden"][k]).T
            outs.append(jnp.concatenate([o[g * lq:(g + 1) * lq, :] for g in range(A_GROUP)], axis=1))
        store_o(c, jnp.concatenate(outs, axis=1).astype(BF16))

    ahead = 2
    sc, sm, th = {}, {}, []

    def do_scores(c):
        sc[c] = scores(c)

    def do_softmax(c):
        sm[c] = softmax(c, sc.pop(c))

    def do_out(c):
        out(c, sm.pop(c))

    for c in range(min(ahead, nchunk)):
        th.append(functools.partial(do_scores, c))
    for c in range(nchunk):
        if c + ahead < nchunk:
            th.append(functools.partial(do_scores, c + ahead))
        th.append(functools.partial(do_softmax, c))
        if c >= 1:
            th.append(functools.partial(do_out, c - 1))
    th.append(functools.partial(do_out, nchunk - 1))
    return th


def _attn_body(q_ref, kvc_ref, kvp_ref, bias_ref, o_ref, kv_s, *, lq, lk, nchunk, tq, nseq):
    zpad = jnp.zeros((_score_rows(lk) - lk, 2 * A_KV_W), BF16)
    streams = []
    for n in range(nseq):
        kv_s[n, 0:WINDOW, :] = kvp_ref[n].astype(BF16)
        kv_s[n, WINDOW:WINDOW + tq, :] = kvc_ref[n].astype(BF16)

        def store_o(c, o, n=n):
            o_ref[n, c * lq:(c + 1) * lq, :] = o

        streams.append(_attn_thunks(
            nchunk, lq, lk,
            load_q=lambda c, n=n: q_ref[n, c * lq:(c + 1) * lq, :],
            load_kw=lambda c, n=n: jnp.concatenate([kv_s[n, c * lq:c * lq + lk, :], zpad], axis=0),
            bias_strip=lambda c, k, j: bias_ref[0, k, :, j:j + 128],
            store_o=store_o))
    for group in zip(*streams):
        for t in group:
            t()


def _attention_cached(qa, kv, kv_prev, bias_ext, *, lq, nseq):
    bsz, t, _ = qa.shape
    lk = WINDOW + lq
    body = functools.partial(_attn_body, lq=lq, lk=lk, nchunk=t // lq, tq=t, nseq=nseq)
    return pl.pallas_call(
        body,
        grid=(bsz // nseq,),
        in_specs=[
            pl.BlockSpec((nseq, t, A_Q_W), lambda b: (b, 0, 0)),
            pl.BlockSpec((nseq, t, 2 * A_KV_W), lambda b: (b, 0, 0)),
            pl.BlockSpec((nseq, WINDOW, 2 * A_KV_W), lambda b: (b, 0, 0)),
            _const_spec(bias_ext.shape),
        ],
        out_specs=pl.BlockSpec((nseq, t, A_Q_W), lambda b: (b, 0, 0)),
        out_shape=jax.ShapeDtypeStruct((bsz, t, A_Q_W), BF16),
        scratch_shapes=[pltpu.VMEM((nseq, WINDOW + t, 2 * A_KV_W), BF16)],
        compiler_params=pltpu.CompilerParams(
            dimension_semantics=("parallel",), vmem_limit_bytes=_VMEM_LIMIT),
        name="attention",
    )(qa, kv, kv_prev, bias_ext)


def _cumsum_rows(x, tril3):
    hi = x.astype(BF16)
    r = x - hi.astype(F32)
    mid = r.astype(BF16)
    lo = (r - mid.astype(F32)).astype(BF16)
    return _dot(tril3, jnp.concatenate([hi, mid, lo], axis=0))


def _hgrn_thunks(nchunk, blk, load, store_y, st_s, lbl, g):
    e = jnp.exp(lbl - jnp.max(lbl, axis=0, keepdims=True))
    lb = e[0:1, :] / jnp.sum(e, axis=0, keepdims=True)
    fa = 0.5 * (1.0 + lb)
    fb = 0.5 * (1.0 - lb)
    ri = lax.broadcasted_iota(jnp.int32, (blk, blk), 0)
    ci = lax.broadcasted_iota(jnp.int32, (blk, blk), 1)
    tril = (ri >= ci).astype(BF16)
    tril3 = jnp.concatenate([tril, tril, tril], axis=1)
    ri2 = lax.broadcasted_iota(jnp.int32, (blk, 2 * blk), 0)
    ci2 = lax.broadcasted_iota(jnp.int32, (blk, 2 * blk), 1)
    causal2 = ri2 >= (ci2 & (blk - 1))
    mid = blk // 2
    w = B_KEY_W
    pw = 2 * B_KEY_DIM
    npair = B_HEADS // 2
    hs = [slice(h * B_KEY_DIM, (h + 1) * B_KEY_DIM) for h in range(B_HEADS)]
    ps = [slice(j * pw, (j + 1) * pw) for j in range(npair)]

    def blockdiag(x0, x1):
        z = jnp.zeros_like(x0)
        return jnp.concatenate([jnp.concatenate([x0, z], axis=1), jnp.concatenate([z, x1], axis=1)], axis=0)

    def stage_decay(c):
        bt = fb * jnp.tanh(load(c, w, 2 * w))
        f = fa + bt
        cum = _cumsum_rows(jnp.log2(f), tril3)
        return dict(kk=fb - bt, cum=cum)

    def stage_state(c, s):
        cum = s["cum"]
        hq = load(c, 0, w)
        qs = hq + hq * jnp.tanh(hq)
        b_last = cum[blk - 1:blk, :]
        b_mid = cum[mid:mid + 1, :]
        q2f = qs * jnp.exp2(cum - b_mid)
        k2f = s["kk"] * jnp.exp2(b_mid - cum)
        q1 = (q2f * jnp.exp2(b_mid)).astype(BF16)
        k3 = (k2f * jnp.exp2(b_last - b_mid)).astype(BF16)
        q2 = q2f.astype(BF16)
        k2 = k2f.astype(BF16)
        vb = load(c, 2 * w, 3 * w).astype(BF16)
        dec = jnp.exp2(b_last)
        a = [_dot_nt(q2[:, ps[j]], blockdiag(k2[:, hs[2 * j]], k2[:, hs[2 * j + 1]])) for j in range(npair)]
        st = [st_s[h] for h in range(B_HEADS)]
        o1 = [_dot_nt(q1[:, ps[j]], blockdiag(st[2 * j].astype(BF16), st[2 * j + 1].astype(BF16)))
              for j in range(npair)]
        upd = []
        for q in range(0, B_HEADS, 4):
            vstack = jnp.concatenate([vb[:, hs[h]] for h in range(q, q + 4)], axis=0)
            kdiag = jnp.concatenate(
                [jnp.concatenate([k3[:, hs[h]] if h == h2 else jnp.zeros((blk, B_KEY_DIM), BF16)
                                  for h2 in range(q, q + 4)], axis=1) for h in range(q, q + 4)], axis=0)
            upd.append(_dot_tn(vstack, kdiag))
        for h, sl in enumerate(hs):
            st_s[h] = dec[:, sl] * st[h] + upd[h // 4][:, (h % 4) * B_KEY_DIM:(h % 4 + 1) * B_KEY_DIM]
        return dict(a=a, o1=o1, vb=vb)

    def stage_out(c, s):
        vb = s["vb"]
        am = [jnp.where(causal2, s["a"][j], 0.0).astype(BF16) for j in range(npair)]
        o2 = [_dot(am[j], blockdiag(vb[:, hs[2 * j]], vb[:, hs[2 * j + 1]])) for j in range(npair)]
        o = [s["o1"][j] + o2[j] for j in range(npair)]
        ys = [_rms(o[h // 2][:, (h % 2) * B_VAL_DIM:(h % 2 + 1) * B_VAL_DIM], g) for h in range(B_HEADS)]
        hog = load(c, 3 * w, 4 * w)
        y = jnp.concatenate(ys, axis=1) * (hog + hog * jnp.tanh(hog))
        store_y(c, y.astype(BF16))

    dec, sta, th = {}, {}, []

    def do_decay(c):
        dec[c] = stage_decay(c)

    def do_state(c):
        sta[c] = stage_state(c, dec.pop(c))

    def do_out(c):
        stage_out(c, sta.pop(c))

    th.append(functools.partial(do_decay, 0))
    for c in range(nchunk):
        if c + 1 < nchunk:
            th.append(functools.partial(do_decay, c + 1))
        th.append(functools.partial(do_state, c))
        if c >= 1:
            th.append(functools.partial(do_out, c - 1))
    th.append(functools.partial(do_out, nchunk - 1))
    return th


def _hgrn_body(hb_ref, lbl_ref, g_ref, s0_ref, yb_ref, sfin_ref, st_s, *, blk, nchunk, nseq):
    streams = []
    for q in range(nseq):
        for h in range(B_HEADS):
            st_s[q, h] = s0_ref[q, h].T

        def store_y(c, y, q=q):
            yb_ref[q, c * blk:(c + 1) * blk, :] = y

        streams.append(_hgrn_thunks(
            nchunk, blk, lambda c, lo, hi, q=q: hb_ref[q, c * blk:(c + 1) * blk, lo:hi],
            store_y, st_s.at[q], lbl_ref[...], g_ref[...]))
    for group in zip(*streams):
        for t in group:
            t()
    for q in range(nseq):
        for h in range(B_HEADS):
            sfin_ref[q, h] = st_s[q, h].T


def _hgrn_cached(hb, lb_logits, g_out, s0, *, blk, nseq):
    bsz, t, _ = hb.shape
    body = functools.partial(_hgrn_body, blk=blk, nchunk=t // blk, nseq=nseq)
    st_spec = pl.BlockSpec((nseq, B_HEADS, B_KEY_DIM, B_VAL_DIM), lambda b: (b, 0, 0, 0))
    return pl.pallas_call(
        body,
        grid=(bsz // nseq,),
        in_specs=[pl.BlockSpec((nseq, t, _HB_W), lambda b: (b, 0, 0)),
                  _const_spec(lb_logits.shape), _const_spec((1, B_VAL_DIM)), st_spec],
        out_specs=[pl.BlockSpec((nseq, t, B_VAL_W), lambda b: (b, 0, 0)), st_spec],
        out_shape=[jax.ShapeDtypeStruct((bsz, t, B_VAL_W), BF16),
                   jax.ShapeDtypeStruct((bsz, B_HEADS, B_KEY_DIM, B_VAL_DIM), F32)],
        scratch_shapes=[pltpu.VMEM((nseq, B_HEADS, B_VAL_DIM, B_KEY_DIM), F32)],
        compiler_params=pltpu.CompilerParams(
            dimension_semantics=("parallel",), vmem_limit_bytes=_VMEM_LIMIT),
        name="hgrn2",
    )(hb, lb_logits, g_out, s0)


def _inproj_body(x_ref, g_ref, w_ref, qa_ref, kv_ref, hb_ref, gg_ref):
    h = _rms(x_ref[...], g_ref[...]).astype(BF16)
    step = 512

    def mm(lo, width):
        return _dot(h, w_ref[:, lo:lo + width])

    for j in range(0, A_Q_W, step):
        qa_ref[:, j:j + step] = mm(_QA0 + j, step).astype(BF16)
    kv_ref[...] = mm(_KV0, 2 * A_KV_W)
    for j in range(0, _HB_W, step):
        hb_ref[:, j:j + step] = mm(_HB0 + j, step)
    for j in range(0, 2 * D_MODEL, step):
        gg_ref[:, j:j + step] = mm(_GG0 + j, step).astype(BF16)


def _inproj(x2d, g, w_bf, tm):
    n = x2d.shape[0]
    row = lambda w: pl.BlockSpec((tm, w), lambda i: (i, 0))
    return pl.pallas_call(
        _inproj_body,
        grid=(n // tm,),
        in_specs=[row(D_MODEL), _const_spec((1, D_MODEL)), _const_spec((D_MODEL, IN_COLS))],
        out_specs=[row(A_Q_W), row(2 * A_KV_W), row(_HB_W), row(2 * D_MODEL)],
        out_shape=[
            jax.ShapeDtypeStruct((n, A_Q_W), BF16),
            jax.ShapeDtypeStruct((n, 2 * A_KV_W), F32),
            jax.ShapeDtypeStruct((n, _HB_W), F32),
            jax.ShapeDtypeStruct((n, 2 * D_MODEL), BF16),
        ],
        compiler_params=pltpu.CompilerParams(
            dimension_semantics=("parallel",), vmem_limit_bytes=_VMEM_LIMIT),
        name="inproj",
    )(x2d, g, w_bf)


def _mixer_body(x_ref, gpre_ref, w_ref, lbl_ref, ghg_ref, bias_ref,
                kv_ref, gg_ref, ya_ref, yb_ref, sfin_ref,
                h_s, qa_s, kvb_s, kvw_s, hb_s, st_s, *, tm, tiles_per_seq):
    s = pl.program_id(0)
    cur = s % 2
    prv = 1 - cur
    tib = (s + tiles_per_seq - 1) % tiles_per_seq
    nchunk = tm // CHUNK
    lk = WINDOW + CHUNK
    piece = 256

    @pl.when(s == 0)
    def _():
        qa_s[...] = jnp.zeros(qa_s.shape, BF16)
        kvb_s[...] = jnp.zeros(kvb_s.shape, BF16)
        kvw_s[...] = jnp.zeros(kvw_s.shape, BF16)
        hb_s[...] = jnp.zeros(hb_s.shape, F32)

    @pl.when((s == 0) | (tib == 0))
    def _():
        st_s[...] = jnp.zeros(st_s.shape, F32)

    kvw_s[0:WINDOW, :] = kvw_s[tm:tm + WINDOW, :]
    kvw_s[WINDOW:WINDOW + tm, :] = kvb_s[prv]

    def dense_piece(lo, width):
        z = _dot(h_s[...], w_ref[:, lo:lo + width])
        if lo < _KV0:
            qa_s[cur, :, lo:lo + width] = z.astype(BF16)
        elif lo < _HB0:
            kv_ref[...] = z
            kvb_s[cur] = z.astype(BF16)
        elif lo < _GG0:
            hb_s[:, lo - _HB0:lo - _HB0 + width] = z
        else:
            gg_ref[:, lo - _GG0:lo - _GG0 + width] = z.astype(BF16)

    def pieces(lo, hi):
        return [functools.partial(dense_piece, c, min(piece, hi - c)) for c in range(lo, hi, piece)]

    dense_hb = pieces(_HB0, _GG0)
    dense_rest = pieces(_QA0, _KV0) + pieces(_KV0, _HB0) + pieces(_GG0, IN_COLS)

    def store_ya(c, o):
        ya_ref[c * CHUNK:(c + 1) * CHUNK, :] = o

    def store_yb(c, y):
        yb_ref[c * CHUNK:(c + 1) * CHUNK, :] = y

    zpad = jnp.zeros((_score_rows(lk) - lk, 2 * A_KV_W), BF16)

    def bias_strip(c, k, j):
        var = jnp.clip(WINDOW // CHUNK - (tib * nchunk + c), 0, WINDOW // CHUNK)
        return bias_ref[var, k, :, j:j + 128]

    attn = _attn_thunks(
        nchunk, CHUNK, lk,
        load_q=lambda c: qa_s[prv, c * CHUNK:(c + 1) * CHUNK, :],
        load_kw=lambda c: jnp.concatenate([kvw_s[c * CHUNK:c * CHUNK + lk, :], zpad], axis=0),
        bias_strip=bias_strip, store_o=store_ya)
    hgrn = _hgrn_thunks(
        nchunk, CHUNK, lambda c, lo, hi: hb_s[c * CHUNK:(c + 1) * CHUNK, lo:hi],
        store_yb, st_s, lbl_ref[...], ghg_ref[...])
    for t in attn[:2]:
        t()
    h_s[...] = _rms(x_ref[...], gpre_ref[...]).astype(BF16)
    for t in _interleave(dense_rest, hgrn) + _interleave(dense_hb, attn[2:]):
        t()

    @pl.when((tib == tiles_per_seq - 1) & (s > 0))
    def _():
        for h in range(B_HEADS):
            sfin_ref[h] = st_s[h].T


def _mixer(x2d, g_pre, w_bf, lb_logits, g_hgrn, bias_ext, *, tm, seq):
    n = x2d.shape[0]
    nt = n // tm
    tiles_per_seq = seq // tm
    dense_row = lambda w: pl.BlockSpec((tm, w), lambda s: (jnp.minimum(s, nt - 1), 0))
    lag_row = lambda w: pl.BlockSpec((tm, w), lambda s: (jnp.maximum(s - 1, 0), 0))
    body = functools.partial(_mixer_body, tm=tm, tiles_per_seq=tiles_per_seq)
    return pl.pallas_call(
        body,
        grid=(nt + 1,),
        in_specs=[dense_row(D_MODEL), _const_spec((1, D_MODEL)), _const_spec((D_MODEL, IN_COLS)),
                  _const_spec(lb_logits.shape), _const_spec((1, B_VAL_DIM)), _const_spec(bias_ext.shape)],
        out_specs=[dense_row(2 * A_KV_W), dense_row(2 * D_MODEL), lag_row(A_Q_W), lag_row(B_VAL_W),
                   pl.BlockSpec((None, B_HEADS, B_KEY_DIM, B_VAL_DIM),
                                lambda s: (jnp.maximum(s - 1, 0) // tiles_per_seq, 0, 0, 0))],
        out_shape=[jax.ShapeDtypeStruct((n, 2 * A_KV_W), F32),
                   jax.ShapeDtypeStruct((n, 2 * D_MODEL), BF16),
                   jax.ShapeDtypeStruct((n, A_Q_W), BF16),
                   jax.ShapeDtypeStruct((n, B_VAL_W), BF16),
                   jax.ShapeDtypeStruct((n // seq, B_HEADS, B_KEY_DIM, B_VAL_DIM), F32)],
        scratch_shapes=[pltpu.VMEM((tm, D_MODEL), BF16),
                        pltpu.VMEM((2, tm, A_Q_W), BF16),
                        pltpu.VMEM((2, tm, 2 * A_KV_W), BF16),
                        pltpu.VMEM((WINDOW + tm, 2 * A_KV_W), BF16),
                        pltpu.VMEM((tm, _HB_W), F32),
                        pltpu.VMEM((B_HEADS, B_VAL_DIM, B_KEY_DIM), F32)],
        compiler_params=pltpu.CompilerParams(
            dimension_semantics=("arbitrary",), vmem_limit_bytes=_VMEM_LIMIT),
        name="mixer",
    )(x2d, g_pre, w_bf, lb_logits, g_hgrn, bias_ext)


def _merge_body(ya_ref, yb_ref, gg_ref, x_ref, wa_ref, wb_ref, wo_ref, g_ref, o_ref, *, tm, strip):
    def branches(r):
        rows = slice(r * strip, (r + 1) * strip)
        ga = gg_ref[rows, 0:D_MODEL].astype(F32)
        gb = gg_ref[rows, D_MODEL:2 * D_MODEL].astype(F32)
        mix = (_sigmoid_of_twice(ga) * _dot(ya_ref[rows, :], wa_ref[...])
               + _sigmoid_of_twice(gb) * _dot(yb_ref[rows, :], wb_ref[...]))
        return mix.astype(BF16)

    def project(r, mix):
        rows = slice(r * strip, (r + 1) * strip)
        o_ref[rows, :] = x_ref[rows, :] + _rms(_dot(mix, wo_ref[...]), g_ref[...])

    nstrip = tm // strip
    mix = branches(0)
    for r in range(nstrip):
        nxt = branches(r + 1) if r + 1 < nstrip else None
        project(r, mix)
        mix = nxt


def _merge(ya, yb, gg, x2d, wa, wb, wo, g, tm):
    n = x2d.shape[0]
    row = lambda w: pl.BlockSpec((tm, w), lambda i: (i, 0))
    wspec = _const_spec((D_MODEL, D_MODEL))
    return pl.pallas_call(
        functools.partial(_merge_body, tm=tm, strip=min(tm, 256)),
        grid=(n // tm,),
        in_specs=[row(A_Q_W), row(B_VAL_W), row(2 * D_MODEL), row(D_MODEL),
                  wspec, wspec, wspec, _const_spec((1, D_MODEL))],
        out_specs=row(D_MODEL),
        out_shape=jax.ShapeDtypeStruct((n, D_MODEL), F32),
        compiler_params=pltpu.CompilerParams(
            dimension_semantics=("parallel",), vmem_limit_bytes=_VMEM_LIMIT),
        name="merge",
    )(ya, yb, gg, x2d, wa, wb, wo, g)


def _gelu_tanh(x):
    c = math.sqrt(2.0 / math.pi)
    return 0.5 * x * (1.0 + jnp.tanh(c * (x + 0.044715 * (x * x * x))))


def _ffn_up(x, gpre_ref, wup_ref):
    hf = _rms(x, gpre_ref[...]).astype(BF16)
    return _dot(hf, wup_ref[:, 0:D_FF]), _dot(hf, wup_ref[:, D_FF:2 * D_FF])


def _ffn_down(x, pe, a, u, a1, a2, wconv_ref, bconv_ref, wdown_ref, gpost_ref, wple_ref, wgate_ref):
    ac = bconv_ref[...] + a2 * wconv_ref[0:1, :] + a1 * wconv_ref[1:2, :] + a * wconv_ref[2:3, :]
    gl = (_gelu_tanh(ac) * u).astype(BF16)
    x2 = x + _rms(_dot(gl, wdown_ref[...]), gpost_ref[...])
    pex = _dot(pe.astype(BF16), wple_ref[...])
    gate = _sigmoid_of_twice(_dot(x2.astype(BF16), wgate_ref[...]))
    return x2 + pex * gate


def _ffn_tile(x, pe, shifted, gpre_ref, wup_ref, *rest):
    a, u = _ffn_up(x, gpre_ref, wup_ref)
    a1, a2 = shifted(a)
    return _ffn_down(x, pe, a, u, a1, a2, *rest), a


def _ffn_body(x_ref, pe_ref, *refs, tm, strip):
    (gpre_ref, wup_ref), rest, (o_ref, tail_ref, carry_s) = refs[:2], refs[2:8], refs[8:]
    i = pl.program_id(1)

    @pl.when(i == 0)
    def _():
        carry_s[...] = jnp.zeros((CONV_W - 1, D_FF), F32)

    row = lax.broadcasted_iota(jnp.int32, (strip, D_FF), 0)
    nstrip = tm // strip
    rows = [slice(r * strip, (r + 1) * strip) for r in range(nstrip)]
    hist = carry_s[...]
    au = _ffn_up(x_ref[rows[0], :], gpre_ref, wup_ref)
    for r in range(nstrip):
        nxt = _ffn_up(x_ref[rows[r + 1], :], gpre_ref, wup_ref) if r + 1 < nstrip else None
        a, u = au
        c0, c1 = hist[0:1, :], hist[1:2, :]
        a1 = jnp.where(row == 0, c1, pltpu.roll(a, 1, 0))
        a2 = jnp.where(row == 0, c0, jnp.where(row == 1, c1, pltpu.roll(a, 2, 0)))
        o_ref[rows[r], :] = _ffn_down(x_ref[rows[r], :], pe_ref[rows[r], :], a, u, a1, a2, *rest)
        hist = a[strip - (CONV_W - 1):strip, :]
        au = nxt
    carry_s[...] = hist
    tail_ref[...] = hist


def _ffn_cached_body(x_ref, pe_ref, prev_ref, *refs, t, nseq):
    w_refs, (o_ref, tail_ref) = refs[:8], refs[8:]
    pos = lax.broadcasted_iota(jnp.int32, (nseq * t, D_FF), 0) & (t - 1)

    def history(j):
        return jnp.concatenate([jnp.broadcast_to(prev_ref[q, j:j + 1, :], (t, D_FF)) for q in range(nseq)],
                               axis=0)

    def shifted(a):
        p0, p1 = history(0), history(1)
        a1 = jnp.where(pos == 0, p1, pltpu.roll(a, 1, 0))
        a2 = jnp.where(pos == 0, p0, jnp.where(pos == 1, p1, pltpu.roll(a, 2, 0)))
        return a1, a2

    x = x_ref[...].reshape(nseq * t, D_MODEL)
    pe = pe_ref[...].reshape(nseq * t, PLE_DIM)
    out, a = _ffn_tile(x, pe, shifted, *w_refs)
    o_ref[...] = out.reshape(nseq, t, D_MODEL)
    for q in range(nseq):
        tail_ref[q] = a[(q + 1) * t - (CONV_W - 1):(q + 1) * t, :]


def _ffn_weight_specs():
    return [_const_spec((1, D_MODEL)), _const_spec((D_MODEL, 2 * D_FF)),
            _const_spec((CONV_W, D_FF)), _const_spec((1, D_FF)),
            _const_spec((D_FF, D_MODEL)), _const_spec((1, D_MODEL)),
            _const_spec((PLE_DIM, D_MODEL)), _const_spec((D_MODEL, D_MODEL))]


def _ffn(x, pe, weights, tm):
    bsz, t, _ = x.shape
    tail_spec = pl.BlockSpec((None, CONV_W - 1, D_FF), lambda b, i: (b, 0, 0))
    return pl.pallas_call(
        functools.partial(_ffn_body, tm=tm, strip=min(tm, 256)),
        grid=(bsz, t // tm),
        in_specs=[pl.BlockSpec((None, tm, D_MODEL), lambda b, i: (b, i, 0)),
                  pl.BlockSpec((None, tm, PLE_DIM), lambda b, i: (b, i, 0))] + _ffn_weight_specs(),
        out_specs=[pl.BlockSpec((None, tm, D_MODEL), lambda b, i: (b, i, 0)), tail_spec],
        out_shape=[jax.ShapeDtypeStruct((bsz, t, D_MODEL), F32),
                   jax.ShapeDtypeStruct((bsz, CONV_W - 1, D_FF), F32)],
        scratch_shapes=[pltpu.VMEM((CONV_W - 1, D_FF), F32)],
        compiler_params=pltpu.CompilerParams(
            dimension_semantics=("parallel", "arbitrary"), vmem_limit_bytes=_VMEM_LIMIT),
        name="convffn",
    )(x, pe, *weights)


def _ffn_cached(x, pe, prev, weights, nseq):
    bsz, t, _ = x.shape
    tail_spec = pl.BlockSpec((nseq, CONV_W - 1, D_FF), lambda b: (b, 0, 0))
    return pl.pallas_call(
        functools.partial(_ffn_cached_body, t=t, nseq=nseq),
        grid=(bsz // nseq,),
        in_specs=[pl.BlockSpec((nseq, t, D_MODEL), lambda b: (b, 0, 0)),
                  pl.BlockSpec((nseq, t, PLE_DIM), lambda b: (b, 0, 0)), tail_spec] + _ffn_weight_specs(),
        out_specs=[pl.BlockSpec((nseq, t, D_MODEL), lambda b: (b, 0, 0)), tail_spec],
        out_shape=[jax.ShapeDtypeStruct((bsz, t, D_MODEL), F32),
                   jax.ShapeDtypeStruct((bsz, CONV_W - 1, D_FF), F32)],
        compiler_params=pltpu.CompilerParams(
            dimension_semantics=("parallel",), vmem_limit_bytes=_VMEM_LIMIT),
        name="convffn_cached",
    )(x, pe, prev, *weights)


def _t5_bucket(rel):
    nb = NUM_BUCKETS // 2
    ret = jnp.where(rel > 0, nb, 0)
    n = jnp.abs(rel)
    max_exact = nb // 2
    large = max_exact + (jnp.log(jnp.maximum(n, max_exact).astype(jnp.float32) / max_exact)
                         / math.log(MAX_DISTANCE / max_exact) * (nb - max_exact)).astype(jnp.int32)
    large = jnp.minimum(large, nb - 1)
    return ret + jnp.where(n < max_exact, n, large)


def _bias_body(table_ref, sinks_ref, bk_ref, o_ref, *, nvar):
    bk = bk_ref[...]
    row = lax.broadcasted_iota(jnp.int32, bk.shape, 0)
    for k in range(A_KV_HEADS):
        acc = jnp.where(bk == -1, sinks_ref[k], NEG_INF)
        for b in range(NUM_BUCKETS):
            acc = jnp.where(bk == b, table_ref[k, b:b + 1, :], acc)
        for v in range(nvar):
            o_ref[v, k] = jnp.where(row < v * CHUNK, NEG_INF, acc)


def _bias_ext(table, sinks, lq, lk, nvar):
    rows = _score_rows(lk)
    q_pos = jnp.arange(lq) + WINDOW
    k_pos = jnp.arange(lk)
    buckets = _t5_bucket(k_pos[:, None] - q_pos[None, :]).astype(jnp.int32)
    bk = jnp.concatenate([buckets, jnp.full((1, lq), -1, jnp.int32),
                          jnp.full((rows - lk - 1, lq), -2, jnp.int32)], axis=0)
    bk = jnp.tile(bk, (1, A_GROUP))
    tab = jnp.repeat(table.astype(F32).reshape(NUM_BUCKETS, A_KV_HEADS, A_GROUP), lq, axis=2)
    tab = jnp.transpose(tab, (1, 0, 2))
    snk = jnp.repeat(sinks.astype(F32).reshape(A_KV_HEADS, 1, A_GROUP), lq, axis=2)
    vmem = pl.BlockSpec(memory_space=pltpu.VMEM)
    return pl.pallas_call(
        functools.partial(_bias_body, nvar=nvar),
        in_specs=[vmem, vmem, vmem],
        out_specs=vmem,
        out_shape=jax.ShapeDtypeStruct((nvar, A_KV_HEADS, rows, A_GROUP * lq), F32),
        name="relbias",
    )(tab, snk, bk)


def _ffn_weights(w):
    return [w[k] for k in ("g_pre_ffn", "w_up", "w_conv", "b_conv", "w_down", "g_post_ffn", "w_ple", "w_ple_gate")]


def _prompt_layer(x, pe, w, *, tm_mix, tm_tok, tm_ffn):
    bsz, t, _ = x.shape
    n = bsz * t
    x2d = x.reshape(n, D_MODEL)
    bias_ext = _bias_ext(w["rel_table"], w["sinks"], CHUNK, WINDOW + CHUNK, WINDOW // CHUNK + 1)
    kv, gg, ya, yb, s_fin = _mixer(x2d, w["g_pre_mix"], w["w_in"], w["lb_logits"], w["g_hgrn_out"],
                                   bias_ext, tm=tm_mix, seq=t)
    x1 = _merge(ya, yb, gg, x2d, w["w_br_a"], w["w_br_b"], w["w_out"], w["g_post_mix"], tm_tok)
    y, conv_tail = _ffn(x1.reshape(bsz, t, D_MODEL), pe, _ffn_weights(w), tm_ffn)
    return y, kv.reshape(bsz, t, 2 * A_KV_W), s_fin, conv_tail


def _sample_layer(x, pe, kv_prev, s_prev, conv_prev, w, *, tm_tok, nseq_mix):
    bsz, t, _ = x.shape
    n = bsz * t
    x2d = x.reshape(n, D_MODEL)
    qa, kv, hb, gg = _inproj(x2d, w["g_pre_mix"], w["w_in"], tm_tok)
    bias_ext = _bias_ext(w["rel_table"], w["sinks"], t, WINDOW + t, 1)
    kv3 = kv.reshape(bsz, t, 2 * A_KV_W)
    ya = _attention_cached(qa.reshape(bsz, t, A_Q_W), kv3, kv_prev, bias_ext, lq=t, nseq=nseq_mix)
    yb, s_fin = _hgrn_cached(hb.reshape(bsz, t, _HB_W), w["lb_logits"], w["g_hgrn_out"], s_prev,
                             blk=t, nseq=nseq_mix)
    x1 = _merge(ya.reshape(n, A_Q_W), yb.reshape(n, B_VAL_W), gg, x2d,
                w["w_br_a"], w["w_br_b"], w["w_out"], w["g_post_mix"], tm_tok)
    y, conv_tail = _ffn_cached(x1.reshape(bsz, t, D_MODEL), pe, conv_prev, _ffn_weights(w), tm_tok // t)
    return y, kv3, s_fin, conv_tail


def _scale_in_cols(w_in):
    h = B_KEY_W
    scale = jnp.concatenate([
        jnp.full((A_Q_W,), A_HEAD_DIM ** -0.5, F32), jnp.ones((2 * A_KV_W,), F32),
        jnp.full((2 * h,), 0.5, F32), jnp.ones((B_VAL_W,), F32), jnp.full((B_VAL_W,), 0.5, F32),
        jnp.full((2 * D_MODEL,), 0.5, F32)])
    return w_in * scale[None, :]


def kernel(x_prompt, x_sample, cache_win_k, cache_win_v, state_hgrn, cache_ffn_conv, p_prompt, p_sample,
           rel_bias_table, lb_logits, g_pre_mix, w_in, attn_sinks, g_hgrn_out, w_br_a, w_br_b, w_out,
           g_post_mix, g_pre_ffn, w_up, w_conv, b_conv, w_down, g_post_ffn, w_ple, w_ple_gate):
    bsz, seq, _ = x_prompt.shape
    dbsz, dseq, _ = x_sample.shape
    w = {
        "rel_table": rel_bias_table, "sinks": attn_sinks[0], "lb_logits": lb_logits.astype(F32),
        "g_pre_mix": g_pre_mix[0][None, :], "w_in": _scale_in_cols(w_in[0]).astype(BF16),
        "g_hgrn_out": g_hgrn_out[0][None, :],
        "w_br_a": w_br_a[0].astype(BF16), "w_br_b": w_br_b[0].astype(BF16), "w_out": w_out[0].astype(BF16),
        "g_post_mix": g_post_mix[0][None, :], "g_pre_ffn": g_pre_ffn[0][None, :],
        "w_up": w_up[0].astype(BF16), "w_conv": w_conv[0], "b_conv": b_conv[0][None, :],
        "w_down": w_down[0].astype(BF16), "g_post_ffn": g_post_ffn[0][None, :],
        "w_ple": w_ple[0].astype(BF16), "w_ple_gate": (0.5 * w_ple_gate[0]).astype(BF16),
    }
    yp, kvp, sp, cp = _prompt_layer(x_prompt, p_prompt[0], w, tm_mix=512, tm_tok=512, tm_ffn=512)
    wc = cache_win_k.shape[2]
    kv_cache = jnp.concatenate([cache_win_k[0].reshape(dbsz, wc, A_KV_W),
                                cache_win_v[0].reshape(dbsz, wc, A_KV_W)], axis=-1)
    ys, kvs, ss, cs = _sample_layer(x_sample, p_sample[0], kv_cache, state_hgrn[0], cache_ffn_conv[0], w,
                                    tm_tok=256, nseq_mix=4)
    keep = min(WINDOW, seq)

    def heads(a):
        return a.reshape(a.shape[0], a.shape[1], A_KV_HEADS, A_HEAD_DIM)[None]

    return (yp, ys,
            heads(kvp[:, seq - keep:, 0:A_KV_W]), heads(kvp[:, seq - keep:, A_KV_W:]),
            sp[None], cp[None],
            heads(kvs[:, :, 0:A_KV_W]), heads(kvs[:, :, A_KV_W:]),
            ss[None], cs[None])
```

```python
import functools
import math

import jax
import jax.numpy as jnp
from jax import lax
from jax.experimental import pallas as pl
from jax.experimental.pallas import tpu as pltpu

D_MODEL = 1024
CHUNK = 64
A_HEADS = 16
A_KV_HEADS = 2
A_HEAD_DIM = 64
A_GROUP = A_HEADS // A_KV_HEADS
WINDOW = 128
A_Q_W = A_HEADS * A_HEAD_DIM
A_KV_W = A_KV_HEADS * A_HEAD_DIM
NUM_BUCKETS = 32
MAX_DISTANCE = 128
B_HEADS = 8
B_KEY_DIM = 128
B_VAL_DIM = D_MODEL // B_HEADS
B_KEY_W = B_HEADS * B_KEY_DIM
B_VAL_W = B_HEADS * B_VAL_DIM
D_FF = 2816
CONV_W = 3
PLE_DIM = 256
EPS = 1e-6
NEG_INF = -1e30

_QA0 = 0
_KV0 = A_Q_W
_HB0 = _KV0 + 2 * A_KV_W
_GG0 = _HB0 + 2 * B_KEY_W + 2 * B_VAL_W
IN_COLS = _GG0 + 2 * D_MODEL
_HB_W = _GG0 - _HB0

_BF16_ROWS = 16
_VMEM_LIMIT = 56 * 1024 * 1024

BF16 = jnp.bfloat16
F32 = jnp.float32


def _score_rows(lk):
    return lk + _BF16_ROWS


def _const_spec(shape):
    nd = len(shape)
    return pl.BlockSpec(shape, lambda *_: (0,) * nd, pipeline_mode=pl.Buffered(1))


def _rms(x, g):
    ms = jnp.mean(x * x, axis=-1, keepdims=True)
    return x * lax.rsqrt(ms + EPS) * g


def _sigmoid_of_twice(hx):
    return 0.5 * jnp.tanh(hx) + 0.5


def _dot(a, b):
    return jnp.dot(a, b, preferred_element_type=F32)


def _dot_nt(a, b):
    return lax.dot_general(a, b, (((1,), (1,)), ((), ())), preferred_element_type=F32)


def _dot_tn(a, b):
    return lax.dot_general(a, b, (((0,), (0,)), ((), ())), preferred_element_type=F32)


def _interleave(a, b):
    out, nb = [], 0
    for i, t in enumerate(a):
        out.append(t)
        want = ((i + 1) * len(b)) // len(a)
        out.extend(b[nb:want])
        nb = want
    return out + b[nb:]


def _attn_thunks(nchunk, lq, lk, load_q, load_kw, bias_strip, store_o):
    def scores(c):
        qc = load_q(c)
        kw = load_kw(c)
        st = []
        for k in range(A_KV_HEADS):
            qs = jnp.concatenate(
                [qc[:, (k * A_GROUP + g) * A_HEAD_DIM:(k * A_GROUP + g + 1) * A_HEAD_DIM]
                 for g in range(A_GROUP)], axis=0)
            st.append(_dot_nt(kw[:, k * A_HEAD_DIM:(k + 1) * A_HEAD_DIM], qs))
        return dict(kw=kw, st=st)

    def softmax(c, s):
        ot, rden = [], []
        for k in range(A_KV_HEADS):
            ps, rs = [], []
            for j in range(0, A_GROUP * lq, 128):
                t = s["st"][k][:, j:j + 128] + bias_strip(c, k, j)
                m = jnp.max(t, axis=0, keepdims=True)
                p = jnp.exp(t - m)
                rs.append(1.0 / jnp.sum(p, axis=0, keepdims=True))
                ps.append(p.astype(BF16))
            rden.append(jnp.concatenate(rs, axis=1))
            vv = s["kw"][:, A_KV_W + k * A_HEAD_DIM:A_KV_W + (k + 1) * A_HEAD_DIM]
            ot.append(_dot_tn(vv, jnp.concatenate(ps, axis=1)))
        return dict(ot=ot, rden=rden)

    def out(c, s):
        outs = []
        for k in range(A_KV_HEADS):
            o = (s["ot"][k] * s["rden"][k]).T
            outs.append(jnp.concatenate([o[g * lq:(g + 1) * lq, :] for g in range(A_GROUP)], axis=1))
        store_o(c, jnp.concatenate(outs, axis=1).astype(BF16))

    ahead = 2
    sc, sm, th = {}, {}, []

    def do_scores(c):
        sc[c] = scores(c)

    def do_softmax(c):
        sm[c] = softmax(c, sc.pop(c))

    def do_out(c):
        out(c, sm.pop(c))

    for c in range(min(ahead, nchunk)):
        th.append(functools.partial(do_scores, c))
    for c in range(nchunk):
        if c + ahead < nchunk:
            th.append(functools.partial(do_scores, c + ahead))
        th.append(functools.partial(do_softmax, c))
        if c >= 1:
            th.append(functools.partial(do_out, c - 1))
    th.append(functools.partial(do_out, nchunk - 1))
    return th


def _attn_body(q_ref, kvc_ref, kvp_ref, bias_ref, o_ref, kv_s, *, lq, lk, nchunk, tq, nseq):
    zpad = jnp.zeros((_score_rows(lk) - lk, 2 * A_KV_W), BF16)
    streams = []
    for n in range(nseq):
        kv_s[n, 0:WINDOW, :] = kvp_ref[n].astype(BF16)
        kv_s[n, WINDOW:WINDOW + tq, :] = kvc_ref[n].astype(BF16)

        def store_o(c, o, n=n):
            o_ref[n, c * lq:(c + 1) * lq, :] = o

        streams.append(_attn_thunks(
            nchunk, lq, lk,
            load_q=lambda c, n=n: q_ref[n, c * lq:(c + 1) * lq, :],
            load_kw=lambda c, n=n: jnp.concatenate([kv_s[n, c * lq:c * lq + lk, :], zpad], axis=0),
            bias_strip=lambda c, k, j: bias_ref[0, k, :, j:j + 128],
            store_o=store_o))
    for group in zip(*streams):
        for t in group:
            t()


def _attention_cached(qa, kv, kv_prev, bias_ext, *, lq, nseq):
    bsz, t, _ = qa.shape
    lk = WINDOW + lq
    body = functools.partial(_attn_body, lq=lq, lk=lk, nchunk=t // lq, tq=t, nseq=nseq)
    return pl.pallas_call(
        body,
        grid=(bsz // nseq,),
        in_specs=[
            pl.BlockSpec((nseq, t, A_Q_W), lambda b: (b, 0, 0)),
            pl.BlockSpec((nseq, t, 2 * A_KV_W), lambda b: (b, 0, 0)),
            pl.BlockSpec((nseq, WINDOW, 2 * A_KV_W), lambda b: (b, 0, 0)),
            _const_spec(bias_ext.shape),
        ],
        out_specs=pl.BlockSpec((nseq, t, A_Q_W), lambda b: (b, 0, 0)),
        out_shape=jax.ShapeDtypeStruct((bsz, t, A_Q_W), BF16),
        scratch_shapes=[pltpu.VMEM((nseq, WINDOW + t, 2 * A_KV_W), BF16)],
        compiler_params=pltpu.CompilerParams(
            dimension_semantics=("parallel",), vmem_limit_bytes=_VMEM_LIMIT),
        name="attention",
    )(qa, kv, kv_prev, bias_ext)


def _cumsum_rows(x, tril3):
    hi = x.astype(BF16)
    r = x - hi.astype(F32)
    mid = r.astype(BF16)
    lo = (r - mid.astype(F32)).astype(BF16)
    return _dot(tril3, jnp.concatenate([hi, mid, lo], axis=0))


def _hgrn_thunks(nchunk, blk, load, store_y, st_s, lbl, g):
    e = jnp.exp(lbl - jnp.max(lbl, axis=0, keepdims=True))
    lb = e[0:1, :] / jnp.sum(e, axis=0, keepdims=True)
    fa = 0.5 * (1.0 + lb)
    fb = 0.5 * (1.0 - lb)
    ri = lax.broadcasted_iota(jnp.int32, (blk, blk), 0)
    ci = lax.broadcasted_iota(jnp.int32, (blk, blk), 1)
    tril = (ri >= ci).astype(BF16)
    tril3 = jnp.concatenate([tril, tril, tril], axis=1)
    ri2 = lax.broadcasted_iota(jnp.int32, (blk, 2 * blk), 0)
    ci2 = lax.broadcasted_iota(jnp.int32, (blk, 2 * blk), 1)
    causal2 = ri2 >= (ci2 & (blk - 1))
    mid = blk // 2
    w = B_KEY_W
    pw = 2 * B_KEY_DIM
    npair = B_HEADS // 2
    hs = [slice(h * B_KEY_DIM, (h + 1) * B_KEY_DIM) for h in range(B_HEADS)]
    ps = [slice(j * pw, (j + 1) * pw) for j in range(npair)]

    def blockdiag(x0, x1):
        z = jnp.zeros_like(x0)
        return jnp.concatenate([jnp.concatenate([x0, z], axis=1), jnp.concatenate([z, x1], axis=1)], axis=0)

    def stage_decay(c):
        bt = fb * jnp.tanh(load(c, w, 2 * w))
        f = fa + bt
        cum = _cumsum_rows(jnp.log2(f), tril3)
        return dict(kk=fb - bt, cum=cum)

    def stage_state(c, s):
        cum = s["cum"]
        hq = load(c, 0, w)
        qs = hq + hq * jnp.tanh(hq)
        b_last = cum[blk - 1:blk, :]
        b_mid = cum[mid:mid + 1, :]
        q2f = qs * jnp.exp2(cum - b_mid)
        k2f = s["kk"] * jnp.exp2(b_mid - cum)
        q1 = (q2f * jnp.exp2(b_mid)).astype(BF16)
        k3 = (k2f * jnp.exp2(b_last - b_mid)).astype(BF16)
        q2 = q2f.astype(BF16)
        k2 = k2f.astype(BF16)
        vb = load(c, 2 * w, 3 * w).astype(BF16)
        dec = jnp.exp2(b_last)
        a = [_dot_nt(q2[:, ps[j]], blockdiag(k2[:, hs[2 * j]], k2[:, hs[2 * j + 1]])) for j in range(npair)]
        st = [st_s[h] for h in range(B_HEADS)]
        o1 = [_dot_nt(q1[:, ps[j]], blockdiag(st[2 * j].astype(BF16), st[2 * j + 1].astype(BF16)))
              for j in range(npair)]
        upd = []
        for q in range(0, B_HEADS, 4):
            vstack = jnp.concatenate([vb[:, hs[h]] for h in range(q, q + 4)], axis=0)
            kdiag = jnp.concatenate(
                [jnp.concatenate([k3[:, hs[h]] if h == h2 else jnp.zeros((blk, B_KEY_DIM), BF16)
                                  for h2 in range(q, q + 4)], axis=1) for h in range(q, q + 4)], axis=0)
            upd.append(_dot_tn(vstack, kdiag))
        for h, sl in enumerate(hs):
            st_s[h] = dec[:, sl] * st[h] + upd[h // 4][:, (h % 4) * B_KEY_DIM:(h % 4 + 1) * B_KEY_DIM]
        return dict(a=a, o1=o1, vb=vb)

    def stage_out(c, s):
        vb = s["vb"]
        am = [jnp.where(causal2, s["a"][j], 0.0).astype(BF16) for j in range(npair)]
        o2 = [_dot(am[j], blockdiag(vb[:, hs[2 * j]], vb[:, hs[2 * j + 1]])) for j in range(npair)]
        o = [s["o1"][j] + o2[j] for j in range(npair)]
        ys = [_rms(o[h // 2][:, (h % 2) * B_VAL_DIM:(h % 2 + 1) * B_VAL_DIM], g) for h in range(B_HEADS)]
        hog = load(c, 3 * w, 4 * w)
        y = jnp.concatenate(ys, axis=1) * (hog + hog * jnp.tanh(hog))
        store_y(c, y.astype(BF16))

    dec, sta, th = {}, {}, []

    def do_decay(c):
        dec[c] = stage_decay(c)

    def do_state(c):
        sta[c] = stage_state(c, dec.pop(c))

    def do_out(c):
        stage_out(c, sta.pop(c))

    th.append(functools.partial(do_decay, 0))
    for c in range(nchunk):
        if c + 1 < nchunk:
            th.append(functools.partial(do_decay, c + 1))
        th.append(functools.partial(do_state, c))
        if c >= 1:
            th.append(functools.partial(do_out, c - 1))
    th.append(functools.partial(do_out, nchunk - 1))
    return th


def _hgrn_body(hb_ref, lbl_ref, g_ref, s0_ref, yb_ref, sfin_ref, st_s, *, blk, nchunk, nseq):
    streams = []
    for q in range(nseq):
        for h in range(B_HEADS):
            st_s[q, h] = s0_ref[q, h].T

        def store_y(c, y, q=q):
            yb_ref[q, c * blk:(c + 1) * blk, :] = y

        streams.append(_hgrn_thunks(
            nchunk, blk, lambda c, lo, hi, q=q: hb_ref[q, c * blk:(c + 1) * blk, lo:hi],
            store_y, st_s.at[q], lbl_ref[...], g_ref[...]))
    for group in zip(*streams):
        for t in group:
            t()
    for q in range(nseq):
        for h in range(B_HEADS):
            sfin_ref[q, h] = st_s[q, h].T


def _hgrn_cached(hb, lb_logits, g_out, s0, *, blk, nseq):
    bsz, t, _ = hb.shape
    body = functools.partial(_hgrn_body, blk=blk, nchunk=t // blk, nseq=nseq)
    st_spec = pl.BlockSpec((nseq, B_HEADS, B_KEY_DIM, B_VAL_DIM), lambda b: (b, 0, 0, 0))
    return pl.pallas_call(
        body,
        grid=(bsz // nseq,),
        in_specs=[pl.BlockSpec((nseq, t, _HB_W), lambda b: (b, 0, 0)),
                  _const_spec(lb_logits.shape), _const_spec((1, B_VAL_DIM)), st_spec],
        out_specs=[pl.BlockSpec((nseq, t, B_VAL_W), lambda b: (b, 0, 0)), st_spec],
        out_shape=[jax.ShapeDtypeStruct((bsz, t, B_VAL_W), BF16),
                   jax.ShapeDtypeStruct((bsz, B_HEADS, B_KEY_DIM, B_VAL_DIM), F32)],
        scratch_shapes=[pltpu.VMEM((nseq, B_HEADS, B_VAL_DIM, B_KEY_DIM), F32)],
        compiler_params=pltpu.CompilerParams(
            dimension_semantics=("parallel",), vmem_limit_bytes=_VMEM_LIMIT),
        name="hgrn2",
    )(hb, lb_logits, g_out, s0)


def _inproj_body(x_ref, g_ref, w_ref, qa_ref, kv_ref, hb_ref, gg_ref):
    h = _rms(x_ref[...], g_ref[...]).astype(BF16)
    step = 512

    def mm(lo, width):
        return _dot(h, w_ref[:, lo:lo + width])

    for j in range(0, A_Q_W, step):
        qa_ref[:, j:j + step] = mm(_QA0 + j, step).astype(BF16)
    kv_ref[...] = mm(_KV0, 2 * A_KV_W)
    for j in range(0, _HB_W, step):
        hb_ref[:, j:j + step] = mm(_HB0 + j, step)
    for j in range(0, 2 * D_MODEL, step):
        gg_ref[:, j:j + step] = mm(_GG0 + j, step).astype(BF16)


def _inproj(x2d, g, w_bf, tm):
    n = x2d.shape[0]
    row = lambda w: pl.BlockSpec((tm, w), lambda i: (i, 0))
    return pl.pallas_call(
        _inproj_body,
        grid=(n // tm,),
        in_specs=[row(D_MODEL), _const_spec((1, D_MODEL)), _const_spec((D_MODEL, IN_COLS))],
        out_specs=[row(A_Q_W), row(2 * A_KV_W), row(_HB_W), row(2 * D_MODEL)],
        out_shape=[
            jax.ShapeDtypeStruct((n, A_Q_W), BF16),
            jax.ShapeDtypeStruct((n, 2 * A_KV_W), F32),
            jax.ShapeDtypeStruct((n, _HB_W), F32),
            jax.ShapeDtypeStruct((n, 2 * D_MODEL), BF16),
        ],
        compiler_params=pltpu.CompilerParams(
            dimension_semantics=("parallel",), vmem_limit_bytes=_VMEM_LIMIT),
        name="inproj",
    )(x2d, g, w_bf)


def _mixer_body(x_ref, gpre_ref, w_ref, lbl_ref, ghg_ref, bias_ref,
                kv_ref, gg_ref, ya_ref, yb_ref, sfin_ref,
                h_s, qa_s, kvb_s, kvw_s, hb_s, st_s, *, tm, tiles_per_seq):
    s = pl.program_id(0)
    cur = s % 2
    prv = 1 - cur
    tib = (s + tiles_per_seq - 1) % tiles_per_seq
    nchunk = tm // CHUNK
    lk = WINDOW + CHUNK
    piece = 256

    @pl.when(s == 0)
    def _():
        qa_s[...] = jnp.zeros(qa_s.shape, BF16)
        kvb_s[...] = jnp.zeros(kvb_s.shape, BF16)
        kvw_s[...] = jnp.zeros(kvw_s.shape, BF16)
        hb_s[...] = jnp.zeros(hb_s.shape, F32)

    @pl.when((s == 0) | (tib == 0))
    def _():
        st_s[...] = jnp.zeros(st_s.shape, F32)

    kvw_s[0:WINDOW, :] = kvw_s[tm:tm + WINDOW, :]
    kvw_s[WINDOW:WINDOW + tm, :] = kvb_s[prv]

    def dense_piece(lo, width):
        z = _dot(h_s[...], w_ref[:, lo:lo + width])
        if lo < _KV0:
            qa_s[cur, :, lo:lo + width] = z.astype(BF16)
        elif lo < _HB0:
            kv_ref[...] = z
            kvb_s[cur] = z.astype(BF16)
        elif lo < _GG0:
            hb_s[:, lo - _HB0:lo - _HB0 + width] = z
        else:
            gg_ref[:, lo - _GG0:lo - _GG0 + width] = z.astype(BF16)

    def pieces(lo, hi):
        return [functools.partial(dense_piece, c, min(piece, hi - c)) for c in range(lo, hi, piece)]

    dense_hb = pieces(_HB0, _GG0)
    dense_rest = pieces(_QA0, _KV0) + pieces(_KV0, _HB0) + pieces(_GG0, IN_COLS)

    def store_ya(c, o):
        ya_ref[c * CHUNK:(c + 1) * CHUNK, :] = o

    def store_yb(c, y):
        yb_ref[c * CHUNK:(c + 1) * CHUNK, :] = y

    zpad = jnp.zeros((_score_rows(lk) - lk, 2 * A_KV_W), BF16)

    def bias_strip(c, k, j):
        var = jnp.clip(WINDOW // CHUNK - (tib * nchunk + c), 0, WINDOW // CHUNK)
        return bias_ref[var, k, :, j:j + 128]

    attn = _attn_thunks(
        nchunk, CHUNK, lk,
        load_q=lambda c: qa_s[prv, c * CHUNK:(c + 1) * CHUNK, :],
        load_kw=lambda c: jnp.concatenate([kvw_s[c * CHUNK:c * CHUNK + lk, :], zpad], axis=0),
        bias_strip=bias_strip, store_o=store_ya)
    hgrn = _hgrn_thunks(
        nchunk, CHUNK, lambda c, lo, hi: hb_s[c * CHUNK:(c + 1) * CHUNK, lo:hi],
        store_yb, st_s, lbl_ref[...], ghg_ref[...])
    for t in attn[:2]:
        t()
    h_s[...] = _rms(x_ref[...], gpre_ref[...]).astype(BF16)
    for t in _interleave(dense_rest, hgrn) + _interleave(dense_hb, attn[2:]):
        t()

    @pl.when((tib == tiles_per_seq - 1) & (s > 0))
    def _():
        for h in range(B_HEADS):
            sfin_ref[h] = st_s[h].T


def _mixer(x2d, g_pre, w_bf, lb_logits, g_hgrn, bias_ext, *, tm, seq):
    n = x2d.shape[0]
    nt = n // tm
    tiles_per_seq = seq // tm
    dense_row = lambda w: pl.BlockSpec((tm, w), lambda s: (jnp.minimum(s, nt - 1), 0))
    lag_row = lambda w: pl.BlockSpec((tm, w), lambda s: (jnp.maximum(s - 1, 0), 0))
    body = functools.partial(_mixer_body, tm=tm, tiles_per_seq=tiles_per_seq)
    return pl.pallas_call(
        body,
        grid=(nt + 1,),
        in_specs=[dense_row(D_MODEL), _const_spec((1, D_MODEL)), _const_spec((D_MODEL, IN_COLS)),
                  _const_spec(lb_logits.shape), _const_spec((1, B_VAL_DIM)), _const_spec(bias_ext.shape)],
        out_specs=[dense_row(2 * A_KV_W), dense_row(2 * D_MODEL), lag_row(A_Q_W), lag_row(B_VAL_W),
                   pl.BlockSpec((None, B_HEADS, B_KEY_DIM, B_VAL_DIM),
                                lambda s: (jnp.maximum(s - 1, 0) // tiles_per_seq, 0, 0, 0))],
        out_shape=[jax.ShapeDtypeStruct((n, 2 * A_KV_W), F32),
                   jax.ShapeDtypeStruct((n, 2 * D_MODEL), BF16),
                   jax.ShapeDtypeStruct((n, A_Q_W), BF16),
                   jax.ShapeDtypeStruct((n, B_VAL_W), BF16),
                   jax.ShapeDtypeStruct((n // seq, B_HEADS, B_KEY_DIM, B_VAL_DIM), F32)],
        scratch_shapes=[pltpu.VMEM((tm, D_MODEL), BF16),
                        pltpu.VMEM((2, tm, A_Q_W), BF16),
                        pltpu.VMEM((2, tm, 2 * A_KV_W), BF16),
                        pltpu.VMEM((WINDOW + tm, 2 * A_KV_W), BF16),
                        pltpu.VMEM((tm, _HB_W), F32),
                        pltpu.VMEM((B_HEADS, B_VAL_DIM, B_KEY_DIM), F32)],
        compiler_params=pltpu.CompilerParams(
            dimension_semantics=("arbitrary",), vmem_limit_bytes=_VMEM_LIMIT),
        name="mixer",
    )(x2d, g_pre, w_bf, lb_logits, g_hgrn, bias_ext)


def _merge_body(ya_ref, yb_ref, gg_ref, x_ref, wa_ref, wb_ref, wo_ref, g_ref, o_ref, *, tm, strip):
    def branches(r):
        rows = slice(r * strip, (r + 1) * strip)
        ga = gg_ref[rows, 0:D_MODEL].astype(F32)
        gb = gg_ref[rows, D_MODEL:2 * D_MODEL].astype(F32)
        mix = (_sigmoid_of_twice(ga) * _dot(ya_ref[rows, :], wa_ref[...])
               + _sigmoid_of_twice(gb) * _dot(yb_ref[rows, :], wb_ref[...]))
        return mix.astype(BF16)

    def project(r, mix):
        rows = slice(r * strip, (r + 1) * strip)
        o_ref[rows, :] = x_ref[rows, :] + _rms(_dot(mix, wo_ref[...]), g_ref[...])

    nstrip = tm // strip
    mix = branches(0)
    for r in range(nstrip):
        nxt = branches(r + 1) if r + 1 < nstrip else None
        project(r, mix)
        mix = nxt


def _merge(ya, yb, gg, x2d, wa, wb, wo, g, tm):
    n = x2d.shape[0]
    row = lambda w: pl.BlockSpec((tm, w), lambda i: (i, 0))
    wspec = _const_spec((D_MODEL, D_MODEL))
    return pl.pallas_call(
        functools.partial(_merge_body, tm=tm, strip=min(tm, 256)),
        grid=(n // tm,),
        in_specs=[row(A_Q_W), row(B_VAL_W), row(2 * D_MODEL), row(D_MODEL),
                  wspec, wspec, wspec, _const_spec((1, D_MODEL))],
        out_specs=row(D_MODEL),
        out_shape=jax.ShapeDtypeStruct((n, D_MODEL), F32),
        compiler_params=pltpu.CompilerParams(
            dimension_semantics=("parallel",), vmem_limit_bytes=_VMEM_LIMIT),
        name="merge",
    )(ya, yb, gg, x2d, wa, wb, wo, g)


def _gelu_tanh(x):
    c = math.sqrt(2.0 / math.pi)
    return 0.5 * x * (1.0 + jnp.tanh(c * (x + 0.044715 * (x * x * x))))


def _ffn_up(x, gpre_ref, wup_ref):
    hf = _rms(x, gpre_ref[...]).astype(BF16)
    return _dot(hf, wup_ref[:, 0:D_FF]), _dot(hf, wup_ref[:, D_FF:2 * D_FF])


def _ffn_down(x, pe, a, u, a1, a2, wconv_ref, bconv_ref, wdown_ref, gpost_ref, wple_ref, wgate_ref):
    ac = bconv_ref[...] + a2 * wconv_ref[0:1, :] + a1 * wconv_ref[1:2, :] + a * wconv_ref[2:3, :]
    gl = (_gelu_tanh(ac) * u).astype(BF16)
    x2 = x + _rms(_dot(gl, wdown_ref[...]), gpost_ref[...])
    pex = _dot(pe.astype(BF16), wple_ref[...])
    gate = _sigmoid_of_twice(_dot(x2.astype(BF16), wgate_ref[...]))
    return x2 + pex * gate


def _ffn_tile(x, pe, shifted, gpre_ref, wup_ref, *rest):
    a, u = _ffn_up(x, gpre_ref, wup_ref)
    a1, a2 = shifted(a)
    return _ffn_down(x, pe, a, u, a1, a2, *rest), a


def _ffn_body(x_ref, pe_ref, *refs, tm, strip):
    (gpre_ref, wup_ref), rest, (o_ref, tail_ref, carry_s) = refs[:2], refs[2:8], refs[8:]
    i = pl.program_id(1)

    @pl.when(i == 0)
    def _():
        carry_s[...] = jnp.zeros((CONV_W - 1, D_FF), F32)

    row = lax.broadcasted_iota(jnp.int32, (strip, D_FF), 0)
    nstrip = tm // strip
    rows = [slice(r * strip, (r + 1) * strip) for r in range(nstrip)]
    hist = carry_s[...]
    au = _ffn_up(x_ref[rows[0], :], gpre_ref, wup_ref)
    for r in range(nstrip):
        nxt = _ffn_up(x_ref[rows[r + 1], :], gpre_ref, wup_ref) if r + 1 < nstrip else None
        a, u = au
        c0, c1 = hist[0:1, :], hist[1:2, :]
        a1 = jnp.where(row == 0, c1, pltpu.roll(a, 1, 0))
        a2 = jnp.where(row == 0, c0, jnp.where(row == 1, c1, pltpu.roll(a, 2, 0)))
        o_ref[rows[r], :] = _ffn_down(x_ref[rows[r], :], pe_ref[rows[r], :], a, u, a1, a2, *rest)
        hist = a[strip - (CONV_W - 1):strip, :]
        au = nxt
    carry_s[...] = hist
    tail_ref[...] = hist


def _ffn_cached_body(x_ref, pe_ref, prev_ref, *refs, t, nseq):
    w_refs, (o_ref, tail_ref) = refs[:8], refs[8:]
    pos = lax.broadcasted_iota(jnp.int32, (nseq * t, D_FF), 0) & (t - 1)

    def history(j):
        return jnp.concatenate([jnp.broadcast_to(prev_ref[q, j:j + 1, :], (t, D_FF)) for q in range(nseq)],
                               axis=0)

    def shifted(a):
        p0, p1 = history(0), history(1)
        a1 = jnp.where(pos == 0, p1, pltpu.roll(a, 1, 0))
        a2 = jnp.where(pos == 0, p0, jnp.where(pos == 1, p1, pltpu.roll(a, 2, 0)))
        return a1, a2

    x = x_ref[...].reshape(nseq * t, D_MODEL)
    pe = pe_ref[...].reshape(nseq * t, PLE_DIM)
    out, a = _ffn_tile(x, pe, shifted, *w_refs)
    o_ref[...] = out.reshape(nseq, t, D_MODEL)
    for q in range(nseq):
        tail_ref[q] = a[(q + 1) * t - (CONV_W - 1):(q + 1) * t, :]


def _ffn_weight_specs():
    return [_const_spec((1, D_MODEL)), _const_spec((D_MODEL, 2 * D_FF)),
            _const_spec((CONV_W, D_FF)), _const_spec((1, D_FF)),
            _const_spec((D_FF, D_MODEL)), _const_spec((1, D_MODEL)),
            _const_spec((PLE_DIM, D_MODEL)), _const_spec((D_MODEL, D_MODEL))]


def _merge_ffn_body(ya_ref, yb_ref, gg_ref, x_ref, pe_ref, wa_ref, wb_ref, wo_ref, gmix_ref, *refs, tm, strip):
    (gpre_ref, wup_ref), rest, (o_ref, tail_ref, carry_s) = refs[:2], refs[2:8], refs[8:]
    i = pl.program_id(1)

    @pl.when(i == 0)
    def _():
        carry_s[...] = jnp.zeros((CONV_W - 1, D_FF), F32)

    row = lax.broadcasted_iota(jnp.int32, (strip, D_FF), 0)
    nstrip = tm // strip
    rows = [slice(r * strip, (r + 1) * strip) for r in range(nstrip)]

    def branches(r):
        ga = gg_ref[rows[r], 0:D_MODEL].astype(F32)
        gb = gg_ref[rows[r], D_MODEL:2 * D_MODEL].astype(F32)
        mix = (_sigmoid_of_twice(ga) * _dot(ya_ref[rows[r], :], wa_ref[...])
               + _sigmoid_of_twice(gb) * _dot(yb_ref[rows[r], :], wb_ref[...]))
        return mix.astype(BF16)

    def project(r, mix):
        return x_ref[rows[r], :] + _rms(_dot(mix, wo_ref[...]), gmix_ref[...])

    hist = carry_s[...]
    mixes = {0: branches(0)}
    x1 = {0: project(0, mixes.pop(0))}
    au = {0: _ffn_up(x1[0], gpre_ref, wup_ref)}
    for r in range(nstrip):
        if r + 1 < nstrip:
            x1[r + 1] = project(r + 1, branches(r + 1))
            au[r + 1] = _ffn_up(x1[r + 1], gpre_ref, wup_ref)
        a, u = au.pop(r)
        c0, c1 = hist[0:1, :], hist[1:2, :]
        a1 = jnp.where(row == 0, c1, pltpu.roll(a, 1, 0))
        a2 = jnp.where(row == 0, c0, jnp.where(row == 1, c1, pltpu.roll(a, 2, 0)))
        o_ref[rows[r], :] = _ffn_down(x1.pop(r), pe_ref[rows[r], :], a, u, a1, a2, *rest)
        hist = a[strip - (CONV_W - 1):strip, :]
    carry_s[...] = hist
    tail_ref[...] = hist


def _merge_ffn(ya, yb, gg, x, pe, merge_weights, ffn_weights, tm):
    bsz, t, _ = x.shape
    tok = lambda w: pl.BlockSpec((None, tm, w), lambda b, i: (b, i, 0))
    wspec = _const_spec((D_MODEL, D_MODEL))
    tail_spec = pl.BlockSpec((None, CONV_W - 1, D_FF), lambda b, i: (b, 0, 0))
    return pl.pallas_call(
        functools.partial(_merge_ffn_body, tm=tm, strip=min(tm, 256)),
        grid=(bsz, t // tm),
        in_specs=[tok(A_Q_W), tok(B_VAL_W), tok(2 * D_MODEL), tok(D_MODEL), tok(PLE_DIM),
                  wspec, wspec, wspec, _const_spec((1, D_MODEL))] + _ffn_weight_specs(),
        out_specs=[tok(D_MODEL), tail_spec],
        out_shape=[jax.ShapeDtypeStruct((bsz, t, D_MODEL), F32),
                   jax.ShapeDtypeStruct((bsz, CONV_W - 1, D_FF), F32)],
        scratch_shapes=[pltpu.VMEM((CONV_W - 1, D_FF), F32)],
        compiler_params=pltpu.CompilerParams(
            dimension_semantics=("parallel", "arbitrary"), vmem_limit_bytes=_VMEM_LIMIT),
        name="merge_convffn",
    )(ya, yb, gg, x, pe, *merge_weights, *ffn_weights)


def _ffn(x, pe, weights, tm):
    bsz, t, _ = x.shape
    tail_spec = pl.BlockSpec((None, CONV_W - 1, D_FF), lambda b, i: (b, 0, 0))
    return pl.pallas_call(
        functools.partial(_ffn_body, tm=tm, strip=min(tm, 256)),
        grid=(bsz, t // tm),
        in_specs=[pl.BlockSpec((None, tm, D_MODEL), lambda b, i: (b, i, 0)),
                  pl.BlockSpec((None, tm, PLE_DIM), lambda b, i: (b, i, 0))] + _ffn_weight_specs(),
        out_specs=[pl.BlockSpec((None, tm, D_MODEL), lambda b, i: (b, i, 0)), tail_spec],
        out_shape=[jax.ShapeDtypeStruct((bsz, t, D_MODEL), F32),
                   jax.ShapeDtypeStruct((bsz, CONV_W - 1, D_FF), F32)],
        scratch_shapes=[pltpu.VMEM((CONV_W - 1, D_FF), F32)],
        compiler_params=pltpu.CompilerParams(
            dimension_semantics=("parallel", "arbitrary"), vmem_limit_bytes=_VMEM_LIMIT),
        name="convffn",
    )(x, pe, *weights)


def _ffn_cached(x, pe, prev, weights, nseq):
    bsz, t, _ = x.shape
    tail_spec = pl.BlockSpec((nseq, CONV_W - 1, D_FF), lambda b: (b, 0, 0))
    return pl.pallas_call(
        functools.partial(_ffn_cached_body, t=t, nseq=nseq),
        grid=(bsz // nseq,),
        in_specs=[pl.BlockSpec((nseq, t, D_MODEL), lambda b: (b, 0, 0)),
                  pl.BlockSpec((nseq, t, PLE_DIM), lambda b: (b, 0, 0)), tail_spec] + _ffn_weight_specs(),
        out_specs=[pl.BlockSpec((nseq, t, D_MODEL), lambda b: (b, 0, 0)), tail_spec],
        out_shape=[jax.ShapeDtypeStruct((bsz, t, D_MODEL), F32),
                   jax.ShapeDtypeStruct((bsz, CONV_W - 1, D_FF), F32)],
        compiler_params=pltpu.CompilerParams(
            dimension_semantics=("parallel",), vmem_limit_bytes=_VMEM_LIMIT),
        name="convffn_cached",
    )(x, pe, prev, *weights)


def _t5_bucket(rel):
    nb = NUM_BUCKETS // 2
    ret = jnp.where(rel > 0, nb, 0)
    n = jnp.abs(rel)
    max_exact = nb // 2
    large = max_exact + (jnp.log(jnp.maximum(n, max_exact).astype(jnp.float32) / max_exact)
                         / math.log(MAX_DISTANCE / max_exact) * (nb - max_exact)).astype(jnp.int32)
    large = jnp.minimum(large, nb - 1)
    return ret + jnp.where(n < max_exact, n, large)


def _bias_body(table_ref, sinks_ref, bk_ref, o_ref, *, nvar):
    bk = bk_ref[...]
    row = lax.broadcasted_iota(jnp.int32, bk.shape, 0)
    for k in range(A_KV_HEADS):
        acc = jnp.where(bk == -1, sinks_ref[k], NEG_INF)
        for b in range(NUM_BUCKETS):
            acc = jnp.where(bk == b, table_ref[k, b:b + 1, :], acc)
        for v in range(nvar):
            o_ref[v, k] = jnp.where(row < v * CHUNK, NEG_INF, acc)


def _bias_ext(table, sinks, lq, lk, nvar):
    rows = _score_rows(lk)
    q_pos = jnp.arange(lq) + WINDOW
    k_pos = jnp.arange(lk)
    buckets = _t5_bucket(k_pos[:, None] - q_pos[None, :]).astype(jnp.int32)
    bk = jnp.concatenate([buckets, jnp.full((1, lq), -1, jnp.int32),
                          jnp.full((rows - lk - 1, lq), -2, jnp.int32)], axis=0)
    bk = jnp.tile(bk, (1, A_GROUP))
    tab = jnp.repeat(table.astype(F32).reshape(NUM_BUCKETS, A_KV_HEADS, A_GROUP), lq, axis=2)
    tab = jnp.transpose(tab, (1, 0, 2))
    snk = jnp.repeat(sinks.astype(F32).reshape(A_KV_HEADS, 1, A_GROUP), lq, axis=2)
    vmem = pl.BlockSpec(memory_space=pltpu.VMEM)
    return pl.pallas_call(
        functools.partial(_bias_body, nvar=nvar),
        in_specs=[vmem, vmem, vmem],
        out_specs=vmem,
        out_shape=jax.ShapeDtypeStruct((nvar, A_KV_HEADS, rows, A_GROUP * lq), F32),
        name="relbias",
    )(tab, snk, bk)


def _ffn_weights(w):
    return [w[k] for k in ("g_pre_ffn", "w_up", "w_conv", "b_conv", "w_down", "g_post_ffn", "w_ple", "w_ple_gate")]


def _prompt_layer(x, pe, w, *, tm_mix, tm_tok, tm_ffn):
    bsz, t, _ = x.shape
    n = bsz * t
    x2d = x.reshape(n, D_MODEL)
    bias_ext = _bias_ext(w["rel_table"], w["sinks"], CHUNK, WINDOW + CHUNK, WINDOW // CHUNK + 1)
    kv, gg, ya, yb, s_fin = _mixer(x2d, w["g_pre_mix"], w["w_in"], w["lb_logits"], w["g_hgrn_out"],
                                   bias_ext, tm=tm_mix, seq=t)
    seq3 = lambda a: a.reshape(bsz, t, a.shape[-1])
    y, conv_tail = _merge_ffn(seq3(ya), seq3(yb), seq3(gg), x, pe,
                              [w["w_br_a"], w["w_br_b"], w["w_out"], w["g_post_mix"]], _ffn_weights(w), tm_ffn)
    return y, kv.reshape(bsz, t, 2 * A_KV_W), s_fin, conv_tail


def _sample_layer(x, pe, kv_prev, s_prev, conv_prev, w, *, tm_tok, nseq_mix):
    bsz, t, _ = x.shape
    n = bsz * t
    x2d = x.reshape(n, D_MODEL)
    qa, kv, hb, gg = _inproj(x2d, w["g_pre_mix"], w["w_in"], tm_tok)
    bias_ext = _bias_ext(w["rel_table"], w["sinks"], t, WINDOW + t, 1)
    kv3 = kv.reshape(bsz, t, 2 * A_KV_W)
    ya = _attention_cached(qa.reshape(bsz, t, A_Q_W), kv3, kv_prev, bias_ext, lq=t, nseq=nseq_mix)
    yb, s_fin = _hgrn_cached(hb.reshape(bsz, t, _HB_W), w["lb_logits"], w["g_hgrn_out"], s_prev,
                             blk=t, nseq=nseq_mix)
    x1 = _merge(ya.reshape(n, A_Q_W), yb.reshape(n, B_VAL_W), gg, x2d,
                w["w_br_a"], w["w_br_b"], w["w_out"], w["g_post_mix"], tm_tok)
    y, conv_tail = _ffn_cached(x1.reshape(bsz, t, D_MODEL), pe, conv_prev, _ffn_weights(w), tm_tok // t)
    return y, kv3, s_fin, conv_tail


def _scale_in_cols(w_in):
    h = B_KEY_W
    scale = jnp.concatenate([
        jnp.full((A_Q_W,), A_HEAD_DIM ** -0.5, F32), jnp.ones((2 * A_KV_W,), F32),
        jnp.full((2 * h,), 0.5, F32), jnp.ones((B_VAL_W,), F32), jnp.full((B_VAL_W,), 0.5, F32),
        jnp.full((2 * D_MODEL,), 0.5, F32)])
    return w_in * scale[None, :]


def kernel(x_prompt, x_sample, cache_win_k, cache_win_v, state_hgrn, cache_ffn_conv, p_prompt, p_sample,
           rel_bias_table, lb_logits, g_pre_mix, w_in, attn_sinks, g_hgrn_out, w_br_a, w_br_b, w_out,
           g_post_mix, g_pre_ffn, w_up, w_conv, b_conv, w_down, g_post_ffn, w_ple, w_ple_gate):
    bsz, seq, _ = x_prompt.shape
    dbsz, dseq, _ = x_sample.shape
    w = {
        "rel_table": rel_bias_table, "sinks": attn_sinks[0], "lb_logits": lb_logits.astype(F32),
        "g_pre_mix": g_pre_mix[0][None, :], "w_in": _scale_in_cols(w_in[0]).astype(BF16),
        "g_hgrn_out": g_hgrn_out[0][None, :],
        "w_br_a": w_br_a[0].astype(BF16), "w_br_b": w_br_b[0].astype(BF16), "w_out": w_out[0].astype(BF16),
        "g_post_mix": g_post_mix[0][None, :], "g_pre_ffn": g_pre_ffn[0][None, :],
        "w_up": w_up[0].astype(BF16), "w_conv": w_conv[0], "b_conv": b_conv[0][None, :],
        "w_down": w_down[0].astype(BF16), "g_post_ffn": g_post_ffn[0][None, :],
        "w_ple": w_ple[0].astype(BF16), "w_ple_gate": (0.5 * w_ple_gate[0]).astype(BF16),
    }
    yp, kvp, sp, cp = _prompt_layer(x_prompt, p_prompt[0], w, tm_mix=512, tm_tok=512, tm_ffn=512)
    wc = cache_win_k.shape[2]
    kv_cache = jnp.concatenate([cache_win_k[0].reshape(dbsz, wc, A_KV_W),
                                cache_win_v[0].reshape(dbsz, wc, A_KV_W)], axis=-1)
    ys, kvs, ss, cs = _sample_layer(x_sample, p_sample[0], kv_cache, state_hgrn[0], cache_ffn_conv[0], w,
                                    tm_tok=256, nseq_mix=4)
    keep = min(WINDOW, seq)

    def heads(a):
        return a.reshape(a.shape[0], a.shape[1], A_KV_HEADS, A_HEAD_DIM)[None]

    return (yp, ys,
            heads(kvp[:, seq - keep:, 0:A_KV_W]), heads(kvp[:, seq - keep:, A_KV_W:]),
            sp[None], cp[None],
            heads(kvs[:, :, 0:A_KV_W]), heads(kvs[:, :, A_KV_W:]),
            ss[None], cs[None])
```

```python
import functools
import math

import jax
import jax.numpy as jnp
from jax import lax
from jax.experimental import pallas as pl
from jax.experimental.pallas import tpu as pltpu

D_MODEL = 1024
CHUNK = 64
A_HEADS = 16
A_KV_HEADS = 2
A_HEAD_DIM = 64
A_GROUP = A_HEADS // A_KV_HEADS
WINDOW = 128
A_Q_W = A_HEADS * A_HEAD_DIM
A_KV_W = A_KV_HEADS * A_HEAD_DIM
NUM_BUCKETS = 32
MAX_DISTANCE = 128
B_HEADS = 8
B_KEY_DIM = 128
B_VAL_DIM = D_MODEL // B_HEADS
B_KEY_W = B_HEADS * B_KEY_DIM
B_VAL_W = B_HEADS * B_VAL_DIM
D_FF = 2816
CONV_W = 3
PLE_DIM = 256
EPS = 1e-6
NEG_INF = -1e30

_QA0 = 0
_KV0 = A_Q_W
_HB0 = _KV0 + 2 * A_KV_W
_GG0 = _HB0 + 2 * B_KEY_W + 2 * B_VAL_W
IN_COLS = _GG0 + 2 * D_MODEL
_HB_W = _GG0 - _HB0

_BF16_ROWS = 16
_VMEM_LIMIT = 56 * 1024 * 1024

BF16 = jnp.bfloat16
F32 = jnp.float32


def _score_rows(lk):
    return lk + _BF16_ROWS


def _const_spec(shape):
    nd = len(shape)
    return pl.BlockSpec(shape, lambda *_: (0,) * nd, pipeline_mode=pl.Buffered(1))


def _rms(x, g):
    ms = jnp.mean(x * x, axis=-1, keepdims=True)
    return x * lax.rsqrt(ms + EPS) * g


def _sigmoid_of_twice(hx):
    return 0.5 * jnp.tanh(hx) + 0.5


def _dot(a, b):
    return jnp.dot(a, b, preferred_element_type=F32)


def _dot_nt(a, b):
    return lax.dot_general(a, b, (((1,), (1,)), ((), ())), preferred_element_type=F32)


def _dot_tn(a, b):
    return lax.dot_general(a, b, (((0,), (0,)), ((), ())), preferred_element_type=F32)


def _interleave(a, b):
    out, nb = [], 0
    for i, t in enumerate(a):
        out.append(t)
        want = ((i + 1) * len(b)) // len(a)
        out.extend(b[nb:want])
        nb = want
    return out + b[nb:]


def _attn_thunks(nchunk, lq, lk, load_q, load_kw, bias_strip, store_o):
    def scores(c):
        qc = load_q(c)
        kw = load_kw(c)
        st = []
        for k in range(A_KV_HEADS):
            qs = jnp.concatenate(
                [qc[:, (k * A_GROUP + g) * A_HEAD_DIM:(k * A_GROUP + g + 1) * A_HEAD_DIM]
                 for g in range(A_GROUP)], axis=0)
            st.append(_dot_nt(kw[:, k * A_HEAD_DIM:(k + 1) * A_HEAD_DIM], qs))
        return dict(kw=kw, st=st)

    def softmax(c, s):
        ot, rden = [], []
        for k in range(A_KV_HEADS):
            ps, rs = [], []
            for j in range(0, A_GROUP * lq, 128):
                t = s["st"][k][:, j:j + 128] + bias_strip(c, k, j)
                m = jnp.max(t, axis=0, keepdims=True)
                p = jnp.exp(t - m)
                rs.append(1.0 / jnp.sum(p, axis=0, keepdims=True))
                ps.append(p.astype(BF16))
            rden.append(jnp.concatenate(rs, axis=1))
            vv = s["kw"][:, A_KV_W + k * A_HEAD_DIM:A_KV_W + (k + 1) * A_HEAD_DIM]
            ot.append(_dot_tn(vv, jnp.concatenate(ps, axis=1)))
        return dict(ot=ot, rden=rden)

    def out(c, s):
        outs = []
        for k in range(A_KV_HEADS):
            o = (s["ot"][k] * s["rden"][k]).T
            outs.append(jnp.concatenate([o[g * lq:(g + 1) * lq, :] for g in range(A_GROUP)], axis=1))
        store_o(c, jnp.concatenate(outs, axis=1).astype(BF16))

    ahead = 2
    sc, sm, th = {}, {}, []

    def do_scores(c):
        sc[c] = scores(c)

    def do_softmax(c):
        sm[c] = softmax(c, sc.pop(c))

    def do_out(c):
        out(c, sm.pop(c))

    for c in range(min(ahead, nchunk)):
        th.append(functools.partial(do_scores, c))
    for c in range(nchunk):
        if c + ahead < nchunk:
            th.append(functools.partial(do_scores, c + ahead))
        th.append(functools.partial(do_softmax, c))
        if c >= 1:
            th.append(functools.partial(do_out, c - 1))
    th.append(functools.partial(do_out, nchunk - 1))
    return th


def _attn_body(q_ref, kvc_ref, kvp_ref, bias_ref, o_ref, kv_s, *, lq, lk, nchunk, tq, nseq):
    zpad = jnp.zeros((_score_rows(lk) - lk, 2 * A_KV_W), BF16)
    streams = []
    for n in range(nseq):
        kv_s[n, 0:WINDOW, :] = kvp_ref[n].astype(BF16)
        kv_s[n, WINDOW:WINDOW + tq, :] = kvc_ref[n].astype(BF16)

        def store_o(c, o, n=n):
            o_ref[n, c * lq:(c + 1) * lq, :] = o

        streams.append(_attn_thunks(
            nchunk, lq, lk,
            load_q=lambda c, n=n: q_ref[n, c * lq:(c + 1) * lq, :],
            load_kw=lambda c, n=n: jnp.concatenate([kv_s[n, c * lq:c * lq + lk, :], zpad], axis=0),
            bias_strip=lambda c, k, j: bias_ref[0, k, :, j:j + 128],
            store_o=store_o))
    for group in zip(*streams):
        for t in group:
            t()


def _attention_cached(qa, kv, kv_prev, bias_ext, *, lq, nseq):
    bsz, t, _ = qa.shape
    lk = WINDOW + lq
    body = functools.partial(_attn_body, lq=lq, lk=lk, nchunk=t // lq, tq=t, nseq=nseq)
    return pl.pallas_call(
        body,
        grid=(bsz // nseq,),
        in_specs=[
            pl.BlockSpec((nseq, t, A_Q_W), lambda b: (b, 0, 0)),
            pl.BlockSpec((nseq, t, 2 * A_KV_W), lambda b: (b, 0, 0)),
            pl.BlockSpec((nseq, WINDOW, 2 * A_KV_W), lambda b: (b, 0, 0)),
            _const_spec(bias_ext.shape),
        ],
        out_specs=pl.BlockSpec((nseq, t, A_Q_W), lambda b: (b, 0, 0)),
        out_shape=jax.ShapeDtypeStruct((bsz, t, A_Q_W), BF16),
        scratch_shapes=[pltpu.VMEM((nseq, WINDOW + t, 2 * A_KV_W), BF16)],
        compiler_params=pltpu.CompilerParams(
            dimension_semantics=("parallel",), vmem_limit_bytes=_VMEM_LIMIT),
        name="attention",
    )(qa, kv, kv_prev, bias_ext)


def _cumsum_rows(x, tril3):
    hi = x.astype(BF16)
    r = x - hi.astype(F32)
    mid = r.astype(BF16)
    lo = (r - mid.astype(F32)).astype(BF16)
    return _dot(tril3, jnp.concatenate([hi, mid, lo], axis=0))


def _hgrn_thunks(nchunk, blk, load, store_y, st_s, lbl, g):
    e = jnp.exp(lbl - jnp.max(lbl, axis=0, keepdims=True))
    lb = e[0:1, :] / jnp.sum(e, axis=0, keepdims=True)
    fa = 0.5 * (1.0 + lb)
    fb = 0.5 * (1.0 - lb)
    ri = lax.broadcasted_iota(jnp.int32, (blk, blk), 0)
    ci = lax.broadcasted_iota(jnp.int32, (blk, blk), 1)
    tril = (ri >= ci).astype(BF16)
    tril3 = jnp.concatenate([tril, tril, tril], axis=1)
    ri2 = lax.broadcasted_iota(jnp.int32, (blk, 2 * blk), 0)
    ci2 = lax.broadcasted_iota(jnp.int32, (blk, 2 * blk), 1)
    causal2 = ri2 >= (ci2 & (blk - 1))
    mid = blk // 2
    w = B_KEY_W
    pw = 2 * B_KEY_DIM
    npair = B_HEADS // 2
    ps = [slice(j * pw, (j + 1) * pw) for j in range(npair)]

    def blockdiag(x0, x1):
        z = jnp.zeros_like(x0)
        return jnp.concatenate([jnp.concatenate([x0, z], axis=1), jnp.concatenate([z, x1], axis=1)], axis=0)

    lo0 = slice(0, B_KEY_DIM)
    lo1 = slice(B_KEY_DIM, pw)

    def stage_decay(c):
        out = []
        for j in range(npair):
            bt = fb[:, ps[j]] * jnp.tanh(load(c, w + j * pw, w + (j + 1) * pw))
            f = fa[:, ps[j]] + bt
            cum = _cumsum_rows(jnp.log2(f), tril3)
            out.append(dict(kk=fb[:, ps[j]] - bt, cum=cum))
        return out

    def stage_state(c, s):
        out = []
        for j in range(npair):
            cum = s[j]["cum"]
            hq = load(c, j * pw, (j + 1) * pw)
            qs = hq + hq * jnp.tanh(hq)
            b_last = cum[blk - 1:blk, :]
            b_mid = cum[mid:mid + 1, :]
            q2f = qs * jnp.exp2(cum - b_mid)
            k2f = s[j]["kk"] * jnp.exp2(b_mid - cum)
            q1 = (q2f * jnp.exp2(b_mid)).astype(BF16)
            k3 = (k2f * jnp.exp2(b_last - b_mid)).astype(BF16)
            q2 = q2f.astype(BF16)
            k2 = k2f.astype(BF16)
            vb = load(c, 2 * w + j * pw, 2 * w + (j + 1) * pw).astype(BF16)
            dec = jnp.exp2(b_last)
            a = _dot_nt(q2, blockdiag(k2[:, lo0], k2[:, lo1]))
            st0, st1 = st_s[2 * j], st_s[2 * j + 1]
            o1 = _dot_nt(q1, blockdiag(st0.astype(BF16), st1.astype(BF16)))
            upd = _dot_tn(jnp.concatenate([vb[:, lo0], vb[:, lo1]], axis=0),
                          blockdiag(k3[:, lo0], k3[:, lo1]))
            st_s[2 * j] = dec[:, lo0] * st0 + upd[:, lo0]
            st_s[2 * j + 1] = dec[:, lo1] * st1 + upd[:, lo1]
            out.append(dict(a=a, o1=o1, vb=vb))
        return out

    def stage_out(c, s):
        for j in range(npair):
            vb = s[j]["vb"]
            am = jnp.where(causal2, s[j]["a"], 0.0).astype(BF16)
            o = s[j]["o1"] + _dot(am, blockdiag(vb[:, lo0], vb[:, lo1]))
            y = jnp.concatenate([_rms(o[:, lo0], g), _rms(o[:, lo1], g)], axis=1)
            hog = load(c, 3 * w + j * pw, 3 * w + (j + 1) * pw)
            store_y(c, j * pw, (j + 1) * pw, (y * (hog + hog * jnp.tanh(hog))).astype(BF16))

    dec, sta, th = {}, {}, []

    def do_decay(c):
        dec[c] = stage_decay(c)

    def do_state(c):
        sta[c] = stage_state(c, dec.pop(c))

    def do_out(c):
        stage_out(c, sta.pop(c))

    th.append(functools.partial(do_decay, 0))
    for c in range(nchunk):
        if c + 1 < nchunk:
            th.append(functools.partial(do_decay, c + 1))
        th.append(functools.partial(do_state, c))
        if c >= 1:
            th.append(functools.partial(do_out, c - 1))
    th.append(functools.partial(do_out, nchunk - 1))
    return th


def _hgrn_body(hb_ref, lbl_ref, g_ref, s0_ref, yb_ref, sfin_ref, st_s, *, blk, nchunk, nseq):
    streams = []
    for q in range(nseq):
        for h in range(B_HEADS):
            st_s[q, h] = s0_ref[q, h].T

        def store_y(c, lo, hi, y, q=q):
            yb_ref[q, c * blk:(c + 1) * blk, lo:hi] = y

        streams.append(_hgrn_thunks(
            nchunk, blk, lambda c, lo, hi, q=q: hb_ref[q, c * blk:(c + 1) * blk, lo:hi],
            store_y, st_s.at[q], lbl_ref[...], g_ref[...]))
    for group in zip(*streams):
        for t in group:
            t()
    for q in range(nseq):
        for h in range(B_HEADS):
            sfin_ref[q, h] = st_s[q, h].T


def _hgrn_cached(hb, lb_logits, g_out, s0, *, blk, nseq):
    bsz, t, _ = hb.shape
    body = functools.partial(_hgrn_body, blk=blk, nchunk=t // blk, nseq=nseq)
    st_spec = pl.BlockSpec((nseq, B_HEADS, B_KEY_DIM, B_VAL_DIM), lambda b: (b, 0, 0, 0))
    return pl.pallas_call(
        body,
        grid=(bsz // nseq,),
        in_specs=[pl.BlockSpec((nseq, t, _HB_W), lambda b: (b, 0, 0)),
                  _const_spec(lb_logits.shape), _const_spec((1, B_VAL_DIM)), st_spec],
        out_specs=[pl.BlockSpec((nseq, t, B_VAL_W), lambda b: (b, 0, 0)), st_spec],
        out_shape=[jax.ShapeDtypeStruct((bsz, t, B_VAL_W), BF16),
                   jax.ShapeDtypeStruct((bsz, B_HEADS, B_KEY_DIM, B_VAL_DIM), F32)],
        scratch_shapes=[pltpu.VMEM((nseq, B_HEADS, B_VAL_DIM, B_KEY_DIM), F32)],
        compiler_params=pltpu.CompilerParams(
            dimension_semantics=("parallel",), vmem_limit_bytes=_VMEM_LIMIT),
        name="hgrn2",
    )(hb, lb_logits, g_out, s0)


def _inproj_body(x_ref, g_ref, w_ref, qa_ref, kv_ref, hb_ref, gg_ref):
    h = _rms(x_ref[...], g_ref[...]).astype(BF16)
    step = 512

    def mm(lo, width):
        return _dot(h, w_ref[:, lo:lo + width])

    for j in range(0, A_Q_W, step):
        qa_ref[:, j:j + step] = mm(_QA0 + j, step).astype(BF16)
    kv_ref[...] = mm(_KV0, 2 * A_KV_W)
    for j in range(0, _HB_W, step):
        hb_ref[:, j:j + step] = mm(_HB0 + j, step)
    for j in range(0, 2 * D_MODEL, step):
        gg_ref[:, j:j + step] = mm(_GG0 + j, step).astype(BF16)


def _inproj(x2d, g, w_bf, tm):
    n = x2d.shape[0]
    row = lambda w: pl.BlockSpec((tm, w), lambda i: (i, 0))
    return pl.pallas_call(
        _inproj_body,
        grid=(n // tm,),
        in_specs=[row(D_MODEL), _const_spec((1, D_MODEL)), _const_spec((D_MODEL, IN_COLS))],
        out_specs=[row(A_Q_W), row(2 * A_KV_W), row(_HB_W), row(2 * D_MODEL)],
        out_shape=[
            jax.ShapeDtypeStruct((n, A_Q_W), BF16),
            jax.ShapeDtypeStruct((n, 2 * A_KV_W), F32),
            jax.ShapeDtypeStruct((n, _HB_W), F32),
            jax.ShapeDtypeStruct((n, 2 * D_MODEL), BF16),
        ],
        compiler_params=pltpu.CompilerParams(
            dimension_semantics=("parallel",), vmem_limit_bytes=_VMEM_LIMIT),
        name="inproj",
    )(x2d, g, w_bf)


def _mixer_body(x_ref, gpre_ref, w_ref, lbl_ref, ghg_ref, bias_ref,
                kv_ref, gg_ref, ya_ref, yb_ref, sfin_ref,
                h_s, qa_s, kvb_s, kvw_s, hb_s, st_s, *, tm, tiles_per_seq):
    s = pl.program_id(0)
    cur = s % 2
    prv = 1 - cur
    tib = (s + tiles_per_seq - 1) % tiles_per_seq
    nchunk = tm // CHUNK
    lk = WINDOW + CHUNK
    piece = 256

    @pl.when(s == 0)
    def _():
        qa_s[...] = jnp.zeros(qa_s.shape, BF16)
        kvb_s[...] = jnp.zeros(kvb_s.shape, BF16)
        kvw_s[...] = jnp.zeros(kvw_s.shape, BF16)
        hb_s[...] = jnp.zeros(hb_s.shape, F32)

    @pl.when((s == 0) | (tib == 0))
    def _():
        st_s[...] = jnp.zeros(st_s.shape, F32)

    kvw_s[0:WINDOW, :] = kvw_s[tm:tm + WINDOW, :]
    kvw_s[WINDOW:WINDOW + tm, :] = kvb_s[prv]

    def dense_piece(lo, width):
        z = _dot(h_s[...], w_ref[:, lo:lo + width])
        if lo < _KV0:
            qa_s[cur, :, lo:lo + width] = z.astype(BF16)
        elif lo < _HB0:
            kv_ref[...] = z
            kvb_s[cur] = z.astype(BF16)
        elif lo < _GG0:
            hb_s[:, lo - _HB0:lo - _HB0 + width] = z
        else:
            gg_ref[:, lo - _GG0:lo - _GG0 + width] = z.astype(BF16)

    def pieces(lo, hi):
        return [functools.partial(dense_piece, c, min(piece, hi - c)) for c in range(lo, hi, piece)]

    dense_hb = pieces(_HB0, _GG0)
    dense_rest = pieces(_QA0, _KV0) + pieces(_KV0, _HB0) + pieces(_GG0, IN_COLS)

    def store_ya(c, o):
        ya_ref[c * CHUNK:(c + 1) * CHUNK, :] = o

    def store_yb(c, lo, hi, y):
        yb_ref[c * CHUNK:(c + 1) * CHUNK, lo:hi] = y

    zpad = jnp.zeros((_score_rows(lk) - lk, 2 * A_KV_W), BF16)

    def bias_strip(c, k, j):
        var = jnp.clip(WINDOW // CHUNK - (tib * nchunk + c), 0, WINDOW // CHUNK)
        return bias_ref[var, k, :, j:j + 128]

    attn = _attn_thunks(
        nchunk, CHUNK, lk,
        load_q=lambda c: qa_s[prv, c * CHUNK:(c + 1) * CHUNK, :],
        load_kw=lambda c: jnp.concatenate([kvw_s[c * CHUNK:c * CHUNK + lk, :], zpad], axis=0),
        bias_strip=bias_strip, store_o=store_ya)
    hgrn = _hgrn_thunks(
        nchunk, CHUNK, lambda c, lo, hi: hb_s[c * CHUNK:(c + 1) * CHUNK, lo:hi],
        store_yb, st_s, lbl_ref[...], ghg_ref[...])
    for t in attn[:2]:
        t()
    h_s[...] = _rms(x_ref[...], gpre_ref[...]).astype(BF16)
    for t in _interleave(dense_rest, hgrn) + _interleave(dense_hb, attn[2:]):
        t()

    @pl.when((tib == tiles_per_seq - 1) & (s > 0))
    def _():
        for h in range(B_HEADS):
            sfin_ref[h] = st_s[h].T


def _mixer(x2d, g_pre, w_bf, lb_logits, g_hgrn, bias_ext, *, tm, seq):
    n = x2d.shape[0]
    nt = n // tm
    tiles_per_seq = seq // tm
    dense_row = lambda w: pl.BlockSpec((tm, w), lambda s: (jnp.minimum(s, nt - 1), 0))
    lag_row = lambda w: pl.BlockSpec((tm, w), lambda s: (jnp.maximum(s - 1, 0), 0))
    body = functools.partial(_mixer_body, tm=tm, tiles_per_seq=tiles_per_seq)
    return pl.pallas_call(
        body,
        grid=(nt + 1,),
        in_specs=[dense_row(D_MODEL), _const_spec((1, D_MODEL)), _const_spec((D_MODEL, IN_COLS)),
                  _const_spec(lb_logits.shape), _const_spec((1, B_VAL_DIM)), _const_spec(bias_ext.shape)],
        out_specs=[dense_row(2 * A_KV_W), dense_row(2 * D_MODEL), lag_row(A_Q_W), lag_row(B_VAL_W),
                   pl.BlockSpec((None, B_HEADS, B_KEY_DIM, B_VAL_DIM),
                                lambda s: (jnp.maximum(s - 1, 0) // tiles_per_seq, 0, 0, 0))],
        out_shape=[jax.ShapeDtypeStruct((n, 2 * A_KV_W), F32),
                   jax.ShapeDtypeStruct((n, 2 * D_MODEL), BF16),
                   jax.ShapeDtypeStruct((n, A_Q_W), BF16),
                   jax.ShapeDtypeStruct((n, B_VAL_W), BF16),
                   jax.ShapeDtypeStruct((n // seq, B_HEADS, B_KEY_DIM, B_VAL_DIM), F32)],
        scratch_shapes=[pltpu.VMEM((tm, D_MODEL), BF16),
                        pltpu.VMEM((2, tm, A_Q_W), BF16),
                        pltpu.VMEM((2, tm, 2 * A_KV_W), BF16),
                        pltpu.VMEM((WINDOW + tm, 2 * A_KV_W), BF16),
                        pltpu.VMEM((tm, _HB_W), F32),
                        pltpu.VMEM((B_HEADS, B_VAL_DIM, B_KEY_DIM), F32)],
        compiler_params=pltpu.CompilerParams(
            dimension_semantics=("arbitrary",), vmem_limit_bytes=_VMEM_LIMIT),
        name="mixer",
    )(x2d, g_pre, w_bf, lb_logits, g_hgrn, bias_ext)


def _merge_body(ya_ref, yb_ref, gg_ref, x_ref, wa_ref, wb_ref, wo_ref, g_ref, o_ref, *, tm, strip):
    def branches(r):
        rows = slice(r * strip, (r + 1) * strip)
        ga = gg_ref[rows, 0:D_MODEL].astype(F32)
        gb = gg_ref[rows, D_MODEL:2 * D_MODEL].astype(F32)
        mix = (_sigmoid_of_twice(ga) * _dot(ya_ref[rows, :], wa_ref[...])
               + _sigmoid_of_twice(gb) * _dot(yb_ref[rows, :], wb_ref[...]))
        return mix.astype(BF16)

    def project(r, mix):
        rows = slice(r * strip, (r + 1) * strip)
        o_ref[rows, :] = x_ref[rows, :] + _rms(_dot(mix, wo_ref[...]), g_ref[...])

    nstrip = tm // strip
    mix = branches(0)
    for r in range(nstrip):
        nxt = branches(r + 1) if r + 1 < nstrip else None
        project(r, mix)
        mix = nxt


def _merge(ya, yb, gg, x2d, wa, wb, wo, g, tm):
    n = x2d.shape[0]
    row = lambda w: pl.BlockSpec((tm, w), lambda i: (i, 0))
    wspec = _const_spec((D_MODEL, D_MODEL))
    return pl.pallas_call(
        functools.partial(_merge_body, tm=tm, strip=min(tm, 256)),
        grid=(n // tm,),
        in_specs=[row(A_Q_W), row(B_VAL_W), row(2 * D_MODEL), row(D_MODEL),
                  wspec, wspec, wspec, _const_spec((1, D_MODEL))],
        out_specs=row(D_MODEL),
        out_shape=jax.ShapeDtypeStruct((n, D_MODEL), F32),
        compiler_params=pltpu.CompilerParams(
            dimension_semantics=("parallel",), vmem_limit_bytes=_VMEM_LIMIT),
        name="merge",
    )(ya, yb, gg, x2d, wa, wb, wo, g)


def _gelu_tanh(x):
    c = math.sqrt(2.0 / math.pi)
    return 0.5 * x * (1.0 + jnp.tanh(c * (x + 0.044715 * (x * x * x))))


def _ffn_up(x, gpre_ref, wup_ref):
    hf = _rms(x, gpre_ref[...]).astype(BF16)
    return _dot(hf, wup_ref[:, 0:D_FF]), _dot(hf, wup_ref[:, D_FF:2 * D_FF])


def _ffn_down(x, pe, a, u, a1, a2, wconv_ref, bconv_ref, wdown_ref, gpost_ref, wple_ref, wgate_ref):
    ac = bconv_ref[...] + a2 * wconv_ref[0:1, :] + a1 * wconv_ref[1:2, :] + a * wconv_ref[2:3, :]
    gl = (_gelu_tanh(ac) * u).astype(BF16)
    x2 = x + _rms(_dot(gl, wdown_ref[...]), gpost_ref[...])
    pex = _dot(pe.astype(BF16), wple_ref[...])
    gate = _sigmoid_of_twice(_dot(x2.astype(BF16), wgate_ref[...]))
    return x2 + pex * gate


def _ffn_tile(x, pe, shifted, gpre_ref, wup_ref, *rest):
    a, u = _ffn_up(x, gpre_ref, wup_ref)
    a1, a2 = shifted(a)
    return _ffn_down(x, pe, a, u, a1, a2, *rest), a


def _ffn_body(x_ref, pe_ref, *refs, tm, strip):
    (gpre_ref, wup_ref), rest, (o_ref, tail_ref, carry_s) = refs[:2], refs[2:8], refs[8:]
    i = pl.program_id(1)

    @pl.when(i == 0)
    def _():
        carry_s[...] = jnp.zeros((CONV_W - 1, D_FF), F32)

    row = lax.broadcasted_iota(jnp.int32, (strip, D_FF), 0)
    nstrip = tm // strip
    rows = [slice(r * strip, (r + 1) * strip) for r in range(nstrip)]
    hist = carry_s[...]
    au = _ffn_up(x_ref[rows[0], :], gpre_ref, wup_ref)
    for r in range(nstrip):
        nxt = _ffn_up(x_ref[rows[r + 1], :], gpre_ref, wup_ref) if r + 1 < nstrip else None
        a, u = au
        c0, c1 = hist[0:1, :], hist[1:2, :]
        a1 = jnp.where(row == 0, c1, pltpu.roll(a, 1, 0))
        a2 = jnp.where(row == 0, c0, jnp.where(row == 1, c1, pltpu.roll(a, 2, 0)))
        o_ref[rows[r], :] = _ffn_down(x_ref[rows[r], :], pe_ref[rows[r], :], a, u, a1, a2, *rest)
        hist = a[strip - (CONV_W - 1):strip, :]
        au = nxt
    carry_s[...] = hist
    tail_ref[...] = hist


def _ffn_cached_body(x_ref, pe_ref, prev_ref, *refs, t, nseq):
    w_refs, (o_ref, tail_ref) = refs[:8], refs[8:]
    pos = lax.broadcasted_iota(jnp.int32, (nseq * t, D_FF), 0) & (t - 1)

    def history(j):
        return jnp.concatenate([jnp.broadcast_to(prev_ref[q, j:j + 1, :], (t, D_FF)) for q in range(nseq)],
                               axis=0)

    def shifted(a):
        p0, p1 = history(0), history(1)
        a1 = jnp.where(pos == 0, p1, pltpu.roll(a, 1, 0))
        a2 = jnp.where(pos == 0, p0, jnp.where(pos == 1, p1, pltpu.roll(a, 2, 0)))
        return a1, a2

    x = x_ref[...].reshape(nseq * t, D_MODEL)
    pe = pe_ref[...].reshape(nseq * t, PLE_DIM)
    out, a = _ffn_tile(x, pe, shifted, *w_refs)
    o_ref[...] = out.reshape(nseq, t, D_MODEL)
    for q in range(nseq):
        tail_ref[q] = a[(q + 1) * t - (CONV_W - 1):(q + 1) * t, :]


def _ffn_weight_specs():
    return [_const_spec((1, D_MODEL)), _const_spec((D_MODEL, 2 * D_FF)),
            _const_spec((CONV_W, D_FF)), _const_spec((1, D_FF)),
            _const_spec((D_FF, D_MODEL)), _const_spec((1, D_MODEL)),
            _const_spec((PLE_DIM, D_MODEL)), _const_spec((D_MODEL, D_MODEL))]


def _ffn(x, pe, weights, tm):
    bsz, t, _ = x.shape
    tail_spec = pl.BlockSpec((None, CONV_W - 1, D_FF), lambda b, i: (b, 0, 0))
    return pl.pallas_call(
        functools.partial(_ffn_body, tm=tm, strip=min(tm, 256)),
        grid=(bsz, t // tm),
        in_specs=[pl.BlockSpec((None, tm, D_MODEL), lambda b, i: (b, i, 0)),
                  pl.BlockSpec((None, tm, PLE_DIM), lambda b, i: (b, i, 0))] + _ffn_weight_specs(),
        out_specs=[pl.BlockSpec((None, tm, D_MODEL), lambda b, i: (b, i, 0)), tail_spec],
        out_shape=[jax.ShapeDtypeStruct((bsz, t, D_MODEL), F32),
                   jax.ShapeDtypeStruct((bsz, CONV_W - 1, D_FF), F32)],
        scratch_shapes=[pltpu.VMEM((CONV_W - 1, D_FF), F32)],
        compiler_params=pltpu.CompilerParams(
            dimension_semantics=("parallel", "arbitrary"), vmem_limit_bytes=_VMEM_LIMIT),
        name="convffn",
    )(x, pe, *weights)


def _ffn_cached(x, pe, prev, weights, nseq):
    bsz, t, _ = x.shape
    tail_spec = pl.BlockSpec((nseq, CONV_W - 1, D_FF), lambda b: (b, 0, 0))
    return pl.pallas_call(
        functools.partial(_ffn_cached_body, t=t, nseq=nseq),
        grid=(bsz // nseq,),
        in_specs=[pl.BlockSpec((nseq, t, D_MODEL), lambda b: (b, 0, 0)),
                  pl.BlockSpec((nseq, t, PLE_DIM), lambda b: (b, 0, 0)), tail_spec] + _ffn_weight_specs(),
        out_specs=[pl.BlockSpec((nseq, t, D_MODEL), lambda b: (b, 0, 0)), tail_spec],
        out_shape=[jax.ShapeDtypeStruct((bsz, t, D_MODEL), F32),
                   jax.ShapeDtypeStruct((bsz, CONV_W - 1, D_FF), F32)],
        compiler_params=pltpu.CompilerParams(
            dimension_semantics=("parallel",), vmem_limit_bytes=_VMEM_LIMIT),
        name="convffn_cached",
    )(x, pe, prev, *weights)


def _t5_bucket(rel):
    nb = NUM_BUCKETS // 2
    ret = jnp.where(rel > 0, nb, 0)
    n = jnp.abs(rel)
    max_exact = nb // 2
    large = max_exact + (jnp.log(jnp.maximum(n, max_exact).astype(jnp.float32) / max_exact)
                         / math.log(MAX_DISTANCE / max_exact) * (nb - max_exact)).astype(jnp.int32)
    large = jnp.minimum(large, nb - 1)
    return ret + jnp.where(n < max_exact, n, large)


def _bias_body(table_ref, sinks_ref, bk_ref, o_ref, *, nvar):
    bk = bk_ref[...]
    row = lax.broadcasted_iota(jnp.int32, bk.shape, 0)
    for k in range(A_KV_HEADS):
        acc = jnp.where(bk == -1, sinks_ref[k], NEG_INF)
        for b in range(NUM_BUCKETS):
            acc = jnp.where(bk == b, table_ref[k, b:b + 1, :], acc)
        for v in range(nvar):
            o_ref[v, k] = jnp.where(row < v * CHUNK, NEG_INF, acc)


def _bias_ext(table, sinks, lq, lk, nvar):
    rows = _score_rows(lk)
    q_pos = jnp.arange(lq) + WINDOW
    k_pos = jnp.arange(lk)
    buckets = _t5_bucket(k_pos[:, None] - q_pos[None, :]).astype(jnp.int32)
    bk = jnp.concatenate([buckets, jnp.full((1, lq), -1, jnp.int32),
                          jnp.full((rows - lk - 1, lq), -2, jnp.int32)], axis=0)
    bk = jnp.tile(bk, (1, A_GROUP))
    tab = jnp.repeat(table.astype(F32).reshape(NUM_BUCKETS, A_KV_HEADS, A_GROUP), lq, axis=2)
    tab = jnp.transpose(tab, (1, 0, 2))
    snk = jnp.repeat(sinks.astype(F32).reshape(A_KV_HEADS, 1, A_GROUP), lq, axis=2)
    vmem = pl.BlockSpec(memory_space=pltpu.VMEM)
    return pl.pallas_call(
        functools.partial(_bias_body, nvar=nvar),
        in_specs=[vmem, vmem, vmem],
        out_specs=vmem,
        out_shape=jax.ShapeDtypeStruct((nvar, A_KV_HEADS, rows, A_GROUP * lq), F32),
        name="relbias",
    )(tab, snk, bk)


def _ffn_weights(w):
    return [w[k] for k in ("g_pre_ffn", "w_up", "w_conv", "b_conv", "w_down", "g_post_ffn", "w_ple", "w_ple_gate")]


def _prompt_layer(x, pe, w, *, tm_mix, tm_tok, tm_ffn):
    bsz, t, _ = x.shape
    n = bsz * t
    x2d = x.reshape(n, D_MODEL)
    bias_ext = _bias_ext(w["rel_table"], w["sinks"], CHUNK, WINDOW + CHUNK, WINDOW // CHUNK + 1)
    kv, gg, ya, yb, s_fin = _mixer(x2d, w["g_pre_mix"], w["w_in"], w["lb_logits"], w["g_hgrn_out"],
                                   bias_ext, tm=tm_mix, seq=t)
    x1 = _merge(ya, yb, gg, x2d, w["w_br_a"], w["w_br_b"], w["w_out"], w["g_post_mix"], tm_tok)
    y, conv_tail = _ffn(x1.reshape(bsz, t, D_MODEL), pe, _ffn_weights(w), tm_ffn)
    return y, kv.reshape(bsz, t, 2 * A_KV_W), s_fin, conv_tail


def _sample_layer(x, pe, kv_prev, s_prev, conv_prev, w, *, tm_tok, nseq_mix):
    bsz, t, _ = x.shape
    n = bsz * t
    x2d = x.reshape(n, D_MODEL)
    qa, kv, hb, gg = _inproj(x2d, w["g_pre_mix"], w["w_in"], tm_tok)
    bias_ext = _bias_ext(w["rel_table"], w["sinks"], t, WINDOW + t, 1)
    kv3 = kv.reshape(bsz, t, 2 * A_KV_W)
    ya = _attention_cached(qa.reshape(bsz, t, A_Q_W), kv3, kv_prev, bias_ext, lq=t, nseq=nseq_mix)
    yb, s_fin = _hgrn_cached(hb.reshape(bsz, t, _HB_W), w["lb_logits"], w["g_hgrn_out"], s_prev,
                             blk=t, nseq=nseq_mix)
    x1 = _merge(ya.reshape(n, A_Q_W), yb.reshape(n, B_VAL_W), gg, x2d,
                w["w_br_a"], w["w_br_b"], w["w_out"], w["g_post_mix"], tm_tok)
    y, conv_tail = _ffn_cached(x1.reshape(bsz, t, D_MODEL), pe, conv_prev, _ffn_weights(w), tm_tok // t)
    return y, kv3, s_fin, conv_tail


def _scale_in_cols(w_in):
    h = B_KEY_W
    scale = jnp.concatenate([
        jnp.full((A_Q_W,), A_HEAD_DIM ** -0.5, F32), jnp.ones((2 * A_KV_W,), F32),
        jnp.full((2 * h,), 0.5, F32), jnp.ones((B_VAL_W,), F32), jnp.full((B_VAL_W,), 0.5, F32),
        jnp.full((2 * D_MODEL,), 0.5, F32)])
    return w_in * scale[None, :]


def kernel(x_prompt, x_sample, cache_win_k, cache_win_v, state_hgrn, cache_ffn_conv, p_prompt, p_sample,
           rel_bias_table, lb_logits, g_pre_mix, w_in, attn_sinks, g_hgrn_out, w_br_a, w_br_b, w_out,
           g_post_mix, g_pre_ffn, w_up, w_conv, b_conv, w_down, g_post_ffn, w_ple, w_ple_gate):
    bsz, seq, _ = x_prompt.shape
    dbsz, dseq, _ = x_sample.shape
    w = {
        "rel_table": rel_bias_table, "sinks": attn_sinks[0], "lb_logits": lb_logits.astype(F32),
        "g_pre_mix": g_pre_mix[0][None, :], "w_in": _scale_in_cols(w_in[0]).astype(BF16),
        "g_hgrn_out": g_hgrn_out[0][None, :],
        "w_br_a": w_br_a[0].astype(BF16), "w_br_b": w_br_b[0].astype(BF16), "w_out": w_out[0].astype(BF16),
        "g_post_mix": g_post_mix[0][None, :], "g_pre_ffn": g_pre_ffn[0][None, :],
        "w_up": w_up[0].astype(BF16), "w_conv": w_conv[0], "b_conv": b_conv[0][None, :],
        "w_down": w_down[0].astype(BF16), "g_post_ffn": g_post_ffn[0][None, :],
        "w_ple": w_ple[0].astype(BF16), "w_ple_gate": (0.5 * w_ple_gate[0]).astype(BF16),
    }
    yp, kvp, sp, cp = _prompt_layer(x_prompt, p_prompt[0], w, tm_mix=512, tm_tok=1024, tm_ffn=512)
    wc = cache_win_k.shape[2]
    kv_cache = jnp.concatenate([cache_win_k[0].reshape(dbsz, wc, A_KV_W),
                                cache_win_v[0].reshape(dbsz, wc, A_KV_W)], axis=-1)
    ys, kvs, ss, cs = _sample_layer(x_sample, p_sample[0], kv_cache, state_hgrn[0], cache_ffn_conv[0], w,
                                    tm_tok=256, nseq_mix=4)
    keep = min(WINDOW, seq)

    def heads(a):
        return a.reshape(a.shape[0], a.shape[1], A_KV_HEADS, A_HEAD_DIM)[None]

    return (yp, ys,
            heads(kvp[:, seq - keep:, 0:A_KV_W]), heads(kvp[:, seq - keep:, A_KV_W:]),
            sp[None], cp[None],
            heads(kvs[:, :, 0:A_KV_W]), heads(kvs[:, :, A_KV_W:]),
            ss[None], cs[None])
```

```python
import functools
import math

import jax
import jax.numpy as jnp
from jax import lax
from jax.experimental import pallas as pl
from jax.experimental.pallas import tpu as pltpu

D_MODEL = 1024
CHUNK = 64
A_HEADS = 16
A_KV_HEADS = 2
A_HEAD_DIM = 64
A_GROUP = A_HEADS // A_KV_HEADS
WINDOW = 128
A_Q_W = A_HEADS * A_HEAD_DIM
A_KV_W = A_KV_HEADS * A_HEAD_DIM
NUM_BUCKETS = 32
MAX_DISTANCE = 128
B_HEADS = 8
B_KEY_DIM = 128
B_VAL_DIM = D_MODEL // B_HEADS
B_KEY_W = B_HEADS * B_KEY_DIM
B_VAL_W = B_HEADS * B_VAL_DIM
D_FF = 2816
CONV_W = 3
PLE_DIM = 256
EPS = 1e-6
NEG_INF = -1e30

_QA0 = 0
_KV0 = A_Q_W
_HB0 = _KV0 + 2 * A_KV_W
_GG0 = _HB0 + 2 * B_KEY_W + 2 * B_VAL_W
IN_COLS = _GG0 + 2 * D_MODEL
_HB_W = _GG0 - _HB0

_BF16_ROWS = 16
_VMEM_LIMIT = 56 * 1024 * 1024

BF16 = jnp.bfloat16
F32 = jnp.float32


def _score_rows(lk):
    return lk + _BF16_ROWS


def _const_spec(shape):
    nd = len(shape)
    return pl.BlockSpec(shape, lambda *_: (0,) * nd, pipeline_mode=pl.Buffered(1))


def _rms(x, g):
    ms = jnp.mean(x * x, axis=-1, keepdims=True)
    return x * lax.rsqrt(ms + EPS) * g


def _sigmoid_of_twice(hx):
    return 0.5 * jnp.tanh(hx) + 0.5


def _dot(a, b):
    return jnp.dot(a, b, preferred_element_type=F32)


def _dot_nt(a, b):
    return lax.dot_general(a, b, (((1,), (1,)), ((), ())), preferred_element_type=F32)


def _dot_tn(a, b):
    return lax.dot_general(a, b, (((0,), (0,)), ((), ())), preferred_element_type=F32)


def _interleave(a, b):
    out, nb = [], 0
    for i, t in enumerate(a):
        out.append(t)
        want = ((i + 1) * len(b)) // len(a)
        out.extend(b[nb:want])
        nb = want
    return out + b[nb:]


def _attn_thunks(nchunk, lq, lk, load_q, load_kw, bias_strip, store_o):
    def scores(c):
        qc = load_q(c)
        kw = load_kw(c)
        st = []
        for k in range(A_KV_HEADS):
            qs = jnp.concatenate(
                [qc[:, (k * A_GROUP + g) * A_HEAD_DIM:(k * A_GROUP + g + 1) * A_HEAD_DIM]
                 for g in range(A_GROUP)], axis=0)
            st.append(_dot_nt(kw[:, k * A_HEAD_DIM:(k + 1) * A_HEAD_DIM], qs))
        return dict(kw=kw, st=st)

    def softmax(c, s):
        ot, rden = [], []
        for k in range(A_KV_HEADS):
            ps, rs = [], []
            for j in range(0, A_GROUP * lq, 128):
                t = s["st"][k][:, j:j + 128] + bias_strip(c, k, j)
                m = jnp.max(t, axis=0, keepdims=True)
                p = jnp.exp(t - m)
                rs.append(1.0 / jnp.sum(p, axis=0, keepdims=True))
                ps.append(p.astype(BF16))
            rden.append(jnp.concatenate(rs, axis=1))
            vv = s["kw"][:, A_KV_W + k * A_HEAD_DIM:A_KV_W + (k + 1) * A_HEAD_DIM]
            ot.append(_dot_tn(vv, jnp.concatenate(ps, axis=1)))
        return dict(ot=ot, rden=rden)

    def out(c, s):
        outs = []
        for k in range(A_KV_HEADS):
            o = (s["ot"][k] * s["rden"][k]).T
            outs.append(jnp.concatenate([o[g * lq:(g + 1) * lq, :] for g in range(A_GROUP)], axis=1))
        store_o(c, jnp.concatenate(outs, axis=1).astype(BF16))

    ahead = 2
    sc, sm, th = {}, {}, []

    def do_scores(c):
        sc[c] = scores(c)

    def do_softmax(c):
        sm[c] = softmax(c, sc.pop(c))

    def do_out(c):
        out(c, sm.pop(c))

    for c in range(min(ahead, nchunk)):
        th.append(functools.partial(do_scores, c))
    for c in range(nchunk):
        if c + ahead < nchunk:
            th.append(functools.partial(do_scores, c + ahead))
        th.append(functools.partial(do_softmax, c))
        if c >= 1:
            th.append(functools.partial(do_out, c - 1))
    th.append(functools.partial(do_out, nchunk - 1))
    return th


def _attn_body(q_ref, kvc_ref, kvp_ref, bias_ref, o_ref, kv_s, *, lq, lk, nchunk, tq, nseq):
    zpad = jnp.zeros((_score_rows(lk) - lk, 2 * A_KV_W), BF16)
    streams = []
    for n in range(nseq):
        kv_s[n, 0:WINDOW, :] = kvp_ref[n].astype(BF16)
        kv_s[n, WINDOW:WINDOW + tq, :] = kvc_ref[n].astype(BF16)

        def store_o(c, o, n=n):
            o_ref[n, c * lq:(c + 1) * lq, :] = o

        streams.append(_attn_thunks(
            nchunk, lq, lk,
            load_q=lambda c, n=n: q_ref[n, c * lq:(c + 1) * lq, :],
            load_kw=lambda c, n=n: jnp.concatenate([kv_s[n, c * lq:c * lq + lk, :], zpad], axis=0),
            bias_strip=lambda c, k, j: bias_ref[0, k, :, j:j + 128],
            store_o=store_o))
    for group in zip(*streams):
        for t in group:
            t()


def _attention_cached(qa, kv, kv_prev, bias_ext, *, lq, nseq):
    bsz, t, _ = qa.shape
    lk = WINDOW + lq
    body = functools.partial(_attn_body, lq=lq, lk=lk, nchunk=t // lq, tq=t, nseq=nseq)
    return pl.pallas_call(
        body,
        grid=(bsz // nseq,),
        in_specs=[
            pl.BlockSpec((nseq, t, A_Q_W), lambda b: (b, 0, 0)),
            pl.BlockSpec((nseq, t, 2 * A_KV_W), lambda b: (b, 0, 0)),
            pl.BlockSpec((nseq, WINDOW, 2 * A_KV_W), lambda b: (b, 0, 0)),
            _const_spec(bias_ext.shape),
        ],
        out_specs=pl.BlockSpec((nseq, t, A_Q_W), lambda b: (b, 0, 0)),
        out_shape=jax.ShapeDtypeStruct((bsz, t, A_Q_W), BF16),
        scratch_shapes=[pltpu.VMEM((nseq, WINDOW + t, 2 * A_KV_W), BF16)],
        compiler_params=pltpu.CompilerParams(
            dimension_semantics=("parallel",), vmem_limit_bytes=_VMEM_LIMIT),
        name="attention",
    )(qa, kv, kv_prev, bias_ext)


def _cumsum_rows_scan(x):
    n = x.shape[0]
    row = lax.broadcasted_iota(jnp.int32, x.shape, 0)
    s = 1
    while s < n:
        x = x + jnp.where(row >= s, pltpu.roll(x, s, 0), 0.0)
        s *= 2
    return x


def _cumsum_rows(x, tril3):
    hi = x.astype(BF16)
    r = x - hi.astype(F32)
    mid = r.astype(BF16)
    lo = (r - mid.astype(F32)).astype(BF16)
    return _dot(tril3, jnp.concatenate([hi, mid, lo], axis=0))


def _hgrn_thunks(nchunk, blk, load, store_y, st_s, lbl, g):
    e = jnp.exp(lbl - jnp.max(lbl, axis=0, keepdims=True))
    lb = e[0:1, :] / jnp.sum(e, axis=0, keepdims=True)
    fa = 0.5 * (1.0 + lb)
    fb = 0.5 * (1.0 - lb)
    ri = lax.broadcasted_iota(jnp.int32, (blk, blk), 0)
    ci = lax.broadcasted_iota(jnp.int32, (blk, blk), 1)
    tril = (ri >= ci).astype(BF16)
    tril3 = jnp.concatenate([tril, tril, tril], axis=1)
    ri2 = lax.broadcasted_iota(jnp.int32, (blk, 2 * blk), 0)
    ci2 = lax.broadcasted_iota(jnp.int32, (blk, 2 * blk), 1)
    causal2 = ri2 >= (ci2 & (blk - 1))
    mid = blk // 2
    w = B_KEY_W
    pw = 2 * B_KEY_DIM
    npair = B_HEADS // 2
    ps = [slice(j * pw, (j + 1) * pw) for j in range(npair)]

    def blockdiag(x0, x1):
        z = jnp.zeros_like(x0)
        return jnp.concatenate([jnp.concatenate([x0, z], axis=1), jnp.concatenate([z, x1], axis=1)], axis=0)

    lo0 = slice(0, B_KEY_DIM)
    lo1 = slice(B_KEY_DIM, pw)

    def stage_decay(c):
        out = []
        for j in range(npair):
            bt = fb[:, ps[j]] * jnp.tanh(load(c, w + j * pw, w + (j + 1) * pw))
            f = fa[:, ps[j]] + bt
            cum = _cumsum_rows_scan(jnp.log2(f))
            out.append(dict(kk=fb[:, ps[j]] - bt, cum=cum))
        return out

    def stage_state(c, s):
        out = []
        for j in range(npair):
            cum = s[j]["cum"]
            hq = load(c, j * pw, (j + 1) * pw)
            qs = hq + hq * jnp.tanh(hq)
            b_last = cum[blk - 1:blk, :]
            b_mid = cum[mid:mid + 1, :]
            q2f = qs * jnp.exp2(cum - b_mid)
            k2f = s[j]["kk"] * jnp.exp2(b_mid - cum)
            q1 = (q2f * jnp.exp2(b_mid)).astype(BF16)
            k3 = (k2f * jnp.exp2(b_last - b_mid)).astype(BF16)
            q2 = q2f.astype(BF16)
            k2 = k2f.astype(BF16)
            vb = load(c, 2 * w + j * pw, 2 * w + (j + 1) * pw).astype(BF16)
            dec = jnp.exp2(b_last)
            a = _dot_nt(q2, blockdiag(k2[:, lo0], k2[:, lo1]))
            st0, st1 = st_s[2 * j], st_s[2 * j + 1]
            o1 = _dot_nt(q1, blockdiag(st0.astype(BF16), st1.astype(BF16)))
            upd = _dot_tn(jnp.concatenate([vb[:, lo0], vb[:, lo1]], axis=0),
                          blockdiag(k3[:, lo0], k3[:, lo1]))
            st_s[2 * j] = dec[:, lo0] * st0 + upd[:, lo0]
            st_s[2 * j + 1] = dec[:, lo1] * st1 + upd[:, lo1]
            out.append(dict(a=a, o1=o1, vb=vb))
        return out

    def stage_out(c, s):
        for j in range(npair):
            vb = s[j]["vb"]
            am = jnp.where(causal2, s[j]["a"], 0.0).astype(BF16)
            o = s[j]["o1"] + _dot(am, blockdiag(vb[:, lo0], vb[:, lo1]))
            y = jnp.concatenate([_rms(o[:, lo0], g), _rms(o[:, lo1], g)], axis=1)
            hog = load(c, 3 * w + j * pw, 3 * w + (j + 1) * pw)
            store_y(c, j * pw, (j + 1) * pw, (y * (hog + hog * jnp.tanh(hog))).astype(BF16))

    dec, sta, th = {}, {}, []

    def do_decay(c):
        dec[c] = stage_decay(c)

    def do_state(c):
        sta[c] = stage_state(c, dec.pop(c))

    def do_out(c):
        stage_out(c, sta.pop(c))

    th.append(functools.partial(do_decay, 0))
    for c in range(nchunk):
        if c + 1 < nchunk:
            th.append(functools.partial(do_decay, c + 1))
        th.append(functools.partial(do_state, c))
        if c >= 1:
            th.append(functools.partial(do_out, c - 1))
    th.append(functools.partial(do_out, nchunk - 1))
    return th


def _hgrn_body(hb_ref, lbl_ref, g_ref, s0_ref, yb_ref, sfin_ref, st_s, *, blk, nchunk, nseq):
    streams = []
    for q in range(nseq):
        for h in range(B_HEADS):
            st_s[q, h] = s0_ref[q, h].T

        def store_y(c, lo, hi, y, q=q):
            yb_ref[q, c * blk:(c + 1) * blk, lo:hi] = y

        streams.append(_hgrn_thunks(
            nchunk, blk, lambda c, lo, hi, q=q: hb_ref[q, c * blk:(c + 1) * blk, lo:hi],
            store_y, st_s.at[q], lbl_ref[...], g_ref[...]))
    for group in zip(*streams):
        for t in group:
            t()
    for q in range(nseq):
        for h in range(B_HEADS):
            sfin_ref[q, h] = st_s[q, h].T


def _hgrn_cached(hb, lb_logits, g_out, s0, *, blk, nseq):
    bsz, t, _ = hb.shape
    body = functools.partial(_hgrn_body, blk=blk, nchunk=t // blk, nseq=nseq)
    st_spec = pl.BlockSpec((nseq, B_HEADS, B_KEY_DIM, B_VAL_DIM), lambda b: (b, 0, 0, 0))
    return pl.pallas_call(
        body,
        grid=(bsz // nseq,),
        in_specs=[pl.BlockSpec((nseq, t, _HB_W), lambda b: (b, 0, 0)),
                  _const_spec(lb_logits.shape), _const_spec((1, B_VAL_DIM)), st_spec],
        out_specs=[pl.BlockSpec((nseq, t, B_VAL_W), lambda b: (b, 0, 0)), st_spec],
        out_shape=[jax.ShapeDtypeStruct((bsz, t, B_VAL_W), BF16),
                   jax.ShapeDtypeStruct((bsz, B_HEADS, B_KEY_DIM, B_VAL_DIM), F32)],
        scratch_shapes=[pltpu.VMEM((nseq, B_HEADS, B_VAL_DIM, B_KEY_DIM), F32)],
        compiler_params=pltpu.CompilerParams(
            dimension_semantics=("parallel",), vmem_limit_bytes=_VMEM_LIMIT),
        name="hgrn2",
    )(hb, lb_logits, g_out, s0)


def _inproj_body(x_ref, g_ref, w_ref, qa_ref, kv_ref, hb_ref, gg_ref):
    h = _rms(x_ref[...], g_ref[...]).astype(BF16)
    step = 512

    def mm(lo, width):
        return _dot(h, w_ref[:, lo:lo + width])

    for j in range(0, A_Q_W, step):
        qa_ref[:, j:j + step] = mm(_QA0 + j, step).astype(BF16)
    kv_ref[...] = mm(_KV0, 2 * A_KV_W)
    for j in range(0, _HB_W, step):
        hb_ref[:, j:j + step] = mm(_HB0 + j, step)
    for j in range(0, 2 * D_MODEL, step):
        gg_ref[:, j:j + step] = mm(_GG0 + j, step).astype(BF16)


def _inproj(x2d, g, w_bf, tm):
    n = x2d.shape[0]
    row = lambda w: pl.BlockSpec((tm, w), lambda i: (i, 0))
    return pl.pallas_call(
        _inproj_body,
        grid=(n // tm,),
        in_specs=[row(D_MODEL), _const_spec((1, D_MODEL)), _const_spec((D_MODEL, IN_COLS))],
        out_specs=[row(A_Q_W), row(2 * A_KV_W), row(_HB_W), row(2 * D_MODEL)],
        out_shape=[
            jax.ShapeDtypeStruct((n, A_Q_W), BF16),
            jax.ShapeDtypeStruct((n, 2 * A_KV_W), F32),
            jax.ShapeDtypeStruct((n, _HB_W), F32),
            jax.ShapeDtypeStruct((n, 2 * D_MODEL), BF16),
        ],
        compiler_params=pltpu.CompilerParams(
            dimension_semantics=("parallel",), vmem_limit_bytes=_VMEM_LIMIT),
        name="inproj",
    )(x2d, g, w_bf)


def _mixer_body(x_ref, gpre_ref, w_ref, lbl_ref, ghg_ref, bias_ref,
                kv_ref, gg_ref, ya_ref, yb_ref, sfin_ref,
                h_s, qa_s, kvb_s, kvw_s, hb_s, st_s, *, tm, tiles_per_seq):
    s = pl.program_id(0)
    cur = s % 2
    prv = 1 - cur
    tib = (s + tiles_per_seq - 1) % tiles_per_seq
    nchunk = tm // CHUNK
    lk = WINDOW + CHUNK
    piece = 256

    @pl.when(s == 0)
    def _():
        qa_s[...] = jnp.zeros(qa_s.shape, BF16)
        kvb_s[...] = jnp.zeros(kvb_s.shape, BF16)
        kvw_s[...] = jnp.zeros(kvw_s.shape, BF16)
        hb_s[...] = jnp.zeros(hb_s.shape, F32)

    @pl.when((s == 0) | (tib == 0))
    def _():
        st_s[...] = jnp.zeros(st_s.shape, F32)

    kvw_s[0:WINDOW, :] = kvw_s[tm:tm + WINDOW, :]
    kvw_s[WINDOW:WINDOW + tm, :] = kvb_s[prv]

    def dense_piece(lo, width):
        z = _dot(h_s[...], w_ref[:, lo:lo + width])
        if lo < _KV0:
            qa_s[cur, :, lo:lo + width] = z.astype(BF16)
        elif lo < _HB0:
            kv_ref[...] = z
            kvb_s[cur] = z.astype(BF16)
        elif lo < _GG0:
            hb_s[:, lo - _HB0:lo - _HB0 + width] = z
        else:
            gg_ref[:, lo - _GG0:lo - _GG0 + width] = z.astype(BF16)

    def pieces(lo, hi):
        return [functools.partial(dense_piece, c, min(piece, hi - c)) for c in range(lo, hi, piece)]

    dense_hb = pieces(_HB0, _GG0)
    dense_rest = pieces(_QA0, _KV0) + pieces(_KV0, _HB0) + pieces(_GG0, IN_COLS)

    def store_ya(c, o):
        ya_ref[c * CHUNK:(c + 1) * CHUNK, :] = o

    def store_yb(c, lo, hi, y):
        yb_ref[c * CHUNK:(c + 1) * CHUNK, lo:hi] = y

    zpad = jnp.zeros((_score_rows(lk) - lk, 2 * A_KV_W), BF16)

    def bias_strip(c, k, j):
        var = jnp.clip(WINDOW // CHUNK - (tib * nchunk + c), 0, WINDOW // CHUNK)
        return bias_ref[var, k, :, j:j + 128]

    attn = _attn_thunks(
        nchunk, CHUNK, lk,
        load_q=lambda c: qa_s[prv, c * CHUNK:(c + 1) * CHUNK, :],
        load_kw=lambda c: jnp.concatenate([kvw_s[c * CHUNK:c * CHUNK + lk, :], zpad], axis=0),
        bias_strip=bias_strip, store_o=store_ya)
    hgrn = _hgrn_thunks(
        nchunk, CHUNK, lambda c, lo, hi: hb_s[c * CHUNK:(c + 1) * CHUNK, lo:hi],
        store_yb, st_s, lbl_ref[...], ghg_ref[...])
    for t in attn[:2]:
        t()
    h_s[...] = _rms(x_ref[...], gpre_ref[...]).astype(BF16)
    for t in _interleave(dense_rest, hgrn) + _interleave(dense_hb, attn[2:]):
        t()

    @pl.when((tib == tiles_per_seq - 1) & (s > 0))
    def _():
        for h in range(B_HEADS):
            sfin_ref[h] = st_s[h].T


def _mixer(x2d, g_pre, w_bf, lb_logits, g_hgrn, bias_ext, *, tm, seq):
    n = x2d.shape[0]
    nt = n // tm
    tiles_per_seq = seq // tm
    dense_row = lambda w: pl.BlockSpec((tm, w), lambda s: (jnp.minimum(s, nt - 1), 0))
    lag_row = lambda w: pl.BlockSpec((tm, w), lambda s: (jnp.maximum(s - 1, 0), 0))
    body = functools.partial(_mixer_body, tm=tm, tiles_per_seq=tiles_per_seq)
    return pl.pallas_call(
        body,
        grid=(nt + 1,),
        in_specs=[dense_row(D_MODEL), _const_spec((1, D_MODEL)), _const_spec((D_MODEL, IN_COLS)),
                  _const_spec(lb_logits.shape), _const_spec((1, B_VAL_DIM)), _const_spec(bias_ext.shape)],
        out_specs=[dense_row(2 * A_KV_W), dense_row(2 * D_MODEL), lag_row(A_Q_W), lag_row(B_VAL_W),
                   pl.BlockSpec((None, B_HEADS, B_KEY_DIM, B_VAL_DIM),
                                lambda s: (jnp.maximum(s - 1, 0) // tiles_per_seq, 0, 0, 0))],
        out_shape=[jax.ShapeDtypeStruct((n, 2 * A_KV_W), F32),
                   jax.ShapeDtypeStruct((n, 2 * D_MODEL), BF16),
                   jax.ShapeDtypeStruct((n, A_Q_W), BF16),
                   jax.ShapeDtypeStruct((n, B_VAL_W), BF16),
                   jax.ShapeDtypeStruct((n // seq, B_HEADS, B_KEY_DIM, B_VAL_DIM), F32)],
        scratch_shapes=[pltpu.VMEM((tm, D_MODEL), BF16),
                        pltpu.VMEM((2, tm, A_Q_W), BF16),
                        pltpu.VMEM((2, tm, 2 * A_KV_W), BF16),
                        pltpu.VMEM((WINDOW + tm, 2 * A_KV_W), BF16),
                        pltpu.VMEM((tm, _HB_W), F32),
                        pltpu.VMEM((B_HEADS, B_VAL_DIM, B_KEY_DIM), F32)],
        compiler_params=pltpu.CompilerParams(
            dimension_semantics=("arbitrary",), vmem_limit_bytes=_VMEM_LIMIT),
        name="mixer",
    )(x2d, g_pre, w_bf, lb_logits, g_hgrn, bias_ext)


def _merge_body(ya_ref, yb_ref, gg_ref, x_ref, wa_ref, wb_ref, wo_ref, g_ref, o_ref, *, tm, strip):
    def branches(r):
        rows = slice(r * strip, (r + 1) * strip)
        ga = gg_ref[rows, 0:D_MODEL].astype(F32)
        gb = gg_ref[rows, D_MODEL:2 * D_MODEL].astype(F32)
        mix = (_sigmoid_of_twice(ga) * _dot(ya_ref[rows, :], wa_ref[...])
               + _sigmoid_of_twice(gb) * _dot(yb_ref[rows, :], wb_ref[...]))
        return mix.astype(BF16)

    def project(r, mix):
        rows = slice(r * strip, (r + 1) * strip)
        o_ref[rows, :] = x_ref[rows, :] + _rms(_dot(mix, wo_ref[...]), g_ref[...])

    nstrip = tm // strip
    mix = branches(0)
    for r in range(nstrip):
        nxt = branches(r + 1) if r + 1 < nstrip else None
        project(r, mix)
        mix = nxt


def _merge(ya, yb, gg, x2d, wa, wb, wo, g, tm):
    n = x2d.shape[0]
    row = lambda w: pl.BlockSpec((tm, w), lambda i: (i, 0))
    wspec = _const_spec((D_MODEL, D_MODEL))
    return pl.pallas_call(
        functools.partial(_merge_body, tm=tm, strip=min(tm, 256)),
        grid=(n // tm,),
        in_specs=[row(A_Q_W), row(B_VAL_W), row(2 * D_MODEL), row(D_MODEL),
                  wspec, wspec, wspec, _const_spec((1, D_MODEL))],
        out_specs=row(D_MODEL),
        out_shape=jax.ShapeDtypeStruct((n, D_MODEL), F32),
        compiler_params=pltpu.CompilerParams(
            dimension_semantics=("parallel",), vmem_limit_bytes=_VMEM_LIMIT),
        name="merge",
    )(ya, yb, gg, x2d, wa, wb, wo, g)


def _gelu_tanh(x):
    c = math.sqrt(2.0 / math.pi)
    return 0.5 * x * (1.0 + jnp.tanh(c * (x + 0.044715 * (x * x * x))))


def _ffn_up(x, gpre_ref, wup_ref):
    hf = _rms(x, gpre_ref[...]).astype(BF16)
    return _dot(hf, wup_ref[:, 0:D_FF]), _dot(hf, wup_ref[:, D_FF:2 * D_FF])


def _ffn_down(x, pe, a, u, a1, a2, wconv_ref, bconv_ref, wdown_ref, gpost_ref, wple_ref, wgate_ref):
    ac = bconv_ref[...] + a2 * wconv_ref[0:1, :] + a1 * wconv_ref[1:2, :] + a * wconv_ref[2:3, :]
    gl = (_gelu_tanh(ac) * u).astype(BF16)
    x2 = x + _rms(_dot(gl, wdown_ref[...]), gpost_ref[...])
    pex = _dot(pe.astype(BF16), wple_ref[...])
    gate = _sigmoid_of_twice(_dot(x2.astype(BF16), wgate_ref[...]))
    return x2 + pex * gate


def _ffn_tile(x, pe, shifted, gpre_ref, wup_ref, *rest):
    a, u = _ffn_up(x, gpre_ref, wup_ref)
    a1, a2 = shifted(a)
    return _ffn_down(x, pe, a, u, a1, a2, *rest), a


def _ffn_body(x_ref, pe_ref, *refs, tm, strip):
    (gpre_ref, wup_ref), rest, (o_ref, tail_ref, carry_s) = refs[:2], refs[2:8], refs[8:]
    i = pl.program_id(1)

    @pl.when(i == 0)
    def _():
        carry_s[...] = jnp.zeros((CONV_W - 1, D_FF), F32)

    row = lax.broadcasted_iota(jnp.int32, (strip, D_FF), 0)
    nstrip = tm // strip
    rows = [slice(r * strip, (r + 1) * strip) for r in range(nstrip)]
    hist = carry_s[...]
    au = _ffn_up(x_ref[rows[0], :], gpre_ref, wup_ref)
    for r in range(nstrip):
        nxt = _ffn_up(x_ref[rows[r + 1], :], gpre_ref, wup_ref) if r + 1 < nstrip else None
        a, u = au
        c0, c1 = hist[0:1, :], hist[1:2, :]
        a1 = jnp.where(row == 0, c1, pltpu.roll(a, 1, 0))
        a2 = jnp.where(row == 0, c0, jnp.where(row == 1, c1, pltpu.roll(a, 2, 0)))
        o_ref[rows[r], :] = _ffn_down(x_ref[rows[r], :], pe_ref[rows[r], :], a, u, a1, a2, *rest)
        hist = a[strip - (CONV_W - 1):strip, :]
        au = nxt
    carry_s[...] = hist
    tail_ref[...] = hist


def _ffn_cached_body(x_ref, pe_ref, prev_ref, *refs, t, nseq):
    w_refs, (o_ref, tail_ref) = refs[:8], refs[8:]
    pos = lax.broadcasted_iota(jnp.int32, (nseq * t, D_FF), 0) & (t - 1)

    def history(j):
        return jnp.concatenate([jnp.broadcast_to(prev_ref[q, j:j + 1, :], (t, D_FF)) for q in range(nseq)],
                               axis=0)

    def shifted(a):
        p0, p1 = history(0), history(1)
        a1 = jnp.where(pos == 0, p1, pltpu.roll(a, 1, 0))
        a2 = jnp.where(pos == 0, p0, jnp.where(pos == 1, p1, pltpu.roll(a, 2, 0)))
        return a1, a2

    x = x_ref[...].reshape(nseq * t, D_MODEL)
    pe = pe_ref[...].reshape(nseq * t, PLE_DIM)
    out, a = _ffn_tile(x, pe, shifted, *w_refs)
    o_ref[...] = out.reshape(nseq, t, D_MODEL)
    for q in range(nseq):
        tail_ref[q] = a[(q + 1) * t - (CONV_W - 1):(q + 1) * t, :]


def _ffn_weight_specs():
    return [_const_spec((1, D_MODEL)), _const_spec((D_MODEL, 2 * D_FF)),
            _const_spec((CONV_W, D_FF)), _const_spec((1, D_FF)),
            _const_spec((D_FF, D_MODEL)), _const_spec((1, D_MODEL)),
            _const_spec((PLE_DIM, D_MODEL)), _const_spec((D_MODEL, D_MODEL))]


def _ffn(x, pe, weights, tm):
    bsz, t, _ = x.shape
    tail_spec = pl.BlockSpec((None, CONV_W - 1, D_FF), lambda b, i: (b, 0, 0))
    return pl.pallas_call(
        functools.partial(_ffn_body, tm=tm, strip=min(tm, 256)),
        grid=(bsz, t // tm),
        in_specs=[pl.BlockSpec((None, tm, D_MODEL), lambda b, i: (b, i, 0)),
                  pl.BlockSpec((None, tm, PLE_DIM), lambda b, i: (b, i, 0))] + _ffn_weight_specs(),
        out_specs=[pl.BlockSpec((None, tm, D_MODEL), lambda b, i: (b, i, 0)), tail_spec],
        out_shape=[jax.ShapeDtypeStruct((bsz, t, D_MODEL), F32),
                   jax.ShapeDtypeStruct((bsz, CONV_W - 1, D_FF), F32)],
        scratch_shapes=[pltpu.VMEM((CONV_W - 1, D_FF), F32)],
        compiler_params=pltpu.CompilerParams(
            dimension_semantics=("parallel", "arbitrary"), vmem_limit_bytes=_VMEM_LIMIT),
        name="convffn",
    )(x, pe, *weights)


def _ffn_cached(x, pe, prev, weights, nseq):
    bsz, t, _ = x.shape
    tail_spec = pl.BlockSpec((nseq, CONV_W - 1, D_FF), lambda b: (b, 0, 0))
    return pl.pallas_call(
        functools.partial(_ffn_cached_body, t=t, nseq=nseq),
        grid=(bsz // nseq,),
        in_specs=[pl.BlockSpec((nseq, t, D_MODEL), lambda b: (b, 0, 0)),
                  pl.BlockSpec((nseq, t, PLE_DIM), lambda b: (b, 0, 0)), tail_spec] + _ffn_weight_specs(),
        out_specs=[pl.BlockSpec((nseq, t, D_MODEL), lambda b: (b, 0, 0)), tail_spec],
        out_shape=[jax.ShapeDtypeStruct((bsz, t, D_MODEL), F32),
                   jax.ShapeDtypeStruct((bsz, CONV_W - 1, D_FF), F32)],
        compiler_params=pltpu.CompilerParams(
            dimension_semantics=("parallel",), vmem_limit_bytes=_VMEM_LIMIT),
        name="convffn_cached",
    )(x, pe, prev, *weights)


def _t5_bucket(rel):
    nb = NUM_BUCKETS // 2
    ret = jnp.where(rel > 0, nb, 0)
    n = jnp.abs(rel)
    max_exact = nb // 2
    large = max_exact + (jnp.log(jnp.maximum(n, max_exact).astype(jnp.float32) / max_exact)
                         / math.log(MAX_DISTANCE / max_exact) * (nb - max_exact)).astype(jnp.int32)
    large = jnp.minimum(large, nb - 1)
    return ret + jnp.where(n < max_exact, n, large)


def _bias_body(table_ref, sinks_ref, bk_ref, o_ref, *, nvar):
    bk = bk_ref[...]
    row = lax.broadcasted_iota(jnp.int32, bk.shape, 0)
    for k in range(A_KV_HEADS):
        acc = jnp.where(bk == -1, sinks_ref[k], NEG_INF)
        for b in range(NUM_BUCKETS):
            acc = jnp.where(bk == b, table_ref[k, b:b + 1, :], acc)
        for v in range(nvar):
            o_ref[v, k] = jnp.where(row < v * CHUNK, NEG_INF, acc)


def _bias_ext(table, sinks, lq, lk, nvar):
    rows = _score_rows(lk)
    q_pos = jnp.arange(lq) + WINDOW
    k_pos = jnp.arange(lk)
    buckets = _t5_bucket(k_pos[:, None] - q_pos[None, :]).astype(jnp.int32)
    bk = jnp.concatenate([buckets, jnp.full((1, lq), -1, jnp.int32),
                          jnp.full((rows - lk - 1, lq), -2, jnp.int32)], axis=0)
    bk = jnp.tile(bk, (1, A_GROUP))
    tab = jnp.repeat(table.astype(F32).reshape(NUM_BUCKETS, A_KV_HEADS, A_GROUP), lq, axis=2)
    tab = jnp.transpose(tab, (1, 0, 2))
    snk = jnp.repeat(sinks.astype(F32).reshape(A_KV_HEADS, 1, A_GROUP), lq, axis=2)
    vmem = pl.BlockSpec(memory_space=pltpu.VMEM)
    return pl.pallas_call(
        functools.partial(_bias_body, nvar=nvar),
        in_specs=[vmem, vmem, vmem],
        out_specs=vmem,
        out_shape=jax.ShapeDtypeStruct((nvar, A_KV_HEADS, rows, A_GROUP * lq), F32),
        name="relbias",
    )(tab, snk, bk)


def _ffn_weights(w):
    return [w[k] for k in ("g_pre_ffn", "w_up", "w_conv", "b_conv", "w_down", "g_post_ffn", "w_ple", "w_ple_gate")]


def _prompt_layer(x, pe, w, *, tm_mix, tm_tok, tm_ffn):
    bsz, t, _ = x.shape
    n = bsz * t
    x2d = x.reshape(n, D_MODEL)
    bias_ext = _bias_ext(w["rel_table"], w["sinks"], CHUNK, WINDOW + CHUNK, WINDOW // CHUNK + 1)
    kv, gg, ya, yb, s_fin = _mixer(x2d, w["g_pre_mix"], w["w_in"], w["lb_logits"], w["g_hgrn_out"],
                                   bias_ext, tm=tm_mix, seq=t)
    x1 = _merge(ya, yb, gg, x2d, w["w_br_a"], w["w_br_b"], w["w_out"], w["g_post_mix"], tm_tok)
    y, conv_tail = _ffn(x1.reshape(bsz, t, D_MODEL), pe, _ffn_weights(w), tm_ffn)
    return y, kv.reshape(bsz, t, 2 * A_KV_W), s_fin, conv_tail


def _sample_layer(x, pe, kv_prev, s_prev, conv_prev, w, *, tm_tok, nseq_mix):
    bsz, t, _ = x.shape
    n = bsz * t
    x2d = x.reshape(n, D_MODEL)
    qa, kv, hb, gg = _inproj(x2d, w["g_pre_mix"], w["w_in"], tm_tok)
    bias_ext = _bias_ext(w["rel_table"], w["sinks"], t, WINDOW + t, 1)
    kv3 = kv.reshape(bsz, t, 2 * A_KV_W)
    ya = _attention_cached(qa.reshape(bsz, t, A_Q_W), kv3, kv_prev, bias_ext, lq=t, nseq=nseq_mix)
    yb, s_fin = _hgrn_cached(hb.reshape(bsz, t, _HB_W), w["lb_logits"], w["g_hgrn_out"], s_prev,
                             blk=t, nseq=nseq_mix)
    x1 = _merge(ya.reshape(n, A_Q_W), yb.reshape(n, B_VAL_W), gg, x2d,
                w["w_br_a"], w["w_br_b"], w["w_out"], w["g_post_mix"], tm_tok)
    y, conv_tail = _ffn_cached(x1.reshape(bsz, t, D_MODEL), pe, conv_prev, _ffn_weights(w), tm_tok // t)
    return y, kv3, s_fin, conv_tail


def _scale_in_cols(w_in):
    h = B_KEY_W
    scale = jnp.concatenate([
        jnp.full((A_Q_W,), A_HEAD_DIM ** -0.5, F32), jnp.ones((2 * A_KV_W,), F32),
        jnp.full((2 * h,), 0.5, F32), jnp.ones((B_VAL_W,), F32), jnp.full((B_VAL_W,), 0.5, F32),
        jnp.full((2 * D_MODEL,), 0.5, F32)])
    return w_in * scale[None, :]


def kernel(x_prompt, x_sample, cache_win_k, cache_win_v, state_hgrn, cache_ffn_conv, p_prompt, p_sample,
           rel_bias_table, lb_logits, g_pre_mix, w_in, attn_sinks, g_hgrn_out, w_br_a, w_br_b, w_out,
           g_post_mix, g_pre_ffn, w_up, w_conv, b_conv, w_down, g_post_ffn, w_ple, w_ple_gate):
    bsz, seq, _ = x_prompt.shape
    dbsz, dseq, _ = x_sample.shape
    w = {
        "rel_table": rel_bias_table, "sinks": attn_sinks[0], "lb_logits": lb_logits.astype(F32),
        "g_pre_mix": g_pre_mix[0][None, :], "w_in": _scale_in_cols(w_in[0]).astype(BF16),
        "g_hgrn_out": g_hgrn_out[0][None, :],
        "w_br_a": w_br_a[0].astype(BF16), "w_br_b": w_br_b[0].astype(BF16), "w_out": w_out[0].astype(BF16),
        "g_post_mix": g_post_mix[0][None, :], "g_pre_ffn": g_pre_ffn[0][None, :],
        "w_up": w_up[0].astype(BF16), "w_conv": w_conv[0], "b_conv": b_conv[0][None, :],
        "w_down": w_down[0].astype(BF16), "g_post_ffn": g_post_ffn[0][None, :],
        "w_ple": w_ple[0].astype(BF16), "w_ple_gate": (0.5 * w_ple_gate[0]).astype(BF16),
    }
    yp, kvp, sp, cp = _prompt_layer(x_prompt, p_prompt[0], w, tm_mix=512, tm_tok=1024, tm_ffn=512)
    wc = cache_win_k.shape[2]
    kv_cache = jnp.concatenate([cache_win_k[0].reshape(dbsz, wc, A_KV_W),
                                cache_win_v[0].reshape(dbsz, wc, A_KV_W)], axis=-1)
    ys, kvs, ss, cs = _sample_layer(x_sample, p_sample[0], kv_cache, state_hgrn[0], cache_ffn_conv[0], w,
                                    tm_tok=256, nseq_mix=4)
    keep = min(WINDOW, seq)

    def heads(a):
        return a.reshape(a.shape[0], a.shape[1], A_KV_HEADS, A_HEAD_DIM)[None]

    return (yp, ys,
            heads(kvp[:, seq - keep:, 0:A_KV_W]), heads(kvp[:, seq - keep:, A_KV_W:]),
            sp[None], cp[None],
            heads(kvs[:, :, 0:A_KV_W]), heads(kvs[:, :, A_KV_W:]),
            ss[None], cs[None])
```

```python
import functools
import math

import jax
import jax.numpy as jnp
from jax import lax
from jax.experimental import pallas as pl
from jax.experimental.pallas import tpu as pltpu

D_MODEL = 1024
CHUNK = 64
A_HEADS = 16
A_KV_HEADS = 2
A_HEAD_DIM = 64
A_GROUP = A_HEADS // A_KV_HEADS
WINDOW = 128
A_Q_W = A_HEADS * A_HEAD_DIM
A_KV_W = A_KV_HEADS * A_HEAD_DIM
NUM_BUCKETS = 32
MAX_DISTANCE = 128
B_HEADS = 8
B_KEY_DIM = 128
B_VAL_DIM = D_MODEL // B_HEADS
B_KEY_W = B_HEADS * B_KEY_DIM
B_VAL_W = B_HEADS * B_VAL_DIM
D_FF = 2816
CONV_W = 3
PLE_DIM = 256
EPS = 1e-6
NEG_INF = -1e30

_QA0 = 0
_KV0 = A_Q_W
_HB0 = _KV0 + 2 * A_KV_W
_GG0 = _HB0 + 2 * B_KEY_W + 2 * B_VAL_W
IN_COLS = _GG0 + 2 * D_MODEL
_HB_W = _GG0 - _HB0

_VMEM_LIMIT = 56 * 1024 * 1024

BF16 = jnp.bfloat16
F32 = jnp.float32


def _const_spec(shape):
    nd = len(shape)
    return pl.BlockSpec(shape, lambda *_: (0,) * nd, pipeline_mode=pl.Buffered(1))


def _rms(x, g):
    ms = jnp.mean(x * x, axis=-1, keepdims=True)
    return x * lax.rsqrt(ms + EPS) * g


def _sigmoid_of_twice(hx):
    return 0.5 * jnp.tanh(hx) + 0.5


def _dot(a, b):
    return jnp.dot(a, b, preferred_element_type=F32)


def _dot_nt(a, b):
    return lax.dot_general(a, b, (((1,), (1,)), ((), ())), preferred_element_type=F32)


def _dot_tn(a, b):
    return lax.dot_general(a, b, (((0,), (0,)), ((), ())), preferred_element_type=F32)


def _interleave(a, b):
    out, nb = [], 0
    for i, t in enumerate(a):
        out.append(t)
        want = ((i + 1) * len(b)) // len(a)
        out.extend(b[nb:want])
        nb = want
    return out + b[nb:]


def _attn_thunks(nchunk, lq, load_q, load_kw, bias_strip, sink_strip, store_o):
    def scores(c):
        qc = load_q(c)
        kw = load_kw(c)
        st = []
        for k in range(A_KV_HEADS):
            qs = jnp.concatenate(
                [qc[:, (k * A_GROUP + g) * A_HEAD_DIM:(k * A_GROUP + g + 1) * A_HEAD_DIM]
                 for g in range(A_GROUP)], axis=0)
            st.append(_dot_nt(kw[:, k * A_HEAD_DIM:(k + 1) * A_HEAD_DIM], qs))
        return dict(kw=kw, st=st)

    def softmax(c, s):
        ot, rden = [], []
        for k in range(A_KV_HEADS):
            ps, rs = [], []
            for j in range(0, A_GROUP * lq, 128):
                t = s["st"][k][:, j:j + 128] + bias_strip(c, k, j)
                sink = sink_strip(k, j)
                m = jnp.maximum(jnp.max(t, axis=0, keepdims=True), sink)
                p = jnp.exp(t - m)
                rs.append(1.0 / (jnp.sum(p, axis=0, keepdims=True) + jnp.exp(sink - m)))
                ps.append(p.astype(BF16))
            rden.append(jnp.concatenate(rs, axis=1))
            vv = s["kw"][:, A_KV_W + k * A_HEAD_DIM:A_KV_W + (k + 1) * A_HEAD_DIM]
            ot.append(_dot_tn(vv, jnp.concatenate(ps, axis=1)))
        return dict(ot=ot, rden=rden)

    def out(c, s):
        outs = []
        for k in range(A_KV_HEADS):
            o = (s["ot"][k] * s["rden"][k]).T
            outs.append(jnp.concatenate([o[g * lq:(g + 1) * lq, :] for g in range(A_GROUP)], axis=1))
        store_o(c, jnp.concatenate(outs, axis=1).astype(BF16))

    ahead = 2
    sc, sm, th = {}, {}, []

    def do_scores(c):
        sc[c] = scores(c)

    def do_softmax(c):
        sm[c] = softmax(c, sc.pop(c))

    def do_out(c):
        out(c, sm.pop(c))

    for c in range(min(ahead, nchunk)):
        th.append(functools.partial(do_scores, c))
    for c in range(nchunk):
        if c + ahead < nchunk:
            th.append(functools.partial(do_scores, c + ahead))
        th.append(functools.partial(do_softmax, c))
        if c >= 1:
            th.append(functools.partial(do_out, c - 1))
    th.append(functools.partial(do_out, nchunk - 1))
    return th


def _attn_body(q_ref, kvc_ref, kvp_ref, bias_ref, sink_ref, o_ref, kv_s, *, lq, lk, nchunk, tq, nseq):
    streams = []
    for n in range(nseq):
        kv_s[n, 0:WINDOW, :] = kvp_ref[n].astype(BF16)
        kv_s[n, WINDOW:WINDOW + tq, :] = kvc_ref[n].astype(BF16)

        def store_o(c, o, n=n):
            o_ref[n, c * lq:(c + 1) * lq, :] = o

        streams.append(_attn_thunks(
            nchunk, lq,
            load_q=lambda c, n=n: q_ref[n, c * lq:(c + 1) * lq, :],
            load_kw=lambda c, n=n: kv_s[n, c * lq:c * lq + lk, :],
            bias_strip=lambda c, k, j: bias_ref[0, k, :, j:j + 128],
            sink_strip=lambda k, j: sink_ref[k, :, j:j + 128],
            store_o=store_o))
    for group in zip(*streams):
        for t in group:
            t()


def _attention_cached(qa, kv, kv_prev, bias_ext, sink_ext, *, lq, nseq):
    bsz, t, _ = qa.shape
    lk = WINDOW + lq
    body = functools.partial(_attn_body, lq=lq, lk=lk, nchunk=t // lq, tq=t, nseq=nseq)
    return pl.pallas_call(
        body,
        grid=(bsz // nseq,),
        in_specs=[
            pl.BlockSpec((nseq, t, A_Q_W), lambda b: (b, 0, 0)),
            pl.BlockSpec((nseq, t, 2 * A_KV_W), lambda b: (b, 0, 0)),
            pl.BlockSpec((nseq, WINDOW, 2 * A_KV_W), lambda b: (b, 0, 0)),
            _const_spec(bias_ext.shape), _const_spec(sink_ext.shape),
        ],
        out_specs=pl.BlockSpec((nseq, t, A_Q_W), lambda b: (b, 0, 0)),
        out_shape=jax.ShapeDtypeStruct((bsz, t, A_Q_W), BF16),
        scratch_shapes=[pltpu.VMEM((nseq, WINDOW + t, 2 * A_KV_W), BF16)],
        compiler_params=pltpu.CompilerParams(
            dimension_semantics=("parallel",), vmem_limit_bytes=_VMEM_LIMIT),
        name="attention",
    )(qa, kv, kv_prev, bias_ext, sink_ext)


def _cumsum_rows_scan(x):
    n = x.shape[0]
    row = lax.broadcasted_iota(jnp.int32, x.shape, 0)
    s = 1
    while s < n:
        x = x + jnp.where(row >= s, pltpu.roll(x, s, 0), 0.0)
        s *= 2
    return x


def _hgrn_thunks(nchunk, blk, load, store_y, st_s, lbl, g):
    e = jnp.exp(lbl - jnp.max(lbl, axis=0, keepdims=True))
    lb = e[0:1, :] / jnp.sum(e, axis=0, keepdims=True)
    fa = 0.5 * (1.0 + lb)
    fb = 0.5 * (1.0 - lb)
    ri2 = lax.broadcasted_iota(jnp.int32, (blk, 2 * blk), 0)
    ci2 = lax.broadcasted_iota(jnp.int32, (blk, 2 * blk), 1)
    causal2 = ri2 >= (ci2 & (blk - 1))
    mid = blk // 2
    w = B_KEY_W
    pw = 2 * B_KEY_DIM
    npair = B_HEADS // 2
    ps = [slice(j * pw, (j + 1) * pw) for j in range(npair)]

    def blockdiag(x0, x1):
        z = jnp.zeros_like(x0)
        return jnp.concatenate([jnp.concatenate([x0, z], axis=1), jnp.concatenate([z, x1], axis=1)], axis=0)

    lo0 = slice(0, B_KEY_DIM)
    lo1 = slice(B_KEY_DIM, pw)

    def stage_decay(c):
        out = []
        for j in range(npair):
            bt = fb[:, ps[j]] * jnp.tanh(load(c, w + j * pw, w + (j + 1) * pw))
            f = fa[:, ps[j]] + bt
            cum = _cumsum_rows_scan(jnp.log2(f))
            out.append(dict(kk=fb[:, ps[j]] - bt, cum=cum))
        return out

    def stage_state(c, s):
        out = []
        for j in range(npair):
            cum = s[j]["cum"]
            hq = load(c, j * pw, (j + 1) * pw)
            qs = hq + hq * jnp.tanh(hq)
            b_last = cum[blk - 1:blk, :]
            b_mid = cum[mid:mid + 1, :]
            q2f = qs * jnp.exp2(cum - b_mid)
            k2f = s[j]["kk"] * jnp.exp2(b_mid - cum)
            q1 = (q2f * jnp.exp2(b_mid)).astype(BF16)
            k3 = (k2f * jnp.exp2(b_last - b_mid)).astype(BF16)
            q2 = q2f.astype(BF16)
            k2 = k2f.astype(BF16)
            vb = load(c, 2 * w + j * pw, 2 * w + (j + 1) * pw).astype(BF16)
            dec = jnp.exp2(b_last)
            a = _dot_nt(q2, blockdiag(k2[:, lo0], k2[:, lo1]))
            st0, st1 = st_s[2 * j], st_s[2 * j + 1]
            o1 = _dot_nt(q1, blockdiag(st0.astype(BF16), st1.astype(BF16)))
            upd = _dot_tn(jnp.concatenate([vb[:, lo0], vb[:, lo1]], axis=0),
                          blockdiag(k3[:, lo0], k3[:, lo1]))
            st_s[2 * j] = dec[:, lo0] * st0 + upd[:, lo0]
            st_s[2 * j + 1] = dec[:, lo1] * st1 + upd[:, lo1]
            out.append(dict(a=a, o1=o1, vb=vb))
        return out

    def stage_out(c, s):
        for j in range(npair):
            vb = s[j]["vb"]
            am = jnp.where(causal2, s[j]["a"], 0.0).astype(BF16)
            o = s[j]["o1"] + _dot(am, blockdiag(vb[:, lo0], vb[:, lo1]))
            y = jnp.concatenate([_rms(o[:, lo0], g), _rms(o[:, lo1], g)], axis=1)
            hog = load(c, 3 * w + j * pw, 3 * w + (j + 1) * pw)
            store_y(c, j * pw, (j + 1) * pw, (y * (hog + hog * jnp.tanh(hog))).astype(BF16))

    dec, sta, th = {}, {}, []

    def do_decay(c):
        dec[c] = stage_decay(c)

    def do_state(c):
        sta[c] = stage_state(c, dec.pop(c))

    def do_out(c):
        stage_out(c, sta.pop(c))

    th.append(functools.partial(do_decay, 0))
    for c in range(nchunk):
        if c + 1 < nchunk:
            th.append(functools.partial(do_decay, c + 1))
        th.append(functools.partial(do_state, c))
        if c >= 1:
            th.append(functools.partial(do_out, c - 1))
    th.append(functools.partial(do_out, nchunk - 1))
    return th


def _hgrn_body(hb_ref, lbl_ref, g_ref, s0_ref, yb_ref, sfin_ref, st_s, *, blk, nchunk, nseq):
    streams = []
    for q in range(nseq):
        for h in range(B_HEADS):
            st_s[q, h] = s0_ref[q, h].T

        def store_y(c, lo, hi, y, q=q):
            yb_ref[q, c * blk:(c + 1) * blk, lo:hi] = y

        streams.append(_hgrn_thunks(
            nchunk, blk, lambda c, lo, hi, q=q: hb_ref[q, c * blk:(c + 1) * blk, lo:hi],
            store_y, st_s.at[q], lbl_ref[...], g_ref[...]))
    for group in zip(*streams):
        for t in group:
            t()
    for q in range(nseq):
        for h in range(B_HEADS):
            sfin_ref[q, h] = st_s[q, h].T


def _hgrn_cached(hb, lb_logits, g_out, s0, *, blk, nseq):
    bsz, t, _ = hb.shape
    body = functools.partial(_hgrn_body, blk=blk, nchunk=t // blk, nseq=nseq)
    st_spec = pl.BlockSpec((nseq, B_HEADS, B_KEY_DIM, B_VAL_DIM), lambda b: (b, 0, 0, 0))
    return pl.pallas_call(
        body,
        grid=(bsz // nseq,),
        in_specs=[pl.BlockSpec((nseq, t, _HB_W), lambda b: (b, 0, 0)),
                  _const_spec(lb_logits.shape), _const_spec((1, B_VAL_DIM)), st_spec],
        out_specs=[pl.BlockSpec((nseq, t, B_VAL_W), lambda b: (b, 0, 0)), st_spec],
        out_shape=[jax.ShapeDtypeStruct((bsz, t, B_VAL_W), BF16),
                   jax.ShapeDtypeStruct((bsz, B_HEADS, B_KEY_DIM, B_VAL_DIM), F32)],
        scratch_shapes=[pltpu.VMEM((nseq, B_HEADS, B_VAL_DIM, B_KEY_DIM), F32)],
        compiler_params=pltpu.CompilerParams(
            dimension_semantics=("parallel",), vmem_limit_bytes=_VMEM_LIMIT),
        name="hgrn2",
    )(hb, lb_logits, g_out, s0)


def _inproj_body(x_ref, g_ref, w_ref, qa_ref, kv_ref, hb_ref, gg_ref):
    h = _rms(x_ref[...], g_ref[...]).astype(BF16)
    step = 512

    def mm(lo, width):
        return _dot(h, w_ref[:, lo:lo + width])

    for j in range(0, A_Q_W, step):
        qa_ref[:, j:j + step] = mm(_QA0 + j, step).astype(BF16)
    kv_ref[...] = mm(_KV0, 2 * A_KV_W)
    for j in range(0, _HB_W, step):
        hb_ref[:, j:j + step] = mm(_HB0 + j, step)
    for j in range(0, 2 * D_MODEL, step):
        gg_ref[:, j:j + step] = mm(_GG0 + j, step).astype(BF16)


def _inproj(x2d, g, w_bf, tm):
    n = x2d.shape[0]
    row = lambda w: pl.BlockSpec((tm, w), lambda i: (i, 0))
    return pl.pallas_call(
        _inproj_body,
        grid=(n // tm,),
        in_specs=[row(D_MODEL), _const_spec((1, D_MODEL)), _const_spec((D_MODEL, IN_COLS))],
        out_specs=[row(A_Q_W), row(2 * A_KV_W), row(_HB_W), row(2 * D_MODEL)],
        out_shape=[
            jax.ShapeDtypeStruct((n, A_Q_W), BF16),
            jax.ShapeDtypeStruct((n, 2 * A_KV_W), F32),
            jax.ShapeDtypeStruct((n, _HB_W), F32),
            jax.ShapeDtypeStruct((n, 2 * D_MODEL), BF16),
        ],
        compiler_params=pltpu.CompilerParams(
            dimension_semantics=("parallel",), vmem_limit_bytes=_VMEM_LIMIT),
        name="inproj",
    )(x2d, g, w_bf)


def _mixer_body(x_ref, gpre_ref, w_ref, lbl_ref, ghg_ref, bias_ref, sink_ref,
                kv_ref, gg_ref, ya_ref, yb_ref, sfin_ref,
                h_s, qa_s, kvb_s, kvw_s, hb_s, st_s, *, tm, tiles_per_seq):
    s = pl.program_id(0)
    cur = s % 2
    prv = 1 - cur
    tib = (s + tiles_per_seq - 1) % tiles_per_seq
    nchunk = tm // CHUNK
    lk = WINDOW + CHUNK
    piece = 256

    @pl.when(s == 0)
    def _():
        qa_s[...] = jnp.zeros(qa_s.shape, BF16)
        kvb_s[...] = jnp.zeros(kvb_s.shape, BF16)
        kvw_s[...] = jnp.zeros(kvw_s.shape, BF16)
        hb_s[...] = jnp.zeros(hb_s.shape, F32)

    @pl.when((s == 0) | (tib == 0))
    def _():
        st_s[...] = jnp.zeros(st_s.shape, F32)

    kvw_s[0:WINDOW, :] = kvw_s[tm:tm + WINDOW, :]
    kvw_s[WINDOW:WINDOW + tm, :] = kvb_s[prv]

    def dense_piece(lo, width):
        z = _dot(h_s[...], w_ref[:, lo:lo + width])
        if lo < _KV0:
            qa_s[cur, :, lo:lo + width] = z.astype(BF16)
        elif lo < _HB0:
            kv_ref[...] = z
            kvb_s[cur] = z.astype(BF16)
        elif lo < _GG0:
            hb_s[:, lo - _HB0:lo - _HB0 + width] = z
        else:
            gg_ref[:, lo - _GG0:lo - _GG0 + width] = z.astype(BF16)

    def pieces(lo, hi):
        return [functools.partial(dense_piece, c, min(piece, hi - c)) for c in range(lo, hi, piece)]

    dense_hb = pieces(_HB0, _GG0)
    dense_rest = pieces(_QA0, _KV0) + pieces(_KV0, _HB0) + pieces(_GG0, IN_COLS)

    def store_ya(c, o):
        ya_ref[c * CHUNK:(c + 1) * CHUNK, :] = o

    def store_yb(c, lo, hi, y):
        yb_ref[c * CHUNK:(c + 1) * CHUNK, lo:hi] = y

    def bias_strip(c, k, j):
        var = jnp.clip(WINDOW // CHUNK - (tib * nchunk + c), 0, WINDOW // CHUNK)
        return bias_ref[var, k, :, j:j + 128]

    attn = _attn_thunks(
        nchunk, CHUNK,
        load_q=lambda c: qa_s[prv, c * CHUNK:(c + 1) * CHUNK, :],
        load_kw=lambda c: kvw_s[c * CHUNK:c * CHUNK + lk, :],
        bias_strip=bias_strip, sink_strip=lambda k, j: sink_ref[k, :, j:j + 128], store_o=store_ya)
    hgrn = _hgrn_thunks(
        nchunk, CHUNK, lambda c, lo, hi: hb_s[c * CHUNK:(c + 1) * CHUNK, lo:hi],
        store_yb, st_s, lbl_ref[...], ghg_ref[...])
    for t in attn[:2]:
        t()
    h_s[...] = _rms(x_ref[...], gpre_ref[...]).astype(BF16)
    for t in _interleave(dense_rest, hgrn) + _interleave(dense_hb, attn[2:]):
        t()

    @pl.when((tib == tiles_per_seq - 1) & (s > 0))
    def _():
        for h in range(B_HEADS):
            sfin_ref[h] = st_s[h].T


def _mixer(x2d, g_pre, w_bf, lb_logits, g_hgrn, bias_ext, sink_ext, *, tm, seq):
    n = x2d.shape[0]
    nt = n // tm
    tiles_per_seq = seq // tm
    dense_row = lambda w: pl.BlockSpec((tm, w), lambda s: (jnp.minimum(s, nt - 1), 0))
    lag_row = lambda w: pl.BlockSpec((tm, w), lambda s: (jnp.maximum(s - 1, 0), 0))
    body = functools.partial(_mixer_body, tm=tm, tiles_per_seq=tiles_per_seq)
    return pl.pallas_call(
        body,
        grid=(nt + 1,),
        in_specs=[dense_row(D_MODEL), _const_spec((1, D_MODEL)), _const_spec((D_MODEL, IN_COLS)),
                  _const_spec(lb_logits.shape), _const_spec((1, B_VAL_DIM)), _const_spec(bias_ext.shape),
                  _const_spec(sink_ext.shape)],
        out_specs=[dense_row(2 * A_KV_W), dense_row(2 * D_MODEL), lag_row(A_Q_W), lag_row(B_VAL_W),
                   pl.BlockSpec((None, B_HEADS, B_KEY_DIM, B_VAL_DIM),
                                lambda s: (jnp.maximum(s - 1, 0) // tiles_per_seq, 0, 0, 0))],
        out_shape=[jax.ShapeDtypeStruct((n, 2 * A_KV_W), F32),
                   jax.ShapeDtypeStruct((n, 2 * D_MODEL), BF16),
                   jax.ShapeDtypeStruct((n, A_Q_W), BF16),
                   jax.ShapeDtypeStruct((n, B_VAL_W), BF16),
                   jax.ShapeDtypeStruct((n // seq, B_HEADS, B_KEY_DIM, B_VAL_DIM), F32)],
        scratch_shapes=[pltpu.VMEM((tm, D_MODEL), BF16),
                        pltpu.VMEM((2, tm, A_Q_W), BF16),
                        pltpu.VMEM((2, tm, 2 * A_KV_W), BF16),
                        pltpu.VMEM((WINDOW + tm, 2 * A_KV_W), BF16),
                        pltpu.VMEM((tm, _HB_W), F32),
                        pltpu.VMEM((B_HEADS, B_VAL_DIM, B_KEY_DIM), F32)],
        compiler_params=pltpu.CompilerParams(
            dimension_semantics=("arbitrary",), vmem_limit_bytes=_VMEM_LIMIT),
        name="mixer",
    )(x2d, g_pre, w_bf, lb_logits, g_hgrn, bias_ext, sink_ext)


def _merge_body(ya_ref, yb_ref, gg_ref, x_ref, wa_ref, wb_ref, wo_ref, g_ref, o_ref, *, tm, strip):
    def branches(r):
        rows = slice(r * strip, (r + 1) * strip)
        ga = gg_ref[rows, 0:D_MODEL].astype(F32)
        gb = gg_ref[rows, D_MODEL:2 * D_MODEL].astype(F32)
        mix = (_sigmoid_of_twice(ga) * _dot(ya_ref[rows, :], wa_ref[...])
               + _sigmoid_of_twice(gb) * _dot(yb_ref[rows, :], wb_ref[...]))
        return mix.astype(BF16)

    def project(r, mix):
        rows = slice(r * strip, (r + 1) * strip)
        o_ref[rows, :] = x_ref[rows, :] + _rms(_dot(mix, wo_ref[...]), g_ref[...])

    nstrip = tm // strip
    mix = branches(0)
    for r in range(nstrip):
        nxt = branches(r + 1) if r + 1 < nstrip else None
        project(r, mix)
        mix = nxt


def _merge(ya, yb, gg, x2d, wa, wb, wo, g, tm):
    n = x2d.shape[0]
    row = lambda w: pl.BlockSpec((tm, w), lambda i: (i, 0))
    wspec = _const_spec((D_MODEL, D_MODEL))
    return pl.pallas_call(
        functools.partial(_merge_body, tm=tm, strip=min(tm, 256)),
        grid=(n // tm,),
        in_specs=[row(A_Q_W), row(B_VAL_W), row(2 * D_MODEL), row(D_MODEL),
                  wspec, wspec, wspec, _const_spec((1, D_MODEL))],
        out_specs=row(D_MODEL),
        out_shape=jax.ShapeDtypeStruct((n, D_MODEL), F32),
        compiler_params=pltpu.CompilerParams(
            dimension_semantics=("parallel",), vmem_limit_bytes=_VMEM_LIMIT),
        name="merge",
    )(ya, yb, gg, x2d, wa, wb, wo, g)


def _gelu_tanh(x):
    c = math.sqrt(2.0 / math.pi)
    return 0.5 * x * (1.0 + jnp.tanh(c * (x + 0.044715 * (x * x * x))))


def _ffn_up(x, gpre_ref, wup_ref):
    hf = _rms(x, gpre_ref[...]).astype(BF16)
    return _dot(hf, wup_ref[:, 0:D_FF]), _dot(hf, wup_ref[:, D_FF:2 * D_FF])


def _ffn_down(x, pe, a, u, a1, a2, wconv_ref, bconv_ref, wdown_ref, gpost_ref, wple_ref, wgate_ref):
    ac = bconv_ref[...] + a2 * wconv_ref[0:1, :] + a1 * wconv_ref[1:2, :] + a * wconv_ref[2:3, :]
    gl = (_gelu_tanh(ac) * u).astype(BF16)
    x2 = x + _rms(_dot(gl, wdown_ref[...]), gpost_ref[...])
    pex = _dot(pe.astype(BF16), wple_ref[...])
    gate = _sigmoid_of_twice(_dot(x2.astype(BF16), wgate_ref[...]))
    return x2 + pex * gate


def _ffn_tile(x, pe, shifted, gpre_ref, wup_ref, *rest):
    a, u = _ffn_up(x, gpre_ref, wup_ref)
    a1, a2 = shifted(a)
    return _ffn_down(x, pe, a, u, a1, a2, *rest), a


def _ffn_body(x_ref, pe_ref, *refs, tm, strip):
    (gpre_ref, wup_ref), rest, (o_ref, tail_ref, carry_s) = refs[:2], refs[2:8], refs[8:]
    i = pl.program_id(1)

    @pl.when(i == 0)
    def _():
        carry_s[...] = jnp.zeros((CONV_W - 1, D_FF), F32)

    row = lax.broadcasted_iota(jnp.int32, (strip, D_FF), 0)
    nstrip = tm // strip
    rows = [slice(r * strip, (r + 1) * strip) for r in range(nstrip)]
    hist = carry_s[...]
    au = _ffn_up(x_ref[rows[0], :], gpre_ref, wup_ref)
    for r in range(nstrip):
        nxt = _ffn_up(x_ref[rows[r + 1], :], gpre_ref, wup_ref) if r + 1 < nstrip else None
        a, u = au
        c0, c1 = hist[0:1, :], hist[1:2, :]
        a1 = jnp.where(row == 0, c1, pltpu.roll(a, 1, 0))
        a2 = jnp.where(row == 0, c0, jnp.where(row == 1, c1, pltpu.roll(a, 2, 0)))
        o_ref[rows[r], :] = _ffn_down(x_ref[rows[r], :], pe_ref[rows[r], :], a, u, a1, a2, *rest)
        hist = a[strip - (CONV_W - 1):strip, :]
        au = nxt
    carry_s[...] = hist
    tail_ref[...] = hist


def _ffn_cached_body(x_ref, pe_ref, prev_ref, *refs, t, nseq):
    w_refs, (o_ref, tail_ref) = refs[:8], refs[8:]
    pos = lax.broadcasted_iota(jnp.int32, (nseq * t, D_FF), 0) & (t - 1)

    def history(j):
        return jnp.concatenate([jnp.broadcast_to(prev_ref[q, j:j + 1, :], (t, D_FF)) for q in range(nseq)],
                               axis=0)

    def shifted(a):
        p0, p1 = history(0), history(1)
        a1 = jnp.where(pos == 0, p1, pltpu.roll(a, 1, 0))
        a2 = jnp.where(pos == 0, p0, jnp.where(pos == 1, p1, pltpu.roll(a, 2, 0)))
        return a1, a2

    x = x_ref[...].reshape(nseq * t, D_MODEL)
    pe = pe_ref[...].reshape(nseq * t, PLE_DIM)
    out, a = _ffn_tile(x, pe, shifted, *w_refs)
    o_ref[...] = out.reshape(nseq, t, D_MODEL)
    for q in range(nseq):
        tail_ref[q] = a[(q + 1) * t - (CONV_W - 1):(q + 1) * t, :]


def _ffn_weight_specs():
    return [_const_spec((1, D_MODEL)), _const_spec((D_MODEL, 2 * D_FF)),
            _const_spec((CONV_W, D_FF)), _const_spec((1, D_FF)),
            _const_spec((D_FF, D_MODEL)), _const_spec((1, D_MODEL)),
            _const_spec((PLE_DIM, D_MODEL)), _const_spec((D_MODEL, D_MODEL))]


def _ffn(x, pe, weights, tm):
    bsz, t, _ = x.shape
    tail_spec = pl.BlockSpec((None, CONV_W - 1, D_FF), lambda b, i: (b, 0, 0))
    return pl.pallas_call(
        functools.partial(_ffn_body, tm=tm, strip=min(tm, 256)),
        grid=(bsz, t // tm),
        in_specs=[pl.BlockSpec((None, tm, D_MODEL), lambda b, i: (b, i, 0)),
                  pl.BlockSpec((None, tm, PLE_DIM), lambda b, i: (b, i, 0))] + _ffn_weight_specs(),
        out_specs=[pl.BlockSpec((None, tm, D_MODEL), lambda b, i: (b, i, 0)), tail_spec],
        out_shape=[jax.ShapeDtypeStruct((bsz, t, D_MODEL), F32),
                   jax.ShapeDtypeStruct((bsz, CONV_W - 1, D_FF), F32)],
        scratch_shapes=[pltpu.VMEM((CONV_W - 1, D_FF), F32)],
        compiler_params=pltpu.CompilerParams(
            dimension_semantics=("parallel", "arbitrary"), vmem_limit_bytes=_VMEM_LIMIT),
        name="convffn",
    )(x, pe, *weights)


def _ffn_cached(x, pe, prev, weights, nseq):
    bsz, t, _ = x.shape
    tail_spec = pl.BlockSpec((nseq, CONV_W - 1, D_FF), lambda b: (b, 0, 0))
    return pl.pallas_call(
        functools.partial(_ffn_cached_body, t=t, nseq=nseq),
        grid=(bsz // nseq,),
        in_specs=[pl.BlockSpec((nseq, t, D_MODEL), lambda b: (b, 0, 0)),
                  pl.BlockSpec((nseq, t, PLE_DIM), lambda b: (b, 0, 0)), tail_spec] + _ffn_weight_specs(),
        out_specs=[pl.BlockSpec((nseq, t, D_MODEL), lambda b: (b, 0, 0)), tail_spec],
        out_shape=[jax.ShapeDtypeStruct((bsz, t, D_MODEL), F32),
                   jax.ShapeDtypeStruct((bsz, CONV_W - 1, D_FF), F32)],
        compiler_params=pltpu.CompilerParams(
            dimension_semantics=("parallel",), vmem_limit_bytes=_VMEM_LIMIT),
        name="convffn_cached",
    )(x, pe, prev, *weights)


def _t5_bucket(rel):
    nb = NUM_BUCKETS // 2
    ret = jnp.where(rel > 0, nb, 0)
    n = jnp.abs(rel)
    max_exact = nb // 2
    large = max_exact + (jnp.log(jnp.maximum(n, max_exact).astype(jnp.float32) / max_exact)
                         / math.log(MAX_DISTANCE / max_exact) * (nb - max_exact)).astype(jnp.int32)
    large = jnp.minimum(large, nb - 1)
    return ret + jnp.where(n < max_exact, n, large)


def _bias_body(table_ref, bk_ref, o_ref, *, nvar):
    bk = bk_ref[...]
    row = lax.broadcasted_iota(jnp.int32, bk.shape, 0)
    for k in range(A_KV_HEADS):
        acc = jnp.zeros(bk.shape, F32)
        for b in range(NUM_BUCKETS):
            acc = jnp.where(bk == b, table_ref[k, b:b + 1, :], acc)
        for v in range(nvar):
            o_ref[v, k] = jnp.where(row < v * CHUNK, NEG_INF, acc)


def _bias_ext(table, sinks, lq, lk, nvar):
    q_pos = jnp.arange(lq) + WINDOW
    k_pos = jnp.arange(lk)
    buckets = _t5_bucket(k_pos[:, None] - q_pos[None, :]).astype(jnp.int32)
    bk = jnp.tile(buckets, (1, A_GROUP))
    tab = jnp.repeat(table.astype(F32).reshape(NUM_BUCKETS, A_KV_HEADS, A_GROUP), lq, axis=2)
    tab = jnp.transpose(tab, (1, 0, 2))
    snk = jnp.repeat(sinks.astype(F32).reshape(A_KV_HEADS, 1, A_GROUP), lq, axis=2)
    vmem = pl.BlockSpec(memory_space=pltpu.VMEM)
    bias = pl.pallas_call(
        functools.partial(_bias_body, nvar=nvar),
        in_specs=[vmem, vmem],
        out_specs=vmem,
        out_shape=jax.ShapeDtypeStruct((nvar, A_KV_HEADS, lk, A_GROUP * lq), F32),
        name="relbias",
    )(tab, bk)
    return bias, snk


def _ffn_weights(w):
    return [w[k] for k in ("g_pre_ffn", "w_up", "w_conv", "b_conv", "w_down", "g_post_ffn", "w_ple", "w_ple_gate")]


def _prompt_layer(x, pe, w, *, tm_mix, tm_tok, tm_ffn):
    bsz, t, _ = x.shape
    n = bsz * t
    x2d = x.reshape(n, D_MODEL)
    bias_ext, sink_ext = _bias_ext(w["rel_table"], w["sinks"], CHUNK, WINDOW + CHUNK, WINDOW // CHUNK + 1)
    kv, gg, ya, yb, s_fin = _mixer(x2d, w["g_pre_mix"], w["w_in"], w["lb_logits"], w["g_hgrn_out"],
                                   bias_ext, sink_ext, tm=tm_mix, seq=t)
    x1 = _merge(ya, yb, gg, x2d, w["w_br_a"], w["w_br_b"], w["w_out"], w["g_post_mix"], tm_tok)
    y, conv_tail = _ffn(x1.reshape(bsz, t, D_MODEL), pe, _ffn_weights(w), tm_ffn)
    return y, kv.reshape(bsz, t, 2 * A_KV_W), s_fin, conv_tail


def _sample_layer(x, pe, kv_prev, s_prev, conv_prev, w, *, tm_tok, nseq_mix):
    bsz, t, _ = x.shape
    n = bsz * t
    x2d = x.reshape(n, D_MODEL)
    qa, kv, hb, gg = _inproj(x2d, w["g_pre_mix"], w["w_in"], tm_tok)
    bias_ext, sink_ext = _bias_ext(w["rel_table"], w["sinks"], t, WINDOW + t, 1)
    kv3 = kv.reshape(bsz, t, 2 * A_KV_W)
    ya = _attention_cached(qa.reshape(bsz, t, A_Q_W), kv3, kv_prev, bias_ext, sink_ext, lq=t,
                           nseq=nseq_mix)
    yb, s_fin = _hgrn_cached(hb.reshape(bsz, t, _HB_W), w["lb_logits"], w["g_hgrn_out"], s_prev,
                             blk=t, nseq=nseq_mix)
    x1 = _merge(ya.reshape(n, A_Q_W), yb.reshape(n, B_VAL_W), gg, x2d,
                w["w_br_a"], w["w_br_b"], w["w_out"], w["g_post_mix"], tm_tok)
    y, conv_tail = _ffn_cached(x1.reshape(bsz, t, D_MODEL), pe, conv_prev, _ffn_weights(w), tm_tok // t)
    return y, kv3, s_fin, conv_tail


def _scale_in_cols(w_in):
    h = B_KEY_W
    scale = jnp.concatenate([
        jnp.full((A_Q_W,), A_HEAD_DIM ** -0.5, F32), jnp.ones((2 * A_KV_W,), F32),
        jnp.full((2 * h,), 0.5, F32), jnp.ones((B_VAL_W,), F32), jnp.full((B_VAL_W,), 0.5, F32),
        jnp.full((2 * D_MODEL,), 0.5, F32)])
    return w_in * scale[None, :]


def kernel(x_prompt, x_sample, cache_win_k, cache_win_v, state_hgrn, cache_ffn_conv, p_prompt, p_sample,
           rel_bias_table, lb_logits, g_pre_mix, w_in, attn_sinks, g_hgrn_out, w_br_a, w_br_b, w_out,
           g_post_mix, g_pre_ffn, w_up, w_conv, b_conv, w_down, g_post_ffn, w_ple, w_ple_gate):
    bsz, seq, _ = x_prompt.shape
    dbsz, dseq, _ = x_sample.shape
    w = {
        "rel_table": rel_bias_table, "sinks": attn_sinks[0], "lb_logits": lb_logits.astype(F32),
        "g_pre_mix": g_pre_mix[0][None, :], "w_in": _scale_in_cols(w_in[0]).astype(BF16),
        "g_hgrn_out": g_hgrn_out[0][None, :],
        "w_br_a": w_br_a[0].astype(BF16), "w_br_b": w_br_b[0].astype(BF16), "w_out": w_out[0].astype(BF16),
        "g_post_mix": g_post_mix[0][None, :], "g_pre_ffn": g_pre_ffn[0][None, :],
        "w_up": w_up[0].astype(BF16), "w_conv": w_conv[0], "b_conv": b_conv[0][None, :],
        "w_down": w_down[0].astype(BF16), "g_post_ffn": g_post_ffn[0][None, :],
        "w_ple": w_ple[0].astype(BF16), "w_ple_gate": (0.5 * w_ple_gate[0]).astype(BF16),
    }
    yp, kvp, sp, cp = _prompt_layer(x_prompt, p_prompt[0], w, tm_mix=512, tm_tok=1024, tm_ffn=512)
    wc = cache_win_k.shape[2]
    kv_cache = jnp.concatenate([cache_win_k[0].reshape(dbsz, wc, A_KV_W),
                                cache_win_v[0].reshape(dbsz, wc, A_KV_W)], axis=-1)
    ys, kvs, ss, cs = _sample_layer(x_sample, p_sample[0], kv_cache, state_hgrn[0], cache_ffn_conv[0], w,
                                    tm_tok=256, nseq_mix=8)
    keep = min(WINDOW, seq)

    def heads(a):
        return a.reshape(a.shape[0], a.shape[1], A_KV_HEADS, A_HEAD_DIM)[None]

    return (yp, ys,
            heads(kvp[:, seq - keep:, 0:A_KV_W]), heads(kvp[:, seq - keep:, A_KV_W:]),
            sp[None], cp[None],
            heads(kvs[:, :, 0:A_KV_W]), heads(kvs[:, :, A_KV_W:]),
            ss[None], cs[None])
```

```python
import functools
import math

import jax
import jax.numpy as jnp
from jax import lax
from jax.experimental import pallas as pl
from jax.experimental.pallas import tpu as pltpu

D_MODEL = 1024
CHUNK = 64
A_HEADS = 16
A_KV_HEADS = 2
A_HEAD_DIM = 64
A_GROUP = A_HEADS // A_KV_HEADS
WINDOW = 128
A_Q_W = A_HEADS * A_HEAD_DIM
A_KV_W = A_KV_HEADS * A_HEAD_DIM
NUM_BUCKETS = 32
MAX_DISTANCE = 128
B_HEADS = 8
B_KEY_DIM = 128
B_VAL_DIM = D_MODEL // B_HEADS
B_KEY_W = B_HEADS * B_KEY_DIM
B_VAL_W = B_HEADS * B_VAL_DIM
D_FF = 2816
CONV_W = 3
PLE_DIM = 256
EPS = 1e-6
NEG_INF = -1e30

_QA0 = 0
_KV0 = A_Q_W
_HB0 = _KV0 + 2 * A_KV_W
_GG0 = _HB0 + 2 * B_KEY_W + 2 * B_VAL_W
IN_COLS = _GG0 + 2 * D_MODEL
_HB_W = _GG0 - _HB0

_VMEM_LIMIT = 56 * 1024 * 1024

BF16 = jnp.bfloat16
F32 = jnp.float32


def _const_spec(shape):
    nd = len(shape)
    return pl.BlockSpec(shape, lambda *_: (0,) * nd, pipeline_mode=pl.Buffered(1))


def _rms(x, g):
    ms = jnp.mean(x * x, axis=-1, keepdims=True)
    return x * lax.rsqrt(ms + EPS) * g


def _sigmoid_of_twice(hx):
    return 0.5 * jnp.tanh(hx) + 0.5


def _dot(a, b):
    return jnp.dot(a, b, preferred_element_type=F32)


def _dot_nt(a, b):
    return lax.dot_general(a, b, (((1,), (1,)), ((), ())), preferred_element_type=F32)


def _dot_tn(a, b):
    return lax.dot_general(a, b, (((0,), (0,)), ((), ())), preferred_element_type=F32)


def _interleave(a, b):
    out, nb = [], 0
    for i, t in enumerate(a):
        out.append(t)
        want = ((i + 1) * len(b)) // len(a)
        out.extend(b[nb:want])
        nb = want
    return out + b[nb:]


def _attn_thunks(nchunk, lq, load_q, load_kw, bias_strip, sink_strip, store_o):
    def scores(c):
        qc = load_q(c)
        kw = load_kw(c)
        st = []
        for k in range(A_KV_HEADS):
            qs = jnp.concatenate(
                [qc[:, (k * A_GROUP + g) * A_HEAD_DIM:(k * A_GROUP + g + 1) * A_HEAD_DIM]
                 for g in range(A_GROUP)], axis=0)
            st.append(_dot_nt(kw[:, k * A_HEAD_DIM:(k + 1) * A_HEAD_DIM], qs))
        return dict(kw=kw, st=st)

    def softmax(c, s):
        ot, rden = [], []
        for k in range(A_KV_HEADS):
            ps, rs = [], []
            for j in range(0, A_GROUP * lq, 128):
                t = s["st"][k][:, j:j + 128] + bias_strip(c, k, j)
                sink = sink_strip(k, j)
                m = jnp.maximum(jnp.max(t, axis=0, keepdims=True), sink)
                p = jnp.exp(t - m)
                rs.append(1.0 / (jnp.sum(p, axis=0, keepdims=True) + jnp.exp(sink - m)))
                ps.append(p.astype(BF16))
            rden.append(jnp.concatenate(rs, axis=1))
            vv = s["kw"][:, A_KV_W + k * A_HEAD_DIM:A_KV_W + (k + 1) * A_HEAD_DIM]
            ot.append(_dot_tn(vv, jnp.concatenate(ps, axis=1)))
        return dict(ot=ot, rden=rden)

    def out(c, s):
        outs = []
        for k in range(A_KV_HEADS):
            o = (s["ot"][k] * s["rden"][k]).T
            outs.append(jnp.concatenate([o[g * lq:(g + 1) * lq, :] for g in range(A_GROUP)], axis=1))
        store_o(c, jnp.concatenate(outs, axis=1).astype(BF16))

    ahead = 2
    sc, sm, th = {}, {}, []

    def do_scores(c):
        sc[c] = scores(c)

    def do_softmax(c):
        sm[c] = softmax(c, sc.pop(c))

    def do_out(c):
        out(c, sm.pop(c))

    for c in range(min(ahead, nchunk)):
        th.append(functools.partial(do_scores, c))
    for c in range(nchunk):
        if c + ahead < nchunk:
            th.append(functools.partial(do_scores, c + ahead))
        th.append(functools.partial(do_softmax, c))
        if c >= 1:
            th.append(functools.partial(do_out, c - 1))
    th.append(functools.partial(do_out, nchunk - 1))
    return th


def _attn_body(q_ref, kvc_ref, kvp_ref, bias_ref, sink_ref, o_ref, kv_s, *, lq, lk, nchunk, tq, nseq):
    streams = []
    for n in range(nseq):
        kv_s[n, 0:WINDOW, :] = kvp_ref[n].astype(BF16)
        kv_s[n, WINDOW:WINDOW + tq, :] = kvc_ref[n].astype(BF16)

        def store_o(c, o, n=n):
            o_ref[n, c * lq:(c + 1) * lq, :] = o

        streams.append(_attn_thunks(
            nchunk, lq,
            load_q=lambda c, n=n: q_ref[n, c * lq:(c + 1) * lq, :],
            load_kw=lambda c, n=n: kv_s[n, c * lq:c * lq + lk, :],
            bias_strip=lambda c, k, j: bias_ref[0, k, :, j:j + 128],
            sink_strip=lambda k, j: sink_ref[k, :, j:j + 128],
            store_o=store_o))
    for group in zip(*streams):
        for t in group:
            t()


def _attention_cached(qa, kv, kv_prev, bias_ext, sink_ext, *, lq, nseq):
    bsz, t, _ = qa.shape
    lk = WINDOW + lq
    body = functools.partial(_attn_body, lq=lq, lk=lk, nchunk=t // lq, tq=t, nseq=nseq)
    return pl.pallas_call(
        body,
        grid=(bsz // nseq,),
        in_specs=[
            pl.BlockSpec((nseq, t, A_Q_W), lambda b: (b, 0, 0)),
            pl.BlockSpec((nseq, t, 2 * A_KV_W), lambda b: (b, 0, 0)),
            pl.BlockSpec((nseq, WINDOW, 2 * A_KV_W), lambda b: (b, 0, 0)),
            _const_spec(bias_ext.shape), _const_spec(sink_ext.shape),
        ],
        out_specs=pl.BlockSpec((nseq, t, A_Q_W), lambda b: (b, 0, 0)),
        out_shape=jax.ShapeDtypeStruct((bsz, t, A_Q_W), BF16),
        scratch_shapes=[pltpu.VMEM((nseq, WINDOW + t, 2 * A_KV_W), BF16)],
        compiler_params=pltpu.CompilerParams(
            dimension_semantics=("parallel",), vmem_limit_bytes=_VMEM_LIMIT),
        name="attention",
    )(qa, kv, kv_prev, bias_ext, sink_ext)


def _cumsum_rows_scan(x):
    n = x.shape[0]
    row = lax.broadcasted_iota(jnp.int32, x.shape, 0)
    s = 1
    while s < n:
        x = x + jnp.where(row >= s, pltpu.roll(x, s, 0), 0.0)
        s *= 2
    return x


def _hgrn_thunks(nchunk, blk, load, store_y, st_s, lbl, g):
    e = jnp.exp(lbl - jnp.max(lbl, axis=0, keepdims=True))
    lb = e[0:1, :] / jnp.sum(e, axis=0, keepdims=True)
    fa = 0.5 * (1.0 + lb)
    fb = 0.5 * (1.0 - lb)
    ri2 = lax.broadcasted_iota(jnp.int32, (blk, 2 * blk), 0)
    ci2 = lax.broadcasted_iota(jnp.int32, (blk, 2 * blk), 1)
    causal2 = ri2 >= (ci2 & (blk - 1))
    mid = blk // 2
    w = B_KEY_W
    pw = 2 * B_KEY_DIM
    npair = B_HEADS // 2
    ps = [slice(j * pw, (j + 1) * pw) for j in range(npair)]

    def blockdiag(x0, x1):
        z = jnp.zeros_like(x0)
        return jnp.concatenate([jnp.concatenate([x0, z], axis=1), jnp.concatenate([z, x1], axis=1)], axis=0)

    lo0 = slice(0, B_KEY_DIM)
    lo1 = slice(B_KEY_DIM, pw)

    def stage_decay(c):
        out = []
        for j in range(npair):
            bt = fb[:, ps[j]] * jnp.tanh(load(c, w + j * pw, w + (j + 1) * pw))
            f = fa[:, ps[j]] + bt
            cum = _cumsum_rows_scan(jnp.log2(f))
            out.append(dict(kk=fb[:, ps[j]] - bt, cum=cum))
        return out

    def stage_state(c, s):
        out = []
        for j in range(npair):
            cum = s[j]["cum"]
            hq = load(c, j * pw, (j + 1) * pw)
            qs = hq + hq * jnp.tanh(hq)
            b_last = cum[blk - 1:blk, :]
            b_mid = cum[mid:mid + 1, :]
            q2f = qs * jnp.exp2(cum - b_mid)
            k2f = s[j]["kk"] * jnp.exp2(b_mid - cum)
            q1 = (q2f * jnp.exp2(b_mid)).astype(BF16)
            k3 = (k2f * jnp.exp2(b_last - b_mid)).astype(BF16)
            q2 = q2f.astype(BF16)
            k2 = k2f.astype(BF16)
            vb = load(c, 2 * w + j * pw, 2 * w + (j + 1) * pw).astype(BF16)
            dec = jnp.exp2(b_last)
            a = _dot_nt(q2, blockdiag(k2[:, lo0], k2[:, lo1]))
            st0, st1 = st_s[2 * j], st_s[2 * j + 1]
            o1 = jnp.concatenate([_dot_nt(q1[:, lo0], st0.astype(BF16)),
                                  _dot_nt(q1[:, lo1], st1.astype(BF16))], axis=1)
            st_s[2 * j] = dec[:, lo0] * st0 + _dot_tn(vb[:, lo0], k3[:, lo0])
            st_s[2 * j + 1] = dec[:, lo1] * st1 + _dot_tn(vb[:, lo1], k3[:, lo1])
            out.append(dict(a=a, o1=o1, vb=vb))
        return out

    def stage_out(c, s):
        for j in range(npair):
            vb = s[j]["vb"]
            am = jnp.where(causal2, s[j]["a"], 0.0).astype(BF16)
            o = s[j]["o1"] + _dot(am, blockdiag(vb[:, lo0], vb[:, lo1]))
            y = jnp.concatenate([_rms(o[:, lo0], g), _rms(o[:, lo1], g)], axis=1)
            hog = load(c, 3 * w + j * pw, 3 * w + (j + 1) * pw)
            store_y(c, j * pw, (j + 1) * pw, (y * (hog + hog * jnp.tanh(hog))).astype(BF16))

    dec, sta, th = {}, {}, []

    def do_decay(c):
        dec[c] = stage_decay(c)

    def do_state(c):
        sta[c] = stage_state(c, dec.pop(c))

    def do_out(c):
        stage_out(c, sta.pop(c))

    th.append(functools.partial(do_decay, 0))
    for c in range(nchunk):
        if c + 1 < nchunk:
            th.append(functools.partial(do_decay, c + 1))
        th.append(functools.partial(do_state, c))
        if c >= 1:
            th.append(functools.partial(do_out, c - 1))
    th.append(functools.partial(do_out, nchunk - 1))
    return th


def _hgrn_body(hb_ref, lbl_ref, g_ref, s0_ref, yb_ref, sfin_ref, st_s, *, blk, nchunk, nseq):
    streams = []
    for q in range(nseq):
        for h in range(B_HEADS):
            st_s[q, h] = s0_ref[q, h].T

        def store_y(c, lo, hi, y, q=q):
            yb_ref[q, c * blk:(c + 1) * blk, lo:hi] = y

        streams.append(_hgrn_thunks(
            nchunk, blk, lambda c, lo, hi, q=q: hb_ref[q, c * blk:(c + 1) * blk, lo:hi],
            store_y, st_s.at[q], lbl_ref[...], g_ref[...]))
    for group in zip(*streams):
        for t in group:
            t()
    for q in range(nseq):
        for h in range(B_HEADS):
            sfin_ref[q, h] = st_s[q, h].T


def _hgrn_cached(hb, lb_logits, g_out, s0, *, blk, nseq):
    bsz, t, _ = hb.shape
    body = functools.partial(_hgrn_body, blk=blk, nchunk=t // blk, nseq=nseq)
    st_spec = pl.BlockSpec((nseq, B_HEADS, B_KEY_DIM, B_VAL_DIM), lambda b: (b, 0, 0, 0))
    return pl.pallas_call(
        body,
        grid=(bsz // nseq,),
        in_specs=[pl.BlockSpec((nseq, t, _HB_W), lambda b: (b, 0, 0)),
                  _const_spec(lb_logits.shape), _const_spec((1, B_VAL_DIM)), st_spec],
        out_specs=[pl.BlockSpec((nseq, t, B_VAL_W), lambda b: (b, 0, 0)), st_spec],
        out_shape=[jax.ShapeDtypeStruct((bsz, t, B_VAL_W), BF16),
                   jax.ShapeDtypeStruct((bsz, B_HEADS, B_KEY_DIM, B_VAL_DIM), F32)],
        scratch_shapes=[pltpu.VMEM((nseq, B_HEADS, B_VAL_DIM, B_KEY_DIM), F32)],
        compiler_params=pltpu.CompilerParams(
            dimension_semantics=("parallel",), vmem_limit_bytes=_VMEM_LIMIT),
        name="hgrn2",
    )(hb, lb_logits, g_out, s0)


def _inproj_body(x_ref, g_ref, w_ref, qa_ref, kv_ref, hb_ref, gg_ref):
    h = _rms(x_ref[...], g_ref[...]).astype(BF16)
    step = 512

    def mm(lo, width):
        return _dot(h, w_ref[:, lo:lo + width])

    for j in range(0, A_Q_W, step):
        qa_ref[:, j:j + step] = mm(_QA0 + j, step).astype(BF16)
    kv_ref[...] = mm(_KV0, 2 * A_KV_W)
    for j in range(0, _HB_W, step):
        hb_ref[:, j:j + step] = mm(_HB0 + j, step)
    for j in range(0, 2 * D_MODEL, step):
        gg_ref[:, j:j + step] = mm(_GG0 + j, step).astype(BF16)


def _inproj(x2d, g, w_bf, tm):
    n = x2d.shape[0]
    row = lambda w: pl.BlockSpec((tm, w), lambda i: (i, 0))
    return pl.pallas_call(
        _inproj_body,
        grid=(n // tm,),
        in_specs=[row(D_MODEL), _const_spec((1, D_MODEL)), _const_spec((D_MODEL, IN_COLS))],
        out_specs=[row(A_Q_W), row(2 * A_KV_W), row(_HB_W), row(2 * D_MODEL)],
        out_shape=[
            jax.ShapeDtypeStruct((n, A_Q_W), BF16),
            jax.ShapeDtypeStruct((n, 2 * A_KV_W), F32),
            jax.ShapeDtypeStruct((n, _HB_W), F32),
            jax.ShapeDtypeStruct((n, 2 * D_MODEL), BF16),
        ],
        compiler_params=pltpu.CompilerParams(
            dimension_semantics=("parallel",), vmem_limit_bytes=_VMEM_LIMIT),
        name="inproj",
    )(x2d, g, w_bf)


def _mixer_body(x_ref, gpre_ref, w_ref, lbl_ref, ghg_ref, bias_ref, sink_ref,
                kv_ref, gg_ref, ya_ref, yb_ref, sfin_ref,
                h_s, qa_s, kvb_s, kvw_s, hb_s, st_s, *, tm, tiles_per_seq):
    s = pl.program_id(0)
    cur = s % 2
    prv = 1 - cur
    tib = (s + tiles_per_seq - 1) % tiles_per_seq
    nchunk = tm // CHUNK
    lk = WINDOW + CHUNK
    piece = 256

    @pl.when(s == 0)
    def _():
        qa_s[...] = jnp.zeros(qa_s.shape, BF16)
        kvb_s[...] = jnp.zeros(kvb_s.shape, BF16)
        kvw_s[...] = jnp.zeros(kvw_s.shape, BF16)
        hb_s[...] = jnp.zeros(hb_s.shape, F32)

    @pl.when((s == 0) | (tib == 0))
    def _():
        st_s[...] = jnp.zeros(st_s.shape, F32)

    kvw_s[0:WINDOW, :] = kvw_s[tm:tm + WINDOW, :]
    kvw_s[WINDOW:WINDOW + tm, :] = kvb_s[prv]

    def dense_piece(lo, width):
        z = _dot(h_s[...], w_ref[:, lo:lo + width])
        if lo < _KV0:
            qa_s[cur, :, lo:lo + width] = z.astype(BF16)
        elif lo < _HB0:
            kv_ref[...] = z
            kvb_s[cur] = z.astype(BF16)
        elif lo < _GG0:
            hb_s[:, lo - _HB0:lo - _HB0 + width] = z
        else:
            gg_ref[:, lo - _GG0:lo - _GG0 + width] = z.astype(BF16)

    def pieces(lo, hi):
        return [functools.partial(dense_piece, c, min(piece, hi - c)) for c in range(lo, hi, piece)]

    dense_hb = pieces(_HB0, _GG0)
    dense_rest = pieces(_QA0, _KV0) + pieces(_KV0, _HB0) + pieces(_GG0, IN_COLS)

    def store_ya(c, o):
        ya_ref[c * CHUNK:(c + 1) * CHUNK, :] = o

    def store_yb(c, lo, hi, y):
        yb_ref[c * CHUNK:(c + 1) * CHUNK, lo:hi] = y

    def bias_strip(c, k, j):
        var = jnp.clip(WINDOW // CHUNK - (tib * nchunk + c), 0, WINDOW // CHUNK)
        return bias_ref[var, k, :, j:j + 128]

    attn = _attn_thunks(
        nchunk, CHUNK,
        load_q=lambda c: qa_s[prv, c * CHUNK:(c + 1) * CHUNK, :],
        load_kw=lambda c: kvw_s[c * CHUNK:c * CHUNK + lk, :],
        bias_strip=bias_strip, sink_strip=lambda k, j: sink_ref[k, :, j:j + 128], store_o=store_ya)
    hgrn = _hgrn_thunks(
        nchunk, CHUNK, lambda c, lo, hi: hb_s[c * CHUNK:(c + 1) * CHUNK, lo:hi],
        store_yb, st_s, lbl_ref[...], ghg_ref[...])
    for t in attn[:2]:
        t()
    h_s[...] = _rms(x_ref[...], gpre_ref[...]).astype(BF16)
    for t in _interleave(dense_rest, hgrn) + _interleave(dense_hb, attn[2:]):
        t()

    @pl.when((tib == tiles_per_seq - 1) & (s > 0))
    def _():
        for h in range(B_HEADS):
            sfin_ref[h] = st_s[h].T


def _mixer(x2d, g_pre, w_bf, lb_logits, g_hgrn, bias_ext, sink_ext, *, tm, seq):
    n = x2d.shape[0]
    nt = n // tm
    tiles_per_seq = seq // tm
    dense_row = lambda w: pl.BlockSpec((tm, w), lambda s: (jnp.minimum(s, nt - 1), 0))
    lag_row = lambda w: pl.BlockSpec((tm, w), lambda s: (jnp.maximum(s - 1, 0), 0))
    body = functools.partial(_mixer_body, tm=tm, tiles_per_seq=tiles_per_seq)
    return pl.pallas_call(
        body,
        grid=(nt + 1,),
        in_specs=[dense_row(D_MODEL), _const_spec((1, D_MODEL)), _const_spec((D_MODEL, IN_COLS)),
                  _const_spec(lb_logits.shape), _const_spec((1, B_VAL_DIM)), _const_spec(bias_ext.shape),
                  _const_spec(sink_ext.shape)],
        out_specs=[dense_row(2 * A_KV_W), dense_row(2 * D_MODEL), lag_row(A_Q_W), lag_row(B_VAL_W),
                   pl.BlockSpec((None, B_HEADS, B_KEY_DIM, B_VAL_DIM),
                                lambda s: (jnp.maximum(s - 1, 0) // tiles_per_seq, 0, 0, 0))],
        out_shape=[jax.ShapeDtypeStruct((n, 2 * A_KV_W), F32),
                   jax.ShapeDtypeStruct((n, 2 * D_MODEL), BF16),
                   jax.ShapeDtypeStruct((n, A_Q_W), BF16),
                   jax.ShapeDtypeStruct((n, B_VAL_W), BF16),
                   jax.ShapeDtypeStruct((n // seq, B_HEADS, B_KEY_DIM, B_VAL_DIM), F32)],
        scratch_shapes=[pltpu.VMEM((tm, D_MODEL), BF16),
                        pltpu.VMEM((2, tm, A_Q_W), BF16),
                        pltpu.VMEM((2, tm, 2 * A_KV_W), BF16),
                        pltpu.VMEM((WINDOW + tm, 2 * A_KV_W), BF16),
                        pltpu.VMEM((tm, _HB_W), F32),
                        pltpu.VMEM((B_HEADS, B_VAL_DIM, B_KEY_DIM), F32)],
        compiler_params=pltpu.CompilerParams(
            dimension_semantics=("arbitrary",), vmem_limit_bytes=_VMEM_LIMIT),
        name="mixer",
    )(x2d, g_pre, w_bf, lb_logits, g_hgrn, bias_ext, sink_ext)


def _merge_body(ya_ref, yb_ref, gg_ref, x_ref, wa_ref, wb_ref, wo_ref, g_ref, o_ref, *, tm, strip):
    def branches(r):
        rows = slice(r * strip, (r + 1) * strip)
        ga = gg_ref[rows, 0:D_MODEL].astype(F32)
        gb = gg_ref[rows, D_MODEL:2 * D_MODEL].astype(F32)
        mix = (_sigmoid_of_twice(ga) * _dot(ya_ref[rows, :], wa_ref[...])
               + _sigmoid_of_twice(gb) * _dot(yb_ref[rows, :], wb_ref[...]))
        return mix.astype(BF16)

    def project(r, mix):
        rows = slice(r * strip, (r + 1) * strip)
        o_ref[rows, :] = x_ref[rows, :] + _rms(_dot(mix, wo_ref[...]), g_ref[...])

    nstrip = tm // strip
    mix = branches(0)
    for r in range(nstrip):
        nxt = branches(r + 1) if r + 1 < nstrip else None
        project(r, mix)
        mix = nxt


def _merge(ya, yb, gg, x2d, wa, wb, wo, g, tm):
    n = x2d.shape[0]
    row = lambda w: pl.BlockSpec((tm, w), lambda i: (i, 0))
    wspec = _const_spec((D_MODEL, D_MODEL))
    return pl.pallas_call(
        functools.partial(_merge_body, tm=tm, strip=min(tm, 256)),
        grid=(n // tm,),
        in_specs=[row(A_Q_W), row(B_VAL_W), row(2 * D_MODEL), row(D_MODEL),
                  wspec, wspec, wspec, _const_spec((1, D_MODEL))],
        out_specs=row(D_MODEL),
        out_shape=jax.ShapeDtypeStruct((n, D_MODEL), F32),
        compiler_params=pltpu.CompilerParams(
            dimension_semantics=("parallel",), vmem_limit_bytes=_VMEM_LIMIT),
        name="merge",
    )(ya, yb, gg, x2d, wa, wb, wo, g)


def _gelu_tanh(x):
    c = math.sqrt(2.0 / math.pi)
    return 0.5 * x * (1.0 + jnp.tanh(c * (x + 0.044715 * (x * x * x))))


def _ffn_up(x, gpre_ref, wup_ref):
    hf = _rms(x, gpre_ref[...]).astype(BF16)
    return _dot(hf, wup_ref[:, 0:D_FF]), _dot(hf, wup_ref[:, D_FF:2 * D_FF])


def _ffn_down(x, pe, a, u, a1, a2, wconv_ref, bconv_ref, wdown_ref, gpost_ref, wple_ref, wgate_ref):
    ac = bconv_ref[...] + a2 * wconv_ref[0:1, :] + a1 * wconv_ref[1:2, :] + a * wconv_ref[2:3, :]
    gl = (_gelu_tanh(ac) * u).astype(BF16)
    x2 = x + _rms(_dot(gl, wdown_ref[...]), gpost_ref[...])
    pex = _dot(pe.astype(BF16), wple_ref[...])
    gate = _sigmoid_of_twice(_dot(x2.astype(BF16), wgate_ref[...]))
    return x2 + pex * gate


def _ffn_tile(x, pe, shifted, gpre_ref, wup_ref, *rest):
    a, u = _ffn_up(x, gpre_ref, wup_ref)
    a1, a2 = shifted(a)
    return _ffn_down(x, pe, a, u, a1, a2, *rest), a


def _ffn_body(x_ref, pe_ref, *refs, tm, strip):
    (gpre_ref, wup_ref), rest, (o_ref, tail_ref, carry_s) = refs[:2], refs[2:8], refs[8:]
    i = pl.program_id(1)

    @pl.when(i == 0)
    def _():
        carry_s[...] = jnp.zeros((CONV_W - 1, D_FF), F32)

    row = lax.broadcasted_iota(jnp.int32, (strip, D_FF), 0)
    nstrip = tm // strip
    rows = [slice(r * strip, (r + 1) * strip) for r in range(nstrip)]
    hist = carry_s[...]
    au = _ffn_up(x_ref[rows[0], :], gpre_ref, wup_ref)
    for r in range(nstrip):
        nxt = _ffn_up(x_ref[rows[r + 1], :], gpre_ref, wup_ref) if r + 1 < nstrip else None
        a, u = au
        c0, c1 = hist[0:1, :], hist[1:2, :]
        a1 = jnp.where(row == 0, c1, pltpu.roll(a, 1, 0))
        a2 = jnp.where(row == 0, c0, jnp.where(row == 1, c1, pltpu.roll(a, 2, 0)))
        o_ref[rows[r], :] = _ffn_down(x_ref[rows[r], :], pe_ref[rows[r], :], a, u, a1, a2, *rest)
        hist = a[strip - (CONV_W - 1):strip, :]
        au = nxt
    carry_s[...] = hist
    tail_ref[...] = hist


def _ffn_cached_body(x_ref, pe_ref, prev_ref, *refs, t, nseq):
    w_refs, (o_ref, tail_ref) = refs[:8], refs[8:]
    pos = lax.broadcasted_iota(jnp.int32, (nseq * t, D_FF), 0) & (t - 1)

    def history(j):
        return jnp.concatenate([jnp.broadcast_to(prev_ref[q, j:j + 1, :], (t, D_FF)) for q in range(nseq)],
                               axis=0)

    def shifted(a):
        p0, p1 = history(0), history(1)
        a1 = jnp.where(pos == 0, p1, pltpu.roll(a, 1, 0))
        a2 = jnp.where(pos == 0, p0, jnp.where(pos == 1, p1, pltpu.roll(a, 2, 0)))
        return a1, a2

    x = x_ref[...].reshape(nseq * t, D_MODEL)
    pe = pe_ref[...].reshape(nseq * t, PLE_DIM)
    out, a = _ffn_tile(x, pe, shifted, *w_refs)
    o_ref[...] = out.reshape(nseq, t, D_MODEL)
    for q in range(nseq):
        tail_ref[q] = a[(q + 1) * t - (CONV_W - 1):(q + 1) * t, :]


def _ffn_weight_specs():
    return [_const_spec((1, D_MODEL)), _const_spec((D_MODEL, 2 * D_FF)),
            _const_spec((CONV_W, D_FF)), _const_spec((1, D_FF)),
            _const_spec((D_FF, D_MODEL)), _const_spec((1, D_MODEL)),
            _const_spec((PLE_DIM, D_MODEL)), _const_spec((D_MODEL, D_MODEL))]


def _ffn(x, pe, weights, tm):
    bsz, t, _ = x.shape
    tail_spec = pl.BlockSpec((None, CONV_W - 1, D_FF), lambda b, i: (b, 0, 0))
    return pl.pallas_call(
        functools.partial(_ffn_body, tm=tm, strip=min(tm, 256)),
        grid=(bsz, t // tm),
        in_specs=[pl.BlockSpec((None, tm, D_MODEL), lambda b, i: (b, i, 0)),
                  pl.BlockSpec((None, tm, PLE_DIM), lambda b, i: (b, i, 0))] + _ffn_weight_specs(),
        out_specs=[pl.BlockSpec((None, tm, D_MODEL), lambda b, i: (b, i, 0)), tail_spec],
        out_shape=[jax.ShapeDtypeStruct((bsz, t, D_MODEL), F32),
                   jax.ShapeDtypeStruct((bsz, CONV_W - 1, D_FF), F32)],
        scratch_shapes=[pltpu.VMEM((CONV_W - 1, D_FF), F32)],
        compiler_params=pltpu.CompilerParams(
            dimension_semantics=("parallel", "arbitrary"), vmem_limit_bytes=_VMEM_LIMIT),
        name="convffn",
    )(x, pe, *weights)


def _ffn_cached(x, pe, prev, weights, nseq):
    bsz, t, _ = x.shape
    tail_spec = pl.BlockSpec((nseq, CONV_W - 1, D_FF), lambda b: (b, 0, 0))
    return pl.pallas_call(
        functools.partial(_ffn_cached_body, t=t, nseq=nseq),
        grid=(bsz // nseq,),
        in_specs=[pl.BlockSpec((nseq, t, D_MODEL), lambda b: (b, 0, 0)),
                  pl.BlockSpec((nseq, t, PLE_DIM), lambda b: (b, 0, 0)), tail_spec] + _ffn_weight_specs(),
        out_specs=[pl.BlockSpec((nseq, t, D_MODEL), lambda b: (b, 0, 0)), tail_spec],
        out_shape=[jax.ShapeDtypeStruct((bsz, t, D_MODEL), F32),
                   jax.ShapeDtypeStruct((bsz, CONV_W - 1, D_FF), F32)],
        compiler_params=pltpu.CompilerParams(
            dimension_semantics=("parallel",), vmem_limit_bytes=_VMEM_LIMIT),
        name="convffn_cached",
    )(x, pe, prev, *weights)


def _t5_bucket(rel):
    nb = NUM_BUCKETS // 2
    ret = jnp.where(rel > 0, nb, 0)
    n = jnp.abs(rel)
    max_exact = nb // 2
    large = max_exact + (jnp.log(jnp.maximum(n, max_exact).astype(jnp.float32) / max_exact)
                         / math.log(MAX_DISTANCE / max_exact) * (nb - max_exact)).astype(jnp.int32)
    large = jnp.minimum(large, nb - 1)
    return ret + jnp.where(n < max_exact, n, large)


def _bias_body(table_ref, bk_ref, o_ref, *, nvar):
    bk = bk_ref[...]
    row = lax.broadcasted_iota(jnp.int32, bk.shape, 0)
    for k in range(A_KV_HEADS):
        acc = jnp.zeros(bk.shape, F32)
        for b in range(NUM_BUCKETS):
            acc = jnp.where(bk == b, table_ref[k, b:b + 1, :], acc)
        for v in range(nvar):
            o_ref[v, k] = jnp.where(row < v * CHUNK, NEG_INF, acc)


def _bias_ext(table, sinks, lq, lk, nvar):
    q_pos = jnp.arange(lq) + WINDOW
    k_pos = jnp.arange(lk)
    buckets = _t5_bucket(k_pos[:, None] - q_pos[None, :]).astype(jnp.int32)
    bk = jnp.tile(buckets, (1, A_GROUP))
    tab = jnp.repeat(table.astype(F32).reshape(NUM_BUCKETS, A_KV_HEADS, A_GROUP), lq, axis=2)
    tab = jnp.transpose(tab, (1, 0, 2))
    snk = jnp.repeat(sinks.astype(F32).reshape(A_KV_HEADS, 1, A_GROUP), lq, axis=2)
    vmem = pl.BlockSpec(memory_space=pltpu.VMEM)
    bias = pl.pallas_call(
        functools.partial(_bias_body, nvar=nvar),
        in_specs=[vmem, vmem],
        out_specs=vmem,
        out_shape=jax.ShapeDtypeStruct((nvar, A_KV_HEADS, lk, A_GROUP * lq), F32),
        name="relbias",
    )(tab, bk)
    return bias, snk


def _ffn_weights(w):
    return [w[k] for k in ("g_pre_ffn", "w_up", "w_conv", "b_conv", "w_down", "g_post_ffn", "w_ple", "w_ple_gate")]


def _prompt_layer(x, pe, w, *, tm_mix, tm_tok, tm_ffn):
    bsz, t, _ = x.shape
    n = bsz * t
    x2d = x.reshape(n, D_MODEL)
    bias_ext, sink_ext = _bias_ext(w["rel_table"], w["sinks"], CHUNK, WINDOW + CHUNK, WINDOW // CHUNK + 1)
    kv, gg, ya, yb, s_fin = _mixer(x2d, w["g_pre_mix"], w["w_in"], w["lb_logits"], w["g_hgrn_out"],
                                   bias_ext, sink_ext, tm=tm_mix, seq=t)
    x1 = _merge(ya, yb, gg, x2d, w["w_br_a"], w["w_br_b"], w["w_out"], w["g_post_mix"], tm_tok)
    y, conv_tail = _ffn(x1.reshape(bsz, t, D_MODEL), pe, _ffn_weights(w), tm_ffn)
    return y, kv.reshape(bsz, t, 2 * A_KV_W), s_fin, conv_tail


def _sample_layer(x, pe, kv_prev, s_prev, conv_prev, w, *, tm_tok, nseq_mix):
    bsz, t, _ = x.shape
    n = bsz * t
    x2d = x.reshape(n, D_MODEL)
    qa, kv, hb, gg = _inproj(x2d, w["g_pre_mix"], w["w_in"], tm_tok)
    bias_ext, sink_ext = _bias_ext(w["rel_table"], w["sinks"], t, WINDOW + t, 1)
    kv3 = kv.reshape(bsz, t, 2 * A_KV_W)
    ya = _attention_cached(qa.reshape(bsz, t, A_Q_W), kv3, kv_prev, bias_ext, sink_ext, lq=t,
                           nseq=nseq_mix)
    yb, s_fin = _hgrn_cached(hb.reshape(bsz, t, _HB_W), w["lb_logits"], w["g_hgrn_out"], s_prev,
                             blk=t, nseq=nseq_mix)
    x1 = _merge(ya.reshape(n, A_Q_W), yb.reshape(n, B_VAL_W), gg, x2d,
                w["w_br_a"], w["w_br_b"], w["w_out"], w["g_post_mix"], tm_tok)
    y, conv_tail = _ffn_cached(x1.reshape(bsz, t, D_MODEL), pe, conv_prev, _ffn_weights(w), tm_tok // t)
    return y, kv3, s_fin, conv_tail


def _scale_in_cols(w_in):
    h = B_KEY_W
    scale = jnp.concatenate([
        jnp.full((A_Q_W,), A_HEAD_DIM ** -0.5, F32), jnp.ones((2 * A_KV_W,), F32),
        jnp.full((2 * h,), 0.5, F32), jnp.ones((B_VAL_W,), F32), jnp.full((B_VAL_W,), 0.5, F32),
        jnp.full((2 * D_MODEL,), 0.5, F32)])
    return w_in * scale[None, :]


def kernel(x_prompt, x_sample, cache_win_k, cache_win_v, state_hgrn, cache_ffn_conv, p_prompt, p_sample,
           rel_bias_table, lb_logits, g_pre_mix, w_in, attn_sinks, g_hgrn_out, w_br_a, w_br_b, w_out,
           g_post_mix, g_pre_ffn, w_up, w_conv, b_conv, w_down, g_post_ffn, w_ple, w_ple_gate):
    bsz, seq, _ = x_prompt.shape
    dbsz, dseq, _ = x_sample.shape
    w = {
        "rel_table": rel_bias_table, "sinks": attn_sinks[0], "lb_logits": lb_logits.astype(F32),
        "g_pre_mix": g_pre_mix[0][None, :], "w_in": _scale_in_cols(w_in[0]).astype(BF16),
        "g_hgrn_out": g_hgrn_out[0][None, :],
        "w_br_a": w_br_a[0].astype(BF16), "w_br_b": w_br_b[0].astype(BF16), "w_out": w_out[0].astype(BF16),
        "g_post_mix": g_post_mix[0][None, :], "g_pre_ffn": g_pre_ffn[0][None, :],
        "w_up": w_up[0].astype(BF16), "w_conv": w_conv[0], "b_conv": b_conv[0][None, :],
        "w_down": w_down[0].astype(BF16), "g_post_ffn": g_post_ffn[0][None, :],
        "w_ple": w_ple[0].astype(BF16), "w_ple_gate": (0.5 * w_ple_gate[0]).astype(BF16),
    }
    yp, kvp, sp, cp = _prompt_layer(x_prompt, p_prompt[0], w, tm_mix=512, tm_tok=1024, tm_ffn=512)
    wc = cache_win_k.shape[2]
    kv_cache = jnp.concatenate([cache_win_k[0].reshape(dbsz, wc, A_KV_W),
                                cache_win_v[0].reshape(dbsz, wc, A_KV_W)], axis=-1)
    ys, kvs, ss, cs = _sample_layer(x_sample, p_sample[0], kv_cache, state_hgrn[0], cache_ffn_conv[0], w,
                                    tm_tok=256, nseq_mix=8)
    keep = min(WINDOW, seq)

    def heads(a):
        return a.reshape(a.shape[0], a.shape[1], A_KV_HEADS, A_HEAD_DIM)[None]

    return (yp, ys,
            heads(kvp[:, seq - keep:, 0:A_KV_W]), heads(kvp[:, seq - keep:, A_KV_W:]),
            sp[None], cp[None],
            heads(kvs[:, :, 0:A_KV_W]), heads(kvs[:, :, A_KV_W:]),
            ss[None], cs[None])
```

```python
import functools
import math

import jax
import jax.numpy as jnp
from jax import lax
from jax.experimental import pallas as pl
from jax.experimental.pallas import tpu as pltpu

D_MODEL = 1024
CHUNK = 64
A_HEADS = 16
A_KV_HEADS = 2
A_HEAD_DIM = 64
A_GROUP = A_HEADS // A_KV_HEADS
WINDOW = 128
A_Q_W = A_HEADS * A_HEAD_DIM
A_KV_W = A_KV_HEADS * A_HEAD_DIM
NUM_BUCKETS = 32
MAX_DISTANCE = 128
B_HEADS = 8
B_KEY_DIM = 128
B_VAL_DIM = D_MODEL // B_HEADS
B_KEY_W = B_HEADS * B_KEY_DIM
B_VAL_W = B_HEADS * B_VAL_DIM
D_FF = 2816
CONV_W = 3
PLE_DIM = 256
EPS = 1e-6
NEG_INF = -1e30

_QA0 = 0
_KV0 = A_Q_W
_HB0 = _KV0 + 2 * A_KV_W
_GG0 = _HB0 + 2 * B_KEY_W + 2 * B_VAL_W
IN_COLS = _GG0 + 2 * D_MODEL
_HB_W = _GG0 - _HB0
_HBV0 = _HB0 + 2 * B_KEY_W

_VMEM_LIMIT = 56 * 1024 * 1024

BF16 = jnp.bfloat16
F32 = jnp.float32


def _const_spec(shape):
    nd = len(shape)
    return pl.BlockSpec(shape, lambda *_: (0,) * nd, pipeline_mode=pl.Buffered(1))


def _rms(x, g):
    ms = jnp.mean(x * x, axis=-1, keepdims=True)
    return x * lax.rsqrt(ms + EPS) * g


def _sigmoid_of_twice(hx):
    return 0.5 * jnp.tanh(hx) + 0.5


def _dot(a, b):
    return jnp.dot(a, b, preferred_element_type=F32)


def _dot_nt(a, b):
    return lax.dot_general(a, b, (((1,), (1,)), ((), ())), preferred_element_type=F32)


def _dot_tn(a, b):
    return lax.dot_general(a, b, (((0,), (0,)), ((), ())), preferred_element_type=F32)


def _interleave(a, b):
    out, nb = [], 0
    for i, t in enumerate(a):
        out.append(t)
        want = ((i + 1) * len(b)) // len(a)
        out.extend(b[nb:want])
        nb = want
    return out + b[nb:]


def _attn_thunks(nchunk, lq, load_q, load_kw, bias_strip, sink_strip, store_o):
    def scores(c):
        qc = load_q(c)
        kw = load_kw(c)
        st = []
        for k in range(A_KV_HEADS):
            qs = jnp.concatenate(
                [qc[:, (k * A_GROUP + g) * A_HEAD_DIM:(k * A_GROUP + g + 1) * A_HEAD_DIM]
                 for g in range(A_GROUP)], axis=0)
            st.append(_dot_nt(kw[:, k * A_HEAD_DIM:(k + 1) * A_HEAD_DIM], qs))
        return dict(kw=kw, st=st)

    def softmax(c, s):
        ot, rden = [], []
        for k in range(A_KV_HEADS):
            ps, rs = [], []
            for j in range(0, A_GROUP * lq, 128):
                t = s["st"][k][:, j:j + 128] + bias_strip(c, k, j)
                sink = sink_strip(k, j)
                m = jnp.maximum(jnp.max(t, axis=0, keepdims=True), sink)
                p = jnp.exp(t - m)
                rs.append(1.0 / (jnp.sum(p, axis=0, keepdims=True) + jnp.exp(sink - m)))
                ps.append(p.astype(BF16))
            rden.append(jnp.concatenate(rs, axis=1))
            vv = s["kw"][:, A_KV_W + k * A_HEAD_DIM:A_KV_W + (k + 1) * A_HEAD_DIM]
            ot.append(_dot_tn(vv, jnp.concatenate(ps, axis=1)))
        return dict(ot=ot, rden=rden)

    def out(c, s):
        outs = []
        for k in range(A_KV_HEADS):
            o = (s["ot"][k] * s["rden"][k]).T
            outs.append(jnp.concatenate([o[g * lq:(g + 1) * lq, :] for g in range(A_GROUP)], axis=1))
        store_o(c, jnp.concatenate(outs, axis=1).astype(BF16))

    ahead = 2
    sc, sm, th = {}, {}, []

    def do_scores(c):
        sc[c] = scores(c)

    def do_softmax(c):
        sm[c] = softmax(c, sc.pop(c))

    def do_out(c):
        out(c, sm.pop(c))

    for c in range(min(ahead, nchunk)):
        th.append(functools.partial(do_scores, c))
    for c in range(nchunk):
        if c + ahead < nchunk:
            th.append(functools.partial(do_scores, c + ahead))
        th.append(functools.partial(do_softmax, c))
        if c >= 1:
            th.append(functools.partial(do_out, c - 1))
    th.append(functools.partial(do_out, nchunk - 1))
    return th


def _attn_body(q_ref, kvc_ref, kvp_ref, bias_ref, sink_ref, o_ref, kv_s, *, lq, lk, nchunk, tq, nseq):
    streams = []
    for n in range(nseq):
        kv_s[n, 0:WINDOW, :] = kvp_ref[n].astype(BF16)
        kv_s[n, WINDOW:WINDOW + tq, :] = kvc_ref[n].astype(BF16)

        def store_o(c, o, n=n):
            o_ref[n, c * lq:(c + 1) * lq, :] = o

        streams.append(_attn_thunks(
            nchunk, lq,
            load_q=lambda c, n=n: q_ref[n, c * lq:(c + 1) * lq, :],
            load_kw=lambda c, n=n: kv_s[n, c * lq:c * lq + lk, :],
            bias_strip=lambda c, k, j: bias_ref[0, k, :, j:j + 128],
            sink_strip=lambda k, j: sink_ref[k, :, j:j + 128],
            store_o=store_o))
    for group in zip(*streams):
        for t in group:
            t()


def _attention_cached(qa, kv, kv_prev, bias_ext, sink_ext, *, lq, nseq):
    bsz, t, _ = qa.shape
    lk = WINDOW + lq
    body = functools.partial(_attn_body, lq=lq, lk=lk, nchunk=t // lq, tq=t, nseq=nseq)
    return pl.pallas_call(
        body,
        grid=(bsz // nseq,),
        in_specs=[
            pl.BlockSpec((nseq, t, A_Q_W), lambda b: (b, 0, 0)),
            pl.BlockSpec((nseq, t, 2 * A_KV_W), lambda b: (b, 0, 0)),
            pl.BlockSpec((nseq, WINDOW, 2 * A_KV_W), lambda b: (b, 0, 0)),
            _const_spec(bias_ext.shape), _const_spec(sink_ext.shape),
        ],
        out_specs=pl.BlockSpec((nseq, t, A_Q_W), lambda b: (b, 0, 0)),
        out_shape=jax.ShapeDtypeStruct((bsz, t, A_Q_W), BF16),
        scratch_shapes=[pltpu.VMEM((nseq, WINDOW + t, 2 * A_KV_W), BF16)],
        compiler_params=pltpu.CompilerParams(
            dimension_semantics=("parallel",), vmem_limit_bytes=_VMEM_LIMIT),
        name="attention",
    )(qa, kv, kv_prev, bias_ext, sink_ext)


def _cumsum_rows_scan(x):
    n = x.shape[0]
    row = lax.broadcasted_iota(jnp.int32, x.shape, 0)
    s = 1
    while s < n:
        x = x + jnp.where(row >= s, pltpu.roll(x, s, 0), 0.0)
        s *= 2
    return x


def _hgrn_thunks(nchunk, blk, load, store_y, st_s, lbl, g):
    e = jnp.exp(lbl - jnp.max(lbl, axis=0, keepdims=True))
    lb = e[0:1, :] / jnp.sum(e, axis=0, keepdims=True)
    fa = 0.5 * (1.0 + lb)
    fb = 0.5 * (1.0 - lb)
    ri2 = lax.broadcasted_iota(jnp.int32, (blk, 2 * blk), 0)
    ci2 = lax.broadcasted_iota(jnp.int32, (blk, 2 * blk), 1)
    causal2 = ri2 >= (ci2 & (blk - 1))
    mid = blk // 2
    w = B_KEY_W
    pw = 2 * B_KEY_DIM
    npair = B_HEADS // 2
    ps = [slice(j * pw, (j + 1) * pw) for j in range(npair)]

    def blockdiag(x0, x1):
        z = jnp.zeros_like(x0)
        return jnp.concatenate([jnp.concatenate([x0, z], axis=1), jnp.concatenate([z, x1], axis=1)], axis=0)

    lo0 = slice(0, B_KEY_DIM)
    lo1 = slice(B_KEY_DIM, pw)

    def stage_decay(c):
        out = []
        for j in range(npair):
            bt = fb[:, ps[j]] * jnp.tanh(load(c, w + j * pw, w + (j + 1) * pw))
            f = fa[:, ps[j]] + bt
            cum = _cumsum_rows_scan(jnp.log2(f))
            out.append(dict(kk=fb[:, ps[j]] - bt, cum=cum))
        return out

    def stage_state(c, s):
        out = []
        for j in range(npair):
            cum = s[j]["cum"]
            hq = load(c, j * pw, (j + 1) * pw)
            qs = hq + hq * jnp.tanh(hq)
            b_last = cum[blk - 1:blk, :]
            b_mid = cum[mid:mid + 1, :]
            q2f = qs * jnp.exp2(cum - b_mid)
            k2f = s[j]["kk"] * jnp.exp2(b_mid - cum)
            q1 = (q2f * jnp.exp2(b_mid)).astype(BF16)
            k3 = (k2f * jnp.exp2(b_last - b_mid)).astype(BF16)
            q2 = q2f.astype(BF16)
            k2 = k2f.astype(BF16)
            vb = load(c, 2 * w + j * pw, 2 * w + (j + 1) * pw).astype(BF16)
            dec = jnp.exp2(b_last)
            a = _dot_nt(q2, blockdiag(k2[:, lo0], k2[:, lo1]))
            st0, st1 = st_s[2 * j], st_s[2 * j + 1]
            o1 = jnp.concatenate([_dot_nt(q1[:, lo0], st0.astype(BF16)),
                                  _dot_nt(q1[:, lo1], st1.astype(BF16))], axis=1)
            st_s[2 * j] = dec[:, lo0] * st0 + _dot_tn(vb[:, lo0], k3[:, lo0])
            st_s[2 * j + 1] = dec[:, lo1] * st1 + _dot_tn(vb[:, lo1], k3[:, lo1])
            out.append(dict(a=a, o1=o1, vb=vb))
        return out

    def stage_out(c, s):
        for j in range(npair):
            vb = s[j]["vb"]
            am = jnp.where(causal2, s[j]["a"], 0.0).astype(BF16)
            o = s[j]["o1"] + _dot(am, blockdiag(vb[:, lo0], vb[:, lo1]))
            y = jnp.concatenate([_rms(o[:, lo0], g), _rms(o[:, lo1], g)], axis=1)
            hog = load(c, 3 * w + j * pw, 3 * w + (j + 1) * pw)
            store_y(c, j * pw, (j + 1) * pw, (y * (hog + hog * jnp.tanh(hog))).astype(BF16))

    dec, sta, th = {}, {}, []

    def do_decay(c):
        dec[c] = stage_decay(c)

    def do_state(c):
        sta[c] = stage_state(c, dec.pop(c))

    def do_out(c):
        stage_out(c, sta.pop(c))

    th.append(functools.partial(do_decay, 0))
    for c in range(nchunk):
        if c + 1 < nchunk:
            th.append(functools.partial(do_decay, c + 1))
        th.append(functools.partial(do_state, c))
        if c >= 1:
            th.append(functools.partial(do_out, c - 1))
    th.append(functools.partial(do_out, nchunk - 1))
    return th


def _hgrn_body(hb_ref, lbl_ref, g_ref, s0_ref, yb_ref, sfin_ref, st_s, *, blk, nchunk, nseq):
    streams = []
    for q in range(nseq):
        for h in range(B_HEADS):
            st_s[q, h] = s0_ref[q, h].T

        def store_y(c, lo, hi, y, q=q):
            yb_ref[q, c * blk:(c + 1) * blk, lo:hi] = y

        streams.append(_hgrn_thunks(
            nchunk, blk, lambda c, lo, hi, q=q: hb_ref[q, c * blk:(c + 1) * blk, lo:hi],
            store_y, st_s.at[q], lbl_ref[...], g_ref[...]))
    for group in zip(*streams):
        for t in group:
            t()
    for q in range(nseq):
        for h in range(B_HEADS):
            sfin_ref[q, h] = st_s[q, h].T


def _hgrn_cached(hb, lb_logits, g_out, s0, *, blk, nseq):
    bsz, t, _ = hb.shape
    body = functools.partial(_hgrn_body, blk=blk, nchunk=t // blk, nseq=nseq)
    st_spec = pl.BlockSpec((nseq, B_HEADS, B_KEY_DIM, B_VAL_DIM), lambda b: (b, 0, 0, 0))
    return pl.pallas_call(
        body,
        grid=(bsz // nseq,),
        in_specs=[pl.BlockSpec((nseq, t, _HB_W), lambda b: (b, 0, 0)),
                  _const_spec(lb_logits.shape), _const_spec((1, B_VAL_DIM)), st_spec],
        out_specs=[pl.BlockSpec((nseq, t, B_VAL_W), lambda b: (b, 0, 0)), st_spec],
        out_shape=[jax.ShapeDtypeStruct((bsz, t, B_VAL_W), BF16),
                   jax.ShapeDtypeStruct((bsz, B_HEADS, B_KEY_DIM, B_VAL_DIM), F32)],
        scratch_shapes=[pltpu.VMEM((nseq, B_HEADS, B_VAL_DIM, B_KEY_DIM), F32)],
        compiler_params=pltpu.CompilerParams(
            dimension_semantics=("parallel",), vmem_limit_bytes=_VMEM_LIMIT),
        name="hgrn2",
    )(hb, lb_logits, g_out, s0)


def _inproj_body(x_ref, g_ref, w_ref, qa_ref, kv_ref, hb_ref, gg_ref):
    h = _rms(x_ref[...], g_ref[...]).astype(BF16)
    step = 512

    def mm(lo, width):
        return _dot(h, w_ref[:, lo:lo + width])

    for j in range(0, A_Q_W, step):
        qa_ref[:, j:j + step] = mm(_QA0 + j, step).astype(BF16)
    kv_ref[...] = mm(_KV0, 2 * A_KV_W)
    for j in range(0, _HB_W, step):
        hb_ref[:, j:j + step] = mm(_HB0 + j, step)
    for j in range(0, 2 * D_MODEL, step):
        gg_ref[:, j:j + step] = mm(_GG0 + j, step).astype(BF16)


def _inproj(x2d, g, w_bf, tm):
    n = x2d.shape[0]
    row = lambda w: pl.BlockSpec((tm, w), lambda i: (i, 0))
    return pl.pallas_call(
        _inproj_body,
        grid=(n // tm,),
        in_specs=[row(D_MODEL), _const_spec((1, D_MODEL)), _const_spec((D_MODEL, IN_COLS))],
        out_specs=[row(A_Q_W), row(2 * A_KV_W), row(_HB_W), row(2 * D_MODEL)],
        out_shape=[
            jax.ShapeDtypeStruct((n, A_Q_W), BF16),
            jax.ShapeDtypeStruct((n, 2 * A_KV_W), F32),
            jax.ShapeDtypeStruct((n, _HB_W), F32),
            jax.ShapeDtypeStruct((n, 2 * D_MODEL), BF16),
        ],
        compiler_params=pltpu.CompilerParams(
            dimension_semantics=("parallel",), vmem_limit_bytes=_VMEM_LIMIT),
        name="inproj",
    )(x2d, g, w_bf)


def _mixer_body(x_ref, gpre_ref, w_ref, lbl_ref, ghg_ref, bias_ref, sink_ref,
                kv_ref, gg_ref, ya_ref, yb_ref, sfin_ref,
                h_s, qa_s, kvb_s, kvw_s, hb_s, vb_s, st_s, *, tm, tiles_per_seq):
    s = pl.program_id(0)
    cur = s % 2
    prv = 1 - cur
    tib = (s + tiles_per_seq - 1) % tiles_per_seq
    nchunk = tm // CHUNK
    lk = WINDOW + CHUNK
    piece = 256

    @pl.when(s == 0)
    def _():
        qa_s[...] = jnp.zeros(qa_s.shape, BF16)
        kvb_s[...] = jnp.zeros(kvb_s.shape, BF16)
        kvw_s[...] = jnp.zeros(kvw_s.shape, BF16)
        hb_s[...] = jnp.zeros(hb_s.shape, F32)
        vb_s[...] = jnp.zeros(vb_s.shape, BF16)

    @pl.when((s == 0) | (tib == 0))
    def _():
        st_s[...] = jnp.zeros(st_s.shape, F32)

    kvw_s[0:WINDOW, :] = kvw_s[tm:tm + WINDOW, :]
    kvw_s[WINDOW:WINDOW + tm, :] = kvb_s[prv]

    def dense_piece(lo, width):
        z = _dot(h_s[...], w_ref[:, lo:lo + width])
        if lo < _KV0:
            qa_s[cur, :, lo:lo + width] = z.astype(BF16)
        elif lo < _HB0:
            kv_ref[...] = z
            kvb_s[cur] = z.astype(BF16)
        elif _HBV0 <= lo < _HBV0 + B_VAL_W:
            vb_s[:, lo - _HBV0:lo - _HBV0 + width] = z.astype(BF16)
        elif lo < _GG0:
            hb_s[:, lo - _HB0:lo - _HB0 + width] = z
        else:
            gg_ref[:, lo - _GG0:lo - _GG0 + width] = z.astype(BF16)

    def pieces(lo, hi):
        return [functools.partial(dense_piece, c, min(piece, hi - c)) for c in range(lo, hi, piece)]

    dense_hb = pieces(_HB0, _GG0)
    dense_rest = pieces(_QA0, _KV0) + pieces(_KV0, _HB0) + pieces(_GG0, IN_COLS)

    def store_ya(c, o):
        ya_ref[c * CHUNK:(c + 1) * CHUNK, :] = o

    def store_yb(c, lo, hi, y):
        yb_ref[c * CHUNK:(c + 1) * CHUNK, lo:hi] = y

    def bias_strip(c, k, j):
        var = jnp.clip(WINDOW // CHUNK - (tib * nchunk + c), 0, WINDOW // CHUNK)
        return bias_ref[var, k, :, j:j + 128]

    attn = _attn_thunks(
        nchunk, CHUNK,
        load_q=lambda c: qa_s[prv, c * CHUNK:(c + 1) * CHUNK, :],
        load_kw=lambda c: kvw_s[c * CHUNK:c * CHUNK + lk, :],
        bias_strip=bias_strip, sink_strip=lambda k, j: sink_ref[k, :, j:j + 128], store_o=store_ya)
    v0 = _HBV0 - _HB0

    def load_hb(c, lo, hi):
        rows = slice(c * CHUNK, (c + 1) * CHUNK)
        if v0 <= lo < v0 + B_VAL_W:
            return vb_s[rows, lo - v0:hi - v0]
        return hb_s[rows, lo:hi]

    hgrn = _hgrn_thunks(nchunk, CHUNK, load_hb, store_yb, st_s, lbl_ref[...], ghg_ref[...])
    for t in attn[:2]:
        t()
    h_s[...] = _rms(x_ref[...], gpre_ref[...]).astype(BF16)
    for t in _interleave(dense_rest, hgrn) + _interleave(dense_hb, attn[2:]):
        t()

    @pl.when((tib == tiles_per_seq - 1) & (s > 0))
    def _():
        for h in range(B_HEADS):
            sfin_ref[h] = st_s[h].T


def _mixer(x2d, g_pre, w_bf, lb_logits, g_hgrn, bias_ext, sink_ext, *, tm, seq):
    n = x2d.shape[0]
    nt = n // tm
    tiles_per_seq = seq // tm
    dense_row = lambda w: pl.BlockSpec((tm, w), lambda s: (jnp.minimum(s, nt - 1), 0))
    lag_row = lambda w: pl.BlockSpec((tm, w), lambda s: (jnp.maximum(s - 1, 0), 0))
    body = functools.partial(_mixer_body, tm=tm, tiles_per_seq=tiles_per_seq)
    return pl.pallas_call(
        body,
        grid=(nt + 1,),
        in_specs=[dense_row(D_MODEL), _const_spec((1, D_MODEL)), _const_spec((D_MODEL, IN_COLS)),
                  _const_spec(lb_logits.shape), _const_spec((1, B_VAL_DIM)), _const_spec(bias_ext.shape),
                  _const_spec(sink_ext.shape)],
        out_specs=[dense_row(2 * A_KV_W), dense_row(2 * D_MODEL), lag_row(A_Q_W), lag_row(B_VAL_W),
                   pl.BlockSpec((None, B_HEADS, B_KEY_DIM, B_VAL_DIM),
                                lambda s: (jnp.maximum(s - 1, 0) // tiles_per_seq, 0, 0, 0))],
        out_shape=[jax.ShapeDtypeStruct((n, 2 * A_KV_W), F32),
                   jax.ShapeDtypeStruct((n, 2 * D_MODEL), BF16),
                   jax.ShapeDtypeStruct((n, A_Q_W), BF16),
                   jax.ShapeDtypeStruct((n, B_VAL_W), BF16),
                   jax.ShapeDtypeStruct((n // seq, B_HEADS, B_KEY_DIM, B_VAL_DIM), F32)],
        scratch_shapes=[pltpu.VMEM((tm, D_MODEL), BF16),
                        pltpu.VMEM((2, tm, A_Q_W), BF16),
                        pltpu.VMEM((2, tm, 2 * A_KV_W), BF16),
                        pltpu.VMEM((WINDOW + tm, 2 * A_KV_W), BF16),
                        pltpu.VMEM((tm, _HB_W), F32),
                        pltpu.VMEM((tm, B_VAL_W), BF16),
                        pltpu.VMEM((B_HEADS, B_VAL_DIM, B_KEY_DIM), F32)],
        compiler_params=pltpu.CompilerParams(
            dimension_semantics=("arbitrary",), vmem_limit_bytes=_VMEM_LIMIT),
        name="mixer",
    )(x2d, g_pre, w_bf, lb_logits, g_hgrn, bias_ext, sink_ext)


def _merge_body(ya_ref, yb_ref, gg_ref, x_ref, wa_ref, wb_ref, wo_ref, g_ref, o_ref, *, tm, strip):
    def branches(r):
        rows = slice(r * strip, (r + 1) * strip)
        ga = gg_ref[rows, 0:D_MODEL].astype(F32)
        gb = gg_ref[rows, D_MODEL:2 * D_MODEL].astype(F32)
        mix = (_sigmoid_of_twice(ga) * _dot(ya_ref[rows, :], wa_ref[...])
               + _sigmoid_of_twice(gb) * _dot(yb_ref[rows, :], wb_ref[...]))
        return mix.astype(BF16)

    def project(r, mix):
        rows = slice(r * strip, (r + 1) * strip)
        o_ref[rows, :] = x_ref[rows, :] + _rms(_dot(mix, wo_ref[...]), g_ref[...])

    nstrip = tm // strip
    mix = branches(0)
    for r in range(nstrip):
        nxt = branches(r + 1) if r + 1 < nstrip else None
        project(r, mix)
        mix = nxt


def _merge(ya, yb, gg, x2d, wa, wb, wo, g, tm):
    n = x2d.shape[0]
    row = lambda w: pl.BlockSpec((tm, w), lambda i: (i, 0))
    wspec = _const_spec((D_MODEL, D_MODEL))
    return pl.pallas_call(
        functools.partial(_merge_body, tm=tm, strip=min(tm, 256)),
        grid=(n // tm,),
        in_specs=[row(A_Q_W), row(B_VAL_W), row(2 * D_MODEL), row(D_MODEL),
                  wspec, wspec, wspec, _const_spec((1, D_MODEL))],
        out_specs=row(D_MODEL),
        out_shape=jax.ShapeDtypeStruct((n, D_MODEL), F32),
        compiler_params=pltpu.CompilerParams(
            dimension_semantics=("parallel",), vmem_limit_bytes=_VMEM_LIMIT),
        name="merge",
    )(ya, yb, gg, x2d, wa, wb, wo, g)


def _gelu_tanh(x):
    c = math.sqrt(2.0 / math.pi)
    return 0.5 * x * (1.0 + jnp.tanh(c * (x + 0.044715 * (x * x * x))))


def _ffn_up(x, gpre_ref, wup_ref):
    hf = _rms(x, gpre_ref[...]).astype(BF16)
    return _dot(hf, wup_ref[:, 0:D_FF]), _dot(hf, wup_ref[:, D_FF:2 * D_FF])


def _ffn_down(x, pe, a, u, a1, a2, wconv_ref, bconv_ref, wdown_ref, gpost_ref, wple_ref, wgate_ref):
    ac = bconv_ref[...] + a2 * wconv_ref[0:1, :] + a1 * wconv_ref[1:2, :] + a * wconv_ref[2:3, :]
    gl = (_gelu_tanh(ac) * u).astype(BF16)
    x2 = x + _rms(_dot(gl, wdown_ref[...]), gpost_ref[...])
    pex = _dot(pe.astype(BF16), wple_ref[...])
    gate = _sigmoid_of_twice(_dot(x2.astype(BF16), wgate_ref[...]))
    return x2 + pex * gate


def _ffn_tile(x, pe, shifted, gpre_ref, wup_ref, *rest):
    a, u = _ffn_up(x, gpre_ref, wup_ref)
    a1, a2 = shifted(a)
    return _ffn_down(x, pe, a, u, a1, a2, *rest), a


def _ffn_body(x_ref, pe_ref, *refs, tm, strip):
    (gpre_ref, wup_ref), rest, (o_ref, tail_ref, carry_s) = refs[:2], refs[2:8], refs[8:]
    i = pl.program_id(1)

    @pl.when(i == 0)
    def _():
        carry_s[...] = jnp.zeros((CONV_W - 1, D_FF), F32)

    row = lax.broadcasted_iota(jnp.int32, (strip, D_FF), 0)
    nstrip = tm // strip
    rows = [slice(r * strip, (r + 1) * strip) for r in range(nstrip)]
    hist = carry_s[...]
    au = _ffn_up(x_ref[rows[0], :], gpre_ref, wup_ref)
    for r in range(nstrip):
        nxt = _ffn_up(x_ref[rows[r + 1], :], gpre_ref, wup_ref) if r + 1 < nstrip else None
        a, u = au
        c0, c1 = hist[0:1, :], hist[1:2, :]
        a1 = jnp.where(row == 0, c1, pltpu.roll(a, 1, 0))
        a2 = jnp.where(row == 0, c0, jnp.where(row == 1, c1, pltpu.roll(a, 2, 0)))
        o_ref[rows[r], :] = _ffn_down(x_ref[rows[r], :], pe_ref[rows[r], :], a, u, a1, a2, *rest)
        hist = a[strip - (CONV_W - 1):strip, :]
        au = nxt
    carry_s[...] = hist
    tail_ref[...] = hist


def _ffn_cached_body(x_ref, pe_ref, prev_ref, *refs, t, nseq):
    w_refs, (o_ref, tail_ref) = refs[:8], refs[8:]
    pos = lax.broadcasted_iota(jnp.int32, (nseq * t, D_FF), 0) & (t - 1)

    def history(j):
        return jnp.concatenate([jnp.broadcast_to(prev_ref[q, j:j + 1, :], (t, D_FF)) for q in range(nseq)],
                               axis=0)

    def shifted(a):
        p0, p1 = history(0), history(1)
        a1 = jnp.where(pos == 0, p1, pltpu.roll(a, 1, 0))
        a2 = jnp.where(pos == 0, p0, jnp.where(pos == 1, p1, pltpu.roll(a, 2, 0)))
        return a1, a2

    x = x_ref[...].reshape(nseq * t, D_MODEL)
    pe = pe_ref[...].reshape(nseq * t, PLE_DIM)
    out, a = _ffn_tile(x, pe, shifted, *w_refs)
    o_ref[...] = out.reshape(nseq, t, D_MODEL)
    for q in range(nseq):
        tail_ref[q] = a[(q + 1) * t - (CONV_W - 1):(q + 1) * t, :]


def _ffn_weight_specs():
    return [_const_spec((1, D_MODEL)), _const_spec((D_MODEL, 2 * D_FF)),
            _const_spec((CONV_W, D_FF)), _const_spec((1, D_FF)),
            _const_spec((D_FF, D_MODEL)), _const_spec((1, D_MODEL)),
            _const_spec((PLE_DIM, D_MODEL)), _const_spec((D_MODEL, D_MODEL))]


def _ffn(x, pe, weights, tm):
    bsz, t, _ = x.shape
    tail_spec = pl.BlockSpec((None, CONV_W - 1, D_FF), lambda b, i: (b, 0, 0))
    return pl.pallas_call(
        functools.partial(_ffn_body, tm=tm, strip=min(tm, 256)),
        grid=(bsz, t // tm),
        in_specs=[pl.BlockSpec((None, tm, D_MODEL), lambda b, i: (b, i, 0)),
                  pl.BlockSpec((None, tm, PLE_DIM), lambda b, i: (b, i, 0))] + _ffn_weight_specs(),
        out_specs=[pl.BlockSpec((None, tm, D_MODEL), lambda b, i: (b, i, 0)), tail_spec],
        out_shape=[jax.ShapeDtypeStruct((bsz, t, D_MODEL), F32),
                   jax.ShapeDtypeStruct((bsz, CONV_W - 1, D_FF), F32)],
        scratch_shapes=[pltpu.VMEM((CONV_W - 1, D_FF), F32)],
        compiler_params=pltpu.CompilerParams(
            dimension_semantics=("parallel", "arbitrary"), vmem_limit_bytes=_VMEM_LIMIT),
        name="convffn",
    )(x, pe, *weights)


def _ffn_cached(x, pe, prev, weights, nseq):
    bsz, t, _ = x.shape
    tail_spec = pl.BlockSpec((nseq, CONV_W - 1, D_FF), lambda b: (b, 0, 0))
    return pl.pallas_call(
        functools.partial(_ffn_cached_body, t=t, nseq=nseq),
        grid=(bsz // nseq,),
        in_specs=[pl.BlockSpec((nseq, t, D_MODEL), lambda b: (b, 0, 0)),
                  pl.BlockSpec((nseq, t, PLE_DIM), lambda b: (b, 0, 0)), tail_spec] + _ffn_weight_specs(),
        out_specs=[pl.BlockSpec((nseq, t, D_MODEL), lambda b: (b, 0, 0)), tail_spec],
        out_shape=[jax.ShapeDtypeStruct((bsz, t, D_MODEL), F32),
                   jax.ShapeDtypeStruct((bsz, CONV_W - 1, D_FF), F32)],
        compiler_params=pltpu.CompilerParams(
            dimension_semantics=("parallel",), vmem_limit_bytes=_VMEM_LIMIT),
        name="convffn_cached",
    )(x, pe, prev, *weights)


def _t5_bucket(rel):
    nb = NUM_BUCKETS // 2
    ret = jnp.where(rel > 0, nb, 0)
    n = jnp.abs(rel)
    max_exact = nb // 2
    large = max_exact + (jnp.log(jnp.maximum(n, max_exact).astype(jnp.float32) / max_exact)
                         / math.log(MAX_DISTANCE / max_exact) * (nb - max_exact)).astype(jnp.int32)
    large = jnp.minimum(large, nb - 1)
    return ret + jnp.where(n < max_exact, n, large)


def _bias_body(table_ref, bk_ref, o_ref, *, nvar):
    bk = bk_ref[...]
    row = lax.broadcasted_iota(jnp.int32, bk.shape, 0)
    for k in range(A_KV_HEADS):
        acc = jnp.zeros(bk.shape, F32)
        for b in range(NUM_BUCKETS):
            acc = jnp.where(bk == b, table_ref[k, b:b + 1, :], acc)
        for v in range(nvar):
            o_ref[v, k] = jnp.where(row < v * CHUNK, NEG_INF, acc)


def _bias_ext(table, sinks, lq, lk, nvar):
    q_pos = jnp.arange(lq) + WINDOW
    k_pos = jnp.arange(lk)
    buckets = _t5_bucket(k_pos[:, None] - q_pos[None, :]).astype(jnp.int32)
    bk = jnp.tile(buckets, (1, A_GROUP))
    tab = jnp.repeat(table.astype(F32).reshape(NUM_BUCKETS, A_KV_HEADS, A_GROUP), lq, axis=2)
    tab = jnp.transpose(tab, (1, 0, 2))
    snk = jnp.repeat(sinks.astype(F32).reshape(A_KV_HEADS, 1, A_GROUP), lq, axis=2)
    vmem = pl.BlockSpec(memory_space=pltpu.VMEM)
    bias = pl.pallas_call(
        functools.partial(_bias_body, nvar=nvar),
        in_specs=[vmem, vmem],
        out_specs=vmem,
        out_shape=jax.ShapeDtypeStruct((nvar, A_KV_HEADS, lk, A_GROUP * lq), F32),
        name="relbias",
    )(tab, bk)
    return bias, snk


def _ffn_weights(w):
    return [w[k] for k in ("g_pre_ffn", "w_up", "w_conv", "b_conv", "w_down", "g_post_ffn", "w_ple", "w_ple_gate")]


def _prompt_layer(x, pe, w, *, tm_mix, tm_tok, tm_ffn):
    bsz, t, _ = x.shape
    n = bsz * t
    x2d = x.reshape(n, D_MODEL)
    bias_ext, sink_ext = _bias_ext(w["rel_table"], w["sinks"], CHUNK, WINDOW + CHUNK, WINDOW // CHUNK + 1)
    kv, gg, ya, yb, s_fin = _mixer(x2d, w["g_pre_mix"], w["w_in"], w["lb_logits"], w["g_hgrn_out"],
                                   bias_ext, sink_ext, tm=tm_mix, seq=t)
    x1 = _merge(ya, yb, gg, x2d, w["w_br_a"], w["w_br_b"], w["w_out"], w["g_post_mix"], tm_tok)
    y, conv_tail = _ffn(x1.reshape(bsz, t, D_MODEL), pe, _ffn_weights(w), tm_ffn)
    return y, kv.reshape(bsz, t, 2 * A_KV_W), s_fin, conv_tail


def _sample_layer(x, pe, kv_prev, s_prev, conv_prev, w, *, tm_tok, nseq_mix):
    bsz, t, _ = x.shape
    n = bsz * t
    x2d = x.reshape(n, D_MODEL)
    qa, kv, hb, gg = _inproj(x2d, w["g_pre_mix"], w["w_in"], tm_tok)
    bias_ext, sink_ext = _bias_ext(w["rel_table"], w["sinks"], t, WINDOW + t, 1)
    kv3 = kv.reshape(bsz, t, 2 * A_KV_W)
    ya = _attention_cached(qa.reshape(bsz, t, A_Q_W), kv3, kv_prev, bias_ext, sink_ext, lq=t,
                           nseq=nseq_mix)
    yb, s_fin = _hgrn_cached(hb.reshape(bsz, t, _HB_W), w["lb_logits"], w["g_hgrn_out"], s_prev,
                             blk=t, nseq=nseq_mix)
    x1 = _merge(ya.reshape(n, A_Q_W), yb.reshape(n, B_VAL_W), gg, x2d,
                w["w_br_a"], w["w_br_b"], w["w_out"], w["g_post_mix"], tm_tok)
    y, conv_tail = _ffn_cached(x1.reshape(bsz, t, D_MODEL), pe, conv_prev, _ffn_weights(w), tm_tok // t)
    return y, kv3, s_fin, conv_tail


def _scale_in_cols(w_in):
    h = B_KEY_W
    scale = jnp.concatenate([
        jnp.full((A_Q_W,), A_HEAD_DIM ** -0.5, F32), jnp.ones((2 * A_KV_W,), F32),
        jnp.full((2 * h,), 0.5, F32), jnp.ones((B_VAL_W,), F32), jnp.full((B_VAL_W,), 0.5, F32),
        jnp.full((2 * D_MODEL,), 0.5, F32)])
    return w_in * scale[None, :]


def kernel(x_prompt, x_sample, cache_win_k, cache_win_v, state_hgrn, cache_ffn_conv, p_prompt, p_sample,
           rel_bias_table, lb_logits, g_pre_mix, w_in, attn_sinks, g_hgrn_out, w_br_a, w_br_b, w_out,
           g_post_mix, g_pre_ffn, w_up, w_conv, b_conv, w_down, g_post_ffn, w_ple, w_ple_gate):
    bsz, seq, _ = x_prompt.shape
    dbsz, dseq, _ = x_sample.shape
    w = {
        "rel_table": rel_bias_table, "sinks": attn_sinks[0], "lb_logits": lb_logits.astype(F32),
        "g_pre_mix": g_pre_mix[0][None, :], "w_in": _scale_in_cols(w_in[0]).astype(BF16),
        "g_hgrn_out": g_hgrn_out[0][None, :],
        "w_br_a": w_br_a[0].astype(BF16), "w_br_b": w_br_b[0].astype(BF16), "w_out": w_out[0].astype(BF16),
        "g_post_mix": g_post_mix[0][None, :], "g_pre_ffn": g_pre_ffn[0][None, :],
        "w_up": w_up[0].astype(BF16), "w_conv": w_conv[0], "b_conv": b_conv[0][None, :],
        "w_down": w_down[0].astype(BF16), "g_post_ffn": g_post_ffn[0][None, :],
        "w_ple": w_ple[0].astype(BF16), "w_ple_gate": (0.5 * w_ple_gate[0]).astype(BF16),
    }
    yp, kvp, sp, cp = _prompt_layer(x_prompt, p_prompt[0], w, tm_mix=512, tm_tok=1024, tm_ffn=512)
    wc = cache_win_k.shape[2]
    kv_cache = jnp.concatenate([cache_win_k[0].reshape(dbsz, wc, A_KV_W),
                                cache_win_v[0].reshape(dbsz, wc, A_KV_W)], axis=-1)
    ys, kvs, ss, cs = _sample_layer(x_sample, p_sample[0], kv_cache, state_hgrn[0], cache_ffn_conv[0], w,
                                    tm_tok=256, nseq_mix=8)
    keep = min(WINDOW, seq)

    def heads(a):
        return a.reshape(a.shape[0], a.shape[1], A_KV_HEADS, A_HEAD_DIM)[None]

    return (yp, ys,
            heads(kvp[:, seq - keep:, 0:A_KV_W]), heads(kvp[:, seq - keep:, A_KV_W:]),
            sp[None], cp[None],
            heads(kvs[:, :, 0:A_KV_W]), heads(kvs[:, :, A_KV_W:]),
            ss[None], cs[None])
```

```python
import functools
import math

import jax
import jax.numpy as jnp
from jax import lax
from jax.experimental import pallas as pl
from jax.experimental.pallas import tpu as pltpu

D_MODEL = 1024
CHUNK = 64
A_HEADS = 16
A_KV_HEADS = 2
A_HEAD_DIM = 64
A_GROUP = A_HEADS // A_KV_HEADS
WINDOW = 128
A_Q_W = A_HEADS * A_HEAD_DIM
A_KV_W = A_KV_HEADS * A_HEAD_DIM
NUM_BUCKETS = 32
MAX_DISTANCE = 128
B_HEADS = 8
B_KEY_DIM = 128
B_VAL_DIM = D_MODEL // B_HEADS
B_KEY_W = B_HEADS * B_KEY_DIM
B_VAL_W = B_HEADS * B_VAL_DIM
D_FF = 2816
CONV_W = 3
PLE_DIM = 256
EPS = 1e-6
NEG_INF = -1e30

_QA0 = 0
_KV0 = A_Q_W
_HB0 = _KV0 + 2 * A_KV_W
_GG0 = _HB0 + 2 * B_KEY_W + 2 * B_VAL_W
IN_COLS = _GG0 + 2 * D_MODEL
_HB_W = _GG0 - _HB0
_HBV0 = _HB0 + 2 * B_KEY_W

_VMEM_LIMIT = 56 * 1024 * 1024

BF16 = jnp.bfloat16
F32 = jnp.float32


def _const_spec(shape):
    nd = len(shape)
    return pl.BlockSpec(shape, lambda *_: (0,) * nd, pipeline_mode=pl.Buffered(1))


def _rms(x, g):
    ms = jnp.mean(x * x, axis=-1, keepdims=True)
    return x * lax.rsqrt(ms + EPS) * g


def _sigmoid_of_twice(hx):
    return 0.5 * jnp.tanh(hx) + 0.5


def _dot(a, b):
    return jnp.dot(a, b, preferred_element_type=F32)


def _dot_nt(a, b):
    return lax.dot_general(a, b, (((1,), (1,)), ((), ())), preferred_element_type=F32)


def _dot_tn(a, b):
    return lax.dot_general(a, b, (((0,), (0,)), ((), ())), preferred_element_type=F32)


def _interleave(a, b):
    out, nb = [], 0
    for i, t in enumerate(a):
        out.append(t)
        want = ((i + 1) * len(b)) // len(a)
        out.extend(b[nb:want])
        nb = want
    return out + b[nb:]


def _attn_thunks(nchunk, lq, load_q, load_kw, bias_strip, sink_strip, store_o):
    def scores(c):
        qc = load_q(c)
        kw = load_kw(c)
        st = []
        for k in range(A_KV_HEADS):
            qs = jnp.concatenate(
                [qc[:, (k * A_GROUP + g) * A_HEAD_DIM:(k * A_GROUP + g + 1) * A_HEAD_DIM]
                 for g in range(A_GROUP)], axis=0)
            st.append(_dot_nt(kw[:, k * A_HEAD_DIM:(k + 1) * A_HEAD_DIM], qs))
        return dict(kw=kw, st=st)

    def softmax(c, s):
        ot, rden = [], []
        for k in range(A_KV_HEADS):
            ps, rs = [], []
            for j in range(0, A_GROUP * lq, 128):
                t = s["st"][k][:, j:j + 128] + bias_strip(c, k, j)
                sink = sink_strip(k, j)
                m = jnp.maximum(jnp.max(t, axis=0, keepdims=True), sink)
                p = jnp.exp(t - m)
                rs.append(1.0 / (jnp.sum(p, axis=0, keepdims=True) + jnp.exp(sink - m)))
                ps.append(p.astype(BF16))
            rden.append(jnp.concatenate(rs, axis=1))
            vv = s["kw"][:, A_KV_W + k * A_HEAD_DIM:A_KV_W + (k + 1) * A_HEAD_DIM]
            ot.append(_dot_tn(vv, jnp.concatenate(ps, axis=1)))
        return dict(ot=ot, rden=rden)

    def out(c, s):
        outs = []
        for k in range(A_KV_HEADS):
            o = (s["ot"][k] * s["rden"][k]).T
            outs.append(jnp.concatenate([o[g * lq:(g + 1) * lq, :] for g in range(A_GROUP)], axis=1))
        store_o(c, jnp.concatenate(outs, axis=1).astype(BF16))

    ahead = 2
    sc, sm, th = {}, {}, []

    def do_scores(c):
        sc[c] = scores(c)

    def do_softmax(c):
        sm[c] = softmax(c, sc.pop(c))

    def do_out(c):
        out(c, sm.pop(c))

    for c in range(min(ahead, nchunk)):
        th.append(functools.partial(do_scores, c))
    for c in range(nchunk):
        if c + ahead < nchunk:
            th.append(functools.partial(do_scores, c + ahead))
        th.append(functools.partial(do_softmax, c))
        if c >= 1:
            th.append(functools.partial(do_out, c - 1))
    th.append(functools.partial(do_out, nchunk - 1))
    return th


def _attn_body(q_ref, kvc_ref, kvp_ref, bias_ref, sink_ref, o_ref, kv_s, *, lq, lk, nchunk, tq, nseq):
    streams = []
    for n in range(nseq):
        kv_s[n, 0:WINDOW, :] = kvp_ref[n].astype(BF16)
        kv_s[n, WINDOW:WINDOW + tq, :] = kvc_ref[n].astype(BF16)

        def store_o(c, o, n=n):
            o_ref[n, c * lq:(c + 1) * lq, :] = o

        streams.append(_attn_thunks(
            nchunk, lq,
            load_q=lambda c, n=n: q_ref[n, c * lq:(c + 1) * lq, :],
            load_kw=lambda c, n=n: kv_s[n, c * lq:c * lq + lk, :],
            bias_strip=lambda c, k, j: bias_ref[0, k, :, j:j + 128],
            sink_strip=lambda k, j: sink_ref[k, :, j:j + 128],
            store_o=store_o))
    for group in zip(*streams):
        for t in group:
            t()


def _attention_cached(qa, kv, kv_prev, bias_ext, sink_ext, *, lq, nseq):
    bsz, t, _ = qa.shape
    lk = WINDOW + lq
    body = functools.partial(_attn_body, lq=lq, lk=lk, nchunk=t // lq, tq=t, nseq=nseq)
    return pl.pallas_call(
        body,
        grid=(bsz // nseq,),
        in_specs=[
            pl.BlockSpec((nseq, t, A_Q_W), lambda b: (b, 0, 0)),
            pl.BlockSpec((nseq, t, 2 * A_KV_W), lambda b: (b, 0, 0)),
            pl.BlockSpec((nseq, WINDOW, 2 * A_KV_W), lambda b: (b, 0, 0)),
            _const_spec(bias_ext.shape), _const_spec(sink_ext.shape),
        ],
        out_specs=pl.BlockSpec((nseq, t, A_Q_W), lambda b: (b, 0, 0)),
        out_shape=jax.ShapeDtypeStruct((bsz, t, A_Q_W), BF16),
        scratch_shapes=[pltpu.VMEM((nseq, WINDOW + t, 2 * A_KV_W), BF16)],
        compiler_params=pltpu.CompilerParams(
            dimension_semantics=("parallel",), vmem_limit_bytes=_VMEM_LIMIT),
        name="attention",
    )(qa, kv, kv_prev, bias_ext, sink_ext)


def _cumsum_rows_scan(x):
    n = x.shape[0]
    row = lax.broadcasted_iota(jnp.int32, x.shape, 0)
    s = 1
    while s < n:
        x = x + jnp.where(row >= s, pltpu.roll(x, s, 0), 0.0)
        s *= 2
    return x


def _hgrn_thunks(nchunk, blk, load, store_y, st_s, lbl, g):
    e = jnp.exp(lbl - jnp.max(lbl, axis=0, keepdims=True))
    lb = e[0:1, :] / jnp.sum(e, axis=0, keepdims=True)
    fa = 0.5 * (1.0 + lb)
    fb = 0.5 * (1.0 - lb)
    ri2 = lax.broadcasted_iota(jnp.int32, (blk, 2 * blk), 0)
    ci2 = lax.broadcasted_iota(jnp.int32, (blk, 2 * blk), 1)
    causal2 = ri2 >= (ci2 & (blk - 1))
    mid = blk // 2
    w = B_KEY_W
    pw = 2 * B_KEY_DIM
    npair = B_HEADS // 2
    ps = [slice(j * pw, (j + 1) * pw) for j in range(npair)]

    def blockdiag(x0, x1):
        z = jnp.zeros_like(x0)
        return jnp.concatenate([jnp.concatenate([x0, z], axis=1), jnp.concatenate([z, x1], axis=1)], axis=0)

    lo0 = slice(0, B_KEY_DIM)
    lo1 = slice(B_KEY_DIM, pw)

    def stage_decay(c):
        out = []
        for j in range(npair):
            bt = fb[:, ps[j]] * jnp.tanh(load(c, w + j * pw, w + (j + 1) * pw))
            f = fa[:, ps[j]] + bt
            cum = _cumsum_rows_scan(jnp.log2(f))
            out.append(dict(kk=fb[:, ps[j]] - bt, cum=cum))
        return out

    def stage_state(c, s):
        out = []
        for j in range(npair):
            cum = s[j]["cum"]
            hq = load(c, j * pw, (j + 1) * pw)
            qs = hq + hq * jnp.tanh(hq)
            b_last = cum[blk - 1:blk, :]
            b_mid = cum[mid:mid + 1, :]
            q2f = qs * jnp.exp2(cum - b_mid)
            k2f = s[j]["kk"] * jnp.exp2(b_mid - cum)
            q1 = (q2f * jnp.exp2(b_mid)).astype(BF16)
            k3 = (k2f * jnp.exp2(b_last - b_mid)).astype(BF16)
            q2 = q2f.astype(BF16)
            k2 = k2f.astype(BF16)
            vb = load(c, 2 * w + j * pw, 2 * w + (j + 1) * pw).astype(BF16)
            dec = jnp.exp2(b_last)
            a = _dot_nt(q2, blockdiag(k2[:, lo0], k2[:, lo1]))
            st0, st1 = st_s[2 * j], st_s[2 * j + 1]
            o1 = jnp.concatenate([_dot_nt(q1[:, lo0], st0.astype(BF16)),
                                  _dot_nt(q1[:, lo1], st1.astype(BF16))], axis=1)
            st_s[2 * j] = dec[:, lo0] * st0 + _dot_tn(vb[:, lo0], k3[:, lo0])
            st_s[2 * j + 1] = dec[:, lo1] * st1 + _dot_tn(vb[:, lo1], k3[:, lo1])
            out.append(dict(a=a, o1=o1, vb=vb))
        return out

    def stage_out(c, s):
        for j in range(npair):
            vb = s[j]["vb"]
            am = jnp.where(causal2, s[j]["a"], 0.0).astype(BF16)
            o = s[j]["o1"] + _dot(am, blockdiag(vb[:, lo0], vb[:, lo1]))
            y = jnp.concatenate([_rms(o[:, lo0], g), _rms(o[:, lo1], g)], axis=1)
            hog = load(c, 3 * w + j * pw, 3 * w + (j + 1) * pw)
            store_y(c, j * pw, (j + 1) * pw, (y * (hog + hog * jnp.tanh(hog))).astype(BF16))

    dec, sta, th = {}, {}, []

    def do_decay(c):
        dec[c] = stage_decay(c)

    def do_state(c):
        sta[c] = stage_state(c, dec.pop(c))

    def do_out(c):
        stage_out(c, sta.pop(c))

    th.append(functools.partial(do_decay, 0))
    for c in range(nchunk):
        if c + 1 < nchunk:
            th.append(functools.partial(do_decay, c + 1))
        th.append(functools.partial(do_state, c))
        if c >= 1:
            th.append(functools.partial(do_out, c - 1))
    th.append(functools.partial(do_out, nchunk - 1))
    return th


def _hgrn_body(hb_ref, lbl_ref, g_ref, s0_ref, yb_ref, sfin_ref, st_s, *, blk, nchunk, nseq):
    streams = []
    for q in range(nseq):
        for h in range(B_HEADS):
            st_s[q, h] = s0_ref[q, h].T

        def store_y(c, lo, hi, y, q=q):
            yb_ref[q, c * blk:(c + 1) * blk, lo:hi] = y

        streams.append(_hgrn_thunks(
            nchunk, blk, lambda c, lo, hi, q=q: hb_ref[q, c * blk:(c + 1) * blk, lo:hi],
            store_y, st_s.at[q], lbl_ref[...], g_ref[...]))
    for group in zip(*streams):
        for t in group:
            t()
    for q in range(nseq):
        for h in range(B_HEADS):
            sfin_ref[q, h] = st_s[q, h].T


def _hgrn_cached(hb, lb_logits, g_out, s0, *, blk, nseq):
    bsz, t, _ = hb.shape
    body = functools.partial(_hgrn_body, blk=blk, nchunk=t // blk, nseq=nseq)
    st_spec = pl.BlockSpec((nseq, B_HEADS, B_KEY_DIM, B_VAL_DIM), lambda b: (b, 0, 0, 0))
    return pl.pallas_call(
        body,
        grid=(bsz // nseq,),
        in_specs=[pl.BlockSpec((nseq, t, _HB_W), lambda b: (b, 0, 0)),
                  _const_spec(lb_logits.shape), _const_spec((1, B_VAL_DIM)), st_spec],
        out_specs=[pl.BlockSpec((nseq, t, B_VAL_W), lambda b: (b, 0, 0)), st_spec],
        out_shape=[jax.ShapeDtypeStruct((bsz, t, B_VAL_W), BF16),
                   jax.ShapeDtypeStruct((bsz, B_HEADS, B_KEY_DIM, B_VAL_DIM), F32)],
        scratch_shapes=[pltpu.VMEM((nseq, B_HEADS, B_VAL_DIM, B_KEY_DIM), F32)],
        compiler_params=pltpu.CompilerParams(
            dimension_semantics=("parallel",), vmem_limit_bytes=_VMEM_LIMIT),
        name="hgrn2",
    )(hb, lb_logits, g_out, s0)


def _inproj_body(x_ref, g_ref, w_ref, qa_ref, kv_ref, hb_ref, gg_ref):
    h = _rms(x_ref[...], g_ref[...]).astype(BF16)
    step = 512

    def mm(lo, width):
        return _dot(h, w_ref[:, lo:lo + width])

    for j in range(0, A_Q_W, step):
        qa_ref[:, j:j + step] = mm(_QA0 + j, step).astype(BF16)
    kv_ref[...] = mm(_KV0, 2 * A_KV_W)
    for j in range(0, _HB_W, step):
        hb_ref[:, j:j + step] = mm(_HB0 + j, step)
    for j in range(0, 2 * D_MODEL, step):
        gg_ref[:, j:j + step] = mm(_GG0 + j, step).astype(BF16)


def _inproj(x2d, g, w_bf, tm):
    n = x2d.shape[0]
    row = lambda w: pl.BlockSpec((tm, w), lambda i: (i, 0))
    return pl.pallas_call(
        _inproj_body,
        grid=(n // tm,),
        in_specs=[row(D_MODEL), _const_spec((1, D_MODEL)), _const_spec((D_MODEL, IN_COLS))],
        out_specs=[row(A_Q_W), row(2 * A_KV_W), row(_HB_W), row(2 * D_MODEL)],
        out_shape=[
            jax.ShapeDtypeStruct((n, A_Q_W), BF16),
            jax.ShapeDtypeStruct((n, 2 * A_KV_W), F32),
            jax.ShapeDtypeStruct((n, _HB_W), F32),
            jax.ShapeDtypeStruct((n, 2 * D_MODEL), BF16),
        ],
        compiler_params=pltpu.CompilerParams(
            dimension_semantics=("parallel",), vmem_limit_bytes=_VMEM_LIMIT),
        name="inproj",
    )(x2d, g, w_bf)


def _mixer_body(x_ref, gpre_ref, w_ref, lbl_ref, ghg_ref, bias_ref, sink_ref,
                kv_ref, gg_ref, ya_ref, yb_ref, sfin_ref,
                h_s, qa_s, kvb_s, kvw_s, hb_s, vb_s, st_s, *, tm, tiles_per_seq):
    s = pl.program_id(0)
    cur = s % 2
    prv = 1 - cur
    tib = (s + tiles_per_seq - 1) % tiles_per_seq
    nchunk = tm // CHUNK
    lk = WINDOW + CHUNK
    piece = 256

    @pl.when(s == 0)
    def _():
        qa_s[...] = jnp.zeros(qa_s.shape, BF16)
        kvb_s[...] = jnp.zeros(kvb_s.shape, BF16)
        kvw_s[...] = jnp.zeros(kvw_s.shape, BF16)
        hb_s[...] = jnp.zeros(hb_s.shape, F32)
        vb_s[...] = jnp.zeros(vb_s.shape, BF16)

    @pl.when((s == 0) | (tib == 0))
    def _():
        st_s[...] = jnp.zeros(st_s.shape, F32)

    kvw_s[0:WINDOW, :] = kvw_s[tm:tm + WINDOW, :]
    kvw_s[WINDOW:WINDOW + tm, :] = kvb_s[prv]

    def dense_piece(lo, width):
        z = _dot(h_s[...], w_ref[:, lo:lo + width])
        if lo < _KV0:
            qa_s[cur, :, lo:lo + width] = z.astype(BF16)
        elif lo < _HB0:
            kv_ref[...] = z
            kvb_s[cur] = z.astype(BF16)
        elif _HBV0 <= lo < _HBV0 + B_VAL_W:
            vb_s[:, lo - _HBV0:lo - _HBV0 + width] = z.astype(BF16)
        elif lo < _GG0:
            hb_s[:, lo - _HB0:lo - _HB0 + width] = z
        else:
            gg_ref[:, lo - _GG0:lo - _GG0 + width] = z.astype(BF16)

    def pieces(lo, hi):
        return [functools.partial(dense_piece, c, min(piece, hi - c)) for c in range(lo, hi, piece)]

    dense_hb = pieces(_HB0, _GG0)
    dense_rest = pieces(_QA0, _KV0) + pieces(_KV0, _HB0) + pieces(_GG0, IN_COLS)

    def store_ya(c, o):
        ya_ref[c * CHUNK:(c + 1) * CHUNK, :] = o

    def store_yb(c, lo, hi, y):
        yb_ref[c * CHUNK:(c + 1) * CHUNK, lo:hi] = y

    def bias_strip(c, k, j):
        var = jnp.clip(WINDOW // CHUNK - (tib * nchunk + c), 0, WINDOW // CHUNK)
        return bias_ref[var, k, :, j:j + 128]

    attn = _attn_thunks(
        nchunk, CHUNK,
        load_q=lambda c: qa_s[prv, c * CHUNK:(c + 1) * CHUNK, :],
        load_kw=lambda c: kvw_s[c * CHUNK:c * CHUNK + lk, :],
        bias_strip=bias_strip, sink_strip=lambda k, j: sink_ref[k, :, j:j + 128], store_o=store_ya)
    v0 = _HBV0 - _HB0

    def load_hb(c, lo, hi):
        rows = slice(c * CHUNK, (c + 1) * CHUNK)
        if v0 <= lo < v0 + B_VAL_W:
            return vb_s[rows, lo - v0:hi - v0]
        return hb_s[rows, lo:hi]

    hgrn = _hgrn_thunks(nchunk, CHUNK, load_hb, store_yb, st_s, lbl_ref[...], ghg_ref[...])
    for t in attn[:2]:
        t()
    h_s[...] = _rms(x_ref[...], gpre_ref[...]).astype(BF16)
    for t in _interleave(dense_rest, hgrn) + _interleave(dense_hb, attn[2:]):
        t()

    @pl.when((tib == tiles_per_seq - 1) & (s > 0))
    def _():
        for h in range(B_HEADS):
            sfin_ref[h] = st_s[h].T


def _mixer(x2d, g_pre, w_bf, lb_logits, g_hgrn, bias_ext, sink_ext, *, tm, seq):
    n = x2d.shape[0]
    nt = n // tm
    tiles_per_seq = seq // tm
    dense_row = lambda w: pl.BlockSpec((tm, w), lambda s: (jnp.minimum(s, nt - 1), 0))
    lag_row = lambda w: pl.BlockSpec((tm, w), lambda s: (jnp.maximum(s - 1, 0), 0))
    body = functools.partial(_mixer_body, tm=tm, tiles_per_seq=tiles_per_seq)
    return pl.pallas_call(
        body,
        grid=(nt + 1,),
        in_specs=[dense_row(D_MODEL), _const_spec((1, D_MODEL)), _const_spec((D_MODEL, IN_COLS)),
                  _const_spec(lb_logits.shape), _const_spec((1, B_VAL_DIM)), _const_spec(bias_ext.shape),
                  _const_spec(sink_ext.shape)],
        out_specs=[dense_row(2 * A_KV_W), dense_row(2 * D_MODEL), lag_row(A_Q_W), lag_row(B_VAL_W),
                   pl.BlockSpec((None, B_HEADS, B_KEY_DIM, B_VAL_DIM),
                                lambda s: (jnp.maximum(s - 1, 0) // tiles_per_seq, 0, 0, 0))],
        out_shape=[jax.ShapeDtypeStruct((n, 2 * A_KV_W), F32),
                   jax.ShapeDtypeStruct((n, 2 * D_MODEL), BF16),
                   jax.ShapeDtypeStruct((n, A_Q_W), BF16),
                   jax.ShapeDtypeStruct((n, B_VAL_W), BF16),
                   jax.ShapeDtypeStruct((n // seq, B_HEADS, B_KEY_DIM, B_VAL_DIM), F32)],
        scratch_shapes=[pltpu.VMEM((tm, D_MODEL), BF16),
                        pltpu.VMEM((2, tm, A_Q_W), BF16),
                        pltpu.VMEM((2, tm, 2 * A_KV_W), BF16),
                        pltpu.VMEM((WINDOW + tm, 2 * A_KV_W), BF16),
                        pltpu.VMEM((tm, _HB_W), F32),
                        pltpu.VMEM((tm, B_VAL_W), BF16),
                        pltpu.VMEM((B_HEADS, B_VAL_DIM, B_KEY_DIM), F32)],
        compiler_params=pltpu.CompilerParams(
            dimension_semantics=("arbitrary",), vmem_limit_bytes=_VMEM_LIMIT),
        name="mixer",
    )(x2d, g_pre, w_bf, lb_logits, g_hgrn, bias_ext, sink_ext)


def _merge_body(ya_ref, yb_ref, gg_ref, x_ref, wa_ref, wb_ref, wo_ref, g_ref, o_ref, *, tm, strip):
    def branches(r):
        rows = slice(r * strip, (r + 1) * strip)
        ga = gg_ref[rows, 0:D_MODEL].astype(F32)
        gb = gg_ref[rows, D_MODEL:2 * D_MODEL].astype(F32)
        mix = (_sigmoid_of_twice(ga) * _dot(ya_ref[rows, :], wa_ref[...])
               + _sigmoid_of_twice(gb) * _dot(yb_ref[rows, :], wb_ref[...]))
        return mix.astype(BF16)

    def project(r, mix):
        rows = slice(r * strip, (r + 1) * strip)
        o_ref[rows, :] = x_ref[rows, :] + _rms(_dot(mix, wo_ref[...]), g_ref[...])

    nstrip = tm // strip
    mix = branches(0)
    for r in range(nstrip):
        nxt = branches(r + 1) if r + 1 < nstrip else None
        project(r, mix)
        mix = nxt


def _merge(ya, yb, gg, x2d, wa, wb, wo, g, tm):
    n = x2d.shape[0]
    row = lambda w: pl.BlockSpec((tm, w), lambda i: (i, 0))
    wspec = _const_spec((D_MODEL, D_MODEL))
    return pl.pallas_call(
        functools.partial(_merge_body, tm=tm, strip=min(tm, 512)),
        grid=(n // tm,),
        in_specs=[row(A_Q_W), row(B_VAL_W), row(2 * D_MODEL), row(D_MODEL),
                  wspec, wspec, wspec, _const_spec((1, D_MODEL))],
        out_specs=row(D_MODEL),
        out_shape=jax.ShapeDtypeStruct((n, D_MODEL), F32),
        compiler_params=pltpu.CompilerParams(
            dimension_semantics=("parallel",), vmem_limit_bytes=_VMEM_LIMIT),
        name="merge",
    )(ya, yb, gg, x2d, wa, wb, wo, g)


def _gelu_tanh(x):
    c = math.sqrt(2.0 / math.pi)
    return 0.5 * x * (1.0 + jnp.tanh(c * (x + 0.044715 * (x * x * x))))


def _ffn_up(x, gpre_ref, wup_ref):
    hf = _rms(x, gpre_ref[...]).astype(BF16)
    return _dot(hf, wup_ref[:, 0:D_FF]), _dot(hf, wup_ref[:, D_FF:2 * D_FF])


def _ffn_down(x, pe, a, u, a1, a2, wconv_ref, bconv_ref, wdown_ref, gpost_ref, wple_ref, wgate_ref):
    ac = bconv_ref[...] + a2 * wconv_ref[0:1, :] + a1 * wconv_ref[1:2, :] + a * wconv_ref[2:3, :]
    gl = (_gelu_tanh(ac) * u).astype(BF16)
    x2 = x + _rms(_dot(gl, wdown_ref[...]), gpost_ref[...])
    pex = _dot(pe.astype(BF16), wple_ref[...])
    gate = _sigmoid_of_twice(_dot(x2.astype(BF16), wgate_ref[...]))
    return x2 + pex * gate


def _ffn_tile(x, pe, shifted, gpre_ref, wup_ref, *rest):
    a, u = _ffn_up(x, gpre_ref, wup_ref)
    a1, a2 = shifted(a)
    return _ffn_down(x, pe, a, u, a1, a2, *rest), a


def _ffn_body(x_ref, pe_ref, *refs, tm, strip):
    (gpre_ref, wup_ref), rest, (o_ref, tail_ref, carry_s) = refs[:2], refs[2:8], refs[8:]
    i = pl.program_id(1)

    @pl.when(i == 0)
    def _():
        carry_s[...] = jnp.zeros((CONV_W - 1, D_FF), F32)

    row = lax.broadcasted_iota(jnp.int32, (strip, D_FF), 0)
    nstrip = tm // strip
    rows = [slice(r * strip, (r + 1) * strip) for r in range(nstrip)]
    hist = carry_s[...]
    au = _ffn_up(x_ref[rows[0], :], gpre_ref, wup_ref)
    for r in range(nstrip):
        nxt = _ffn_up(x_ref[rows[r + 1], :], gpre_ref, wup_ref) if r + 1 < nstrip else None
        a, u = au
        c0, c1 = hist[0:1, :], hist[1:2, :]
        a1 = jnp.where(row == 0, c1, pltpu.roll(a, 1, 0))
        a2 = jnp.where(row == 0, c0, jnp.where(row == 1, c1, pltpu.roll(a, 2, 0)))
        o_ref[rows[r], :] = _ffn_down(x_ref[rows[r], :], pe_ref[rows[r], :], a, u, a1, a2, *rest)
        hist = a[strip - (CONV_W - 1):strip, :]
        au = nxt
    carry_s[...] = hist
    tail_ref[...] = hist


def _ffn_cached_body(x_ref, pe_ref, prev_ref, *refs, t, nseq):
    w_refs, (o_ref, tail_ref) = refs[:8], refs[8:]
    pos = lax.broadcasted_iota(jnp.int32, (nseq * t, D_FF), 0) & (t - 1)

    def history(j):
        return jnp.concatenate([jnp.broadcast_to(prev_ref[q, j:j + 1, :], (t, D_FF)) for q in range(nseq)],
                               axis=0)

    def shifted(a):
        p0, p1 = history(0), history(1)
        a1 = jnp.where(pos == 0, p1, pltpu.roll(a, 1, 0))
        a2 = jnp.where(pos == 0, p0, jnp.where(pos == 1, p1, pltpu.roll(a, 2, 0)))
        return a1, a2

    x = x_ref[...].reshape(nseq * t, D_MODEL)
    pe = pe_ref[...].reshape(nseq * t, PLE_DIM)
    out, a = _ffn_tile(x, pe, shifted, *w_refs)
    o_ref[...] = out.reshape(nseq, t, D_MODEL)
    for q in range(nseq):
        tail_ref[q] = a[(q + 1) * t - (CONV_W - 1):(q + 1) * t, :]


def _ffn_weight_specs():
    return [_const_spec((1, D_MODEL)), _const_spec((D_MODEL, 2 * D_FF)),
            _const_spec((CONV_W, D_FF)), _const_spec((1, D_FF)),
            _const_spec((D_FF, D_MODEL)), _const_spec((1, D_MODEL)),
            _const_spec((PLE_DIM, D_MODEL)), _const_spec((D_MODEL, D_MODEL))]


def _ffn(x, pe, weights, tm):
    bsz, t, _ = x.shape
    tail_spec = pl.BlockSpec((None, CONV_W - 1, D_FF), lambda b, i: (b, 0, 0))
    return pl.pallas_call(
        functools.partial(_ffn_body, tm=tm, strip=min(tm, 256)),
        grid=(bsz, t // tm),
        in_specs=[pl.BlockSpec((None, tm, D_MODEL), lambda b, i: (b, i, 0)),
                  pl.BlockSpec((None, tm, PLE_DIM), lambda b, i: (b, i, 0))] + _ffn_weight_specs(),
        out_specs=[pl.BlockSpec((None, tm, D_MODEL), lambda b, i: (b, i, 0)), tail_spec],
        out_shape=[jax.ShapeDtypeStruct((bsz, t, D_MODEL), F32),
                   jax.ShapeDtypeStruct((bsz, CONV_W - 1, D_FF), F32)],
        scratch_shapes=[pltpu.VMEM((CONV_W - 1, D_FF), F32)],
        compiler_params=pltpu.CompilerParams(
            dimension_semantics=("parallel", "arbitrary"), vmem_limit_bytes=_VMEM_LIMIT),
        name="convffn",
    )(x, pe, *weights)


def _ffn_cached(x, pe, prev, weights, nseq):
    bsz, t, _ = x.shape
    tail_spec = pl.BlockSpec((nseq, CONV_W - 1, D_FF), lambda b: (b, 0, 0))
    return pl.pallas_call(
        functools.partial(_ffn_cached_body, t=t, nseq=nseq),
        grid=(bsz // nseq,),
        in_specs=[pl.BlockSpec((nseq, t, D_MODEL), lambda b: (b, 0, 0)),
                  pl.BlockSpec((nseq, t, PLE_DIM), lambda b: (b, 0, 0)), tail_spec] + _ffn_weight_specs(),
        out_specs=[pl.BlockSpec((nseq, t, D_MODEL), lambda b: (b, 0, 0)), tail_spec],
        out_shape=[jax.ShapeDtypeStruct((bsz, t, D_MODEL), F32),
                   jax.ShapeDtypeStruct((bsz, CONV_W - 1, D_FF), F32)],
        compiler_params=pltpu.CompilerParams(
            dimension_semantics=("parallel",), vmem_limit_bytes=_VMEM_LIMIT),
        name="convffn_cached",
    )(x, pe, prev, *weights)


def _t5_bucket(rel):
    nb = NUM_BUCKETS // 2
    ret = jnp.where(rel > 0, nb, 0)
    n = jnp.abs(rel)
    max_exact = nb // 2
    large = max_exact + (jnp.log(jnp.maximum(n, max_exact).astype(jnp.float32) / max_exact)
                         / math.log(MAX_DISTANCE / max_exact) * (nb - max_exact)).astype(jnp.int32)
    large = jnp.minimum(large, nb - 1)
    return ret + jnp.where(n < max_exact, n, large)


def _bias_body(table_ref, bk_ref, o_ref, *, nvar):
    bk = bk_ref[...]
    row = lax.broadcasted_iota(jnp.int32, bk.shape, 0)
    for k in range(A_KV_HEADS):
        acc = jnp.zeros(bk.shape, F32)
        for b in range(NUM_BUCKETS):
            acc = jnp.where(bk == b, table_ref[k, b:b + 1, :], acc)
        for v in range(nvar):
            o_ref[v, k] = jnp.where(row < v * CHUNK, NEG_INF, acc)


def _bias_ext(table, sinks, lq, lk, nvar):
    q_pos = jnp.arange(lq) + WINDOW
    k_pos = jnp.arange(lk)
    buckets = _t5_bucket(k_pos[:, None] - q_pos[None, :]).astype(jnp.int32)
    bk = jnp.tile(buckets, (1, A_GROUP))
    tab = jnp.repeat(table.astype(F32).reshape(NUM_BUCKETS, A_KV_HEADS, A_GROUP), lq, axis=2)
    tab = jnp.transpose(tab, (1, 0, 2))
    snk = jnp.repeat(sinks.astype(F32).reshape(A_KV_HEADS, 1, A_GROUP), lq, axis=2)
    vmem = pl.BlockSpec(memory_space=pltpu.VMEM)
    bias = pl.pallas_call(
        functools.partial(_bias_body, nvar=nvar),
        in_specs=[vmem, vmem],
        out_specs=vmem,
        out_shape=jax.ShapeDtypeStruct((nvar, A_KV_HEADS, lk, A_GROUP * lq), F32),
        name="relbias",
    )(tab, bk)
    return bias, snk


def _ffn_weights(w):
    return [w[k] for k in ("g_pre_ffn", "w_up", "w_conv", "b_conv", "w_down", "g_post_ffn", "w_ple", "w_ple_gate")]


def _prompt_layer(x, pe, w, *, tm_mix, tm_tok, tm_ffn):
    bsz, t, _ = x.shape
    n = bsz * t
    x2d = x.reshape(n, D_MODEL)
    bias_ext, sink_ext = _bias_ext(w["rel_table"], w["sinks"], CHUNK, WINDOW + CHUNK, WINDOW // CHUNK + 1)
    kv, gg, ya, yb, s_fin = _mixer(x2d, w["g_pre_mix"], w["w_in"], w["lb_logits"], w["g_hgrn_out"],
                                   bias_ext, sink_ext, tm=tm_mix, seq=t)
    x1 = _merge(ya, yb, gg, x2d, w["w_br_a"], w["w_br_b"], w["w_out"], w["g_post_mix"], tm_tok)
    y, conv_tail = _ffn(x1.reshape(bsz, t, D_MODEL), pe, _ffn_weights(w), tm_ffn)
    return y, kv.reshape(bsz, t, 2 * A_KV_W), s_fin, conv_tail


def _sample_layer(x, pe, kv_prev, s_prev, conv_prev, w, *, tm_tok, nseq_mix):
    bsz, t, _ = x.shape
    n = bsz * t
    x2d = x.reshape(n, D_MODEL)
    qa, kv, hb, gg = _inproj(x2d, w["g_pre_mix"], w["w_in"], tm_tok)
    bias_ext, sink_ext = _bias_ext(w["rel_table"], w["sinks"], t, WINDOW + t, 1)
    kv3 = kv.reshape(bsz, t, 2 * A_KV_W)
    ya = _attention_cached(qa.reshape(bsz, t, A_Q_W), kv3, kv_prev, bias_ext, sink_ext, lq=t,
                           nseq=nseq_mix)
    yb, s_fin = _hgrn_cached(hb.reshape(bsz, t, _HB_W), w["lb_logits"], w["g_hgrn_out"], s_prev,
                             blk=t, nseq=nseq_mix)
    x1 = _merge(ya.reshape(n, A_Q_W), yb.reshape(n, B_VAL_W), gg, x2d,
                w["w_br_a"], w["w_br_b"], w["w_out"], w["g_post_mix"], tm_tok)
    y, conv_tail = _ffn_cached(x1.reshape(bsz, t, D_MODEL), pe, conv_prev, _ffn_weights(w), tm_tok // t)
    return y, kv3, s_fin, conv_tail


def _scale_in_cols(w_in):
    h = B_KEY_W
    scale = jnp.concatenate([
        jnp.full((A_Q_W,), A_HEAD_DIM ** -0.5, F32), jnp.ones((2 * A_KV_W,), F32),
        jnp.full((2 * h,), 0.5, F32), jnp.ones((B_VAL_W,), F32), jnp.full((B_VAL_W,), 0.5, F32),
        jnp.full((2 * D_MODEL,), 0.5, F32)])
    return w_in * scale[None, :]


def kernel(x_prompt, x_sample, cache_win_k, cache_win_v, state_hgrn, cache_ffn_conv, p_prompt, p_sample,
           rel_bias_table, lb_logits, g_pre_mix, w_in, attn_sinks, g_hgrn_out, w_br_a, w_br_b, w_out,
           g_post_mix, g_pre_ffn, w_up, w_conv, b_conv, w_down, g_post_ffn, w_ple, w_ple_gate):
    bsz, seq, _ = x_prompt.shape
    dbsz, dseq, _ = x_sample.shape
    w = {
        "rel_table": rel_bias_table, "sinks": attn_sinks[0], "lb_logits": lb_logits.astype(F32),
        "g_pre_mix": g_pre_mix[0][None, :], "w_in": _scale_in_cols(w_in[0]).astype(BF16),
        "g_hgrn_out": g_hgrn_out[0][None, :],
        "w_br_a": w_br_a[0].astype(BF16), "w_br_b": w_br_b[0].astype(BF16), "w_out": w_out[0].astype(BF16),
        "g_post_mix": g_post_mix[0][None, :], "g_pre_ffn": g_pre_ffn[0][None, :],
        "w_up": w_up[0].astype(BF16), "w_conv": w_conv[0], "b_conv": b_conv[0][None, :],
        "w_down": w_down[0].astype(BF16), "g_post_ffn": g_post_ffn[0][None, :],
        "w_ple": w_ple[0].astype(BF16), "w_ple_gate": (0.5 * w_ple_gate[0]).astype(BF16),
    }
    yp, kvp, sp, cp = _prompt_layer(x_prompt, p_prompt[0], w, tm_mix=512, tm_tok=1024, tm_ffn=512)
    wc = cache_win_k.shape[2]
    kv_cache = jnp.concatenate([cache_win_k[0].reshape(dbsz, wc, A_KV_W),
                                cache_win_v[0].reshape(dbsz, wc, A_KV_W)], axis=-1)
    ys, kvs, ss, cs = _sample_layer(x_sample, p_sample[0], kv_cache, state_hgrn[0], cache_ffn_conv[0], w,
                                    tm_tok=256, nseq_mix=8)
    keep = min(WINDOW, seq)

    def heads(a):
        return a.reshape(a.shape[0], a.shape[1], A_KV_HEADS, A_HEAD_DIM)[None]

    return (yp, ys,
            heads(kvp[:, seq - keep:, 0:A_KV_W]), heads(kvp[:, seq - keep:, A_KV_W:]),
            sp[None], cp[None],
            heads(kvs[:, :, 0:A_KV_W]), heads(kvs[:, :, A_KV_W:]),
            ss[None], cs[None])
```
